```python
import math
import jax, jax.numpy as jnp
from jax import lax
import numpy as np

D_MODEL = 2048
BATCH = 2
SEQ = 4096
DEPTH = 1

HEAD_DIM = 128
MIX_WIDTH = D_MODEL
N_RET_HEADS = MIX_WIDTH // HEAD_DIM // 2
N_DIFF_HEADS = MIX_WIDTH // HEAD_DIM // 2
RET_DK = HEAD_DIM
RET_DV = HEAD_DIM
RET_WIDTH = N_RET_HEADS * RET_DV
DIFF_DQK = HEAD_DIM // 2
DIFF_DV = HEAD_DIM
DIFF_WIDTH = N_DIFF_HEADS * DIFF_DV
D_FF = 4 * D_MODEL
RET_CHUNK = 128
Q_BLOCK = 128
ROPE_THETA = 500000.0
RET_THETA = 10000.0
DIFF_ROT_DIM = DIFF_DQK // 4
NORM_EPS = 1e-6
GN_EPS = 1e-5
IN_SPLITS = (RET_WIDTH, RET_WIDTH, RET_WIDTH, RET_WIDTH,
             N_DIFF_HEADS * 2 * DIFF_DQK, N_DIFF_HEADS * 2 * DIFF_DQK, DIFF_WIDTH)
IN_WIDTH = sum(IN_SPLITS)

kernel_name = "hybrid_retention_diffattn_block"


def lambda_init(layer_idx):
    return 0.8 - 0.6 * math.exp(-0.3 * layer_idx)


def rms_norm(x, w, eps=NORM_EPS):
    xf = x.astype(jnp.float32)
    y = xf * lax.rsqrt(jnp.mean(xf * xf, axis=-1, keepdims=True) + eps)
    return (y * w.astype(jnp.float32)).astype(x.dtype)


def modulate(h, shift, scale):
    return h * (1.0 + scale[:, None, :]) + shift[:, None, :]


def rotary(x, rot_dim, base):
    T = x.shape[-2]
    pos = jnp.arange(T, dtype=jnp.float32)
    inv = base ** (-jnp.arange(0, rot_dim, 2, dtype=jnp.float32) / rot_dim)
    ang = pos[:, None] * inv[None, :]
    cos = jnp.cos(ang).astype(x.dtype)
    sin = jnp.sin(ang).astype(x.dtype)
    half = rot_dim // 2
    x1 = x[..., :half]
    x2 = x[..., half:rot_dim]
    return jnp.concatenate([x1 * cos - x2 * sin, x2 * cos + x1 * sin, x[..., rot_dim:]], axis=-1)


def retention_chunkwise(q, k, v, log_gamma):
    B, H, T, dk = q.shape
    dv = v.shape[-1]
    C = RET_CHUNK
    NC = T // C
    idx = jnp.arange(C, dtype=jnp.float32)
    rel = idx[:, None] - idx[None, :]
    lg = log_gamma[:, None, None]
    decay = jnp.where(rel[None] >= 0, jnp.exp(jnp.maximum(rel, 0.0)[None] * lg), 0.0)
    xi = jnp.exp((idx + 1.0)[None, :] * log_gamma[:, None])
    zeta = jnp.exp((C - 1.0 - idx)[None, :] * log_gamma[:, None])
    chunk_decay = jnp.exp(C * log_gamma)

    def to_chunks(t):
        return t.reshape(B, H, NC, C, t.shape[-1]).transpose(2, 0, 1, 3, 4)

    def step(S, inp):
        qc, kc, vc = inp
        inner = jnp.einsum('bhid,bhjd->bhij', qc, kc) * decay[None]
        o = jnp.einsum('bhij,bhjv->bhiv', inner, vc)
        o = o + jnp.einsum('bhid,bhdv->bhiv', qc, S) * xi[None, :, :, None]
        S = S * chunk_decay[None, :, None, None] + jnp.einsum(
            'bhjd,bhjv->bhdv', kc * zeta[None, :, :, None], vc)
        return S, o

    S0 = jnp.zeros((B, H, dk, dv), jnp.float32)
    _, o = lax.scan(step, S0, (to_chunks(q), to_chunks(k), to_chunks(v)))
    return o.transpose(1, 2, 0, 3, 4).reshape(B, H, T, dv)


def diff_attention(q, k, v, lam):
    T = q.shape[3]
    nb = T // Q_BLOCK
    scale = DIFF_DQK ** -0.5
    kpos = jnp.arange(T)

    def block(i):
        qb = lax.dynamic_slice_in_dim(q, i * Q_BLOCK, Q_BLOCK, axis=3)
        s = jnp.einsum('bhcqd,bhckd->bhcqk', qb, k) * scale
        qpos = i * Q_BLOCK + jnp.arange(Q_BLOCK)
        mask = kpos[None, :] <= qpos[:, None]
        s = jnp.where(mask, s, -jnp.inf)
        p = jax.nn.softmax(s, axis=-1)
        a = p[:, :, 0] - lam * p[:, :, 1]
        return jnp.einsum('bhqk,bhkd->bhqd', a, v)

    o = lax.map(block, jnp.arange(nb))
    B, H = q.shape[0], q.shape[1]
    return o.transpose(1, 2, 0, 3, 4).reshape(B, H, T, v.shape[-1])


def setup_inputs(seed: int = 0) -> dict:
    key = jax.random.key(seed)
    ks = jax.random.split(key, 18)
    f32 = jnp.float32
    x = jax.random.normal(ks[0], (BATCH, SEQ, D_MODEL), f32)
    c = jax.random.normal(ks[1], (BATCH, D_MODEL), f32)
    w_ada = jax.random.normal(ks[2], (DEPTH, D_MODEL, 6 * D_MODEL), f32) * (0.5 * D_MODEL ** -0.5)
    b_ada = jax.random.normal(ks[3], (DEPTH, 6 * D_MODEL), f32) * 0.02
    norm1_w = 1.0 + 0.05 * jax.random.normal(ks[4], (DEPTH, D_MODEL), f32)
    norm2_w = 1.0 + 0.05 * jax.random.normal(ks[5], (DEPTH, D_MODEL), f32)
    w_in = jax.random.normal(ks[6], (DEPTH, D_MODEL, IN_WIDTH), f32) * D_MODEL ** -0.5
    ret_gn_w = 1.0 + 0.05 * jax.random.normal(ks[7], (DEPTH, RET_WIDTH), f32)
    diff_lq1 = 0.1 * jax.random.normal(ks[8], (DEPTH, DIFF_DQK), f32)
    diff_lk1 = 0.1 * jax.random.normal(ks[9], (DEPTH, DIFF_DQK), f32)
    diff_lq2 = 0.1 * jax.random.normal(ks[10], (DEPTH, DIFF_DQK), f32)
    diff_lk2 = 0.1 * jax.random.normal(ks[11], (DEPTH, DIFF_DQK), f32)
    diff_subln_w = 1.0 + 0.05 * jax.random.normal(ks[12], (DEPTH, DIFF_WIDTH), f32)
    w_out = jax.random.normal(ks[13], (DEPTH, MIX_WIDTH, D_MODEL), f32) * MIX_WIDTH ** -0.5
    w_mlp1 = jax.random.normal(ks[14], (DEPTH, D_MODEL, D_FF), f32) * D_MODEL ** -0.5
    w_mlp2 = jax.random.normal(ks[15], (DEPTH, D_FF, D_MODEL), f32) * D_FF ** -0.5
    final_norm_w = 1.0 + 0.05 * jax.random.normal(ks[16], (D_MODEL,), f32)
    return {"x": x, "c": c, "w_ada": w_ada, "b_ada": b_ada, "norm1_w": norm1_w,
            "norm2_w": norm2_w, "w_in": w_in, "ret_gn_w": ret_gn_w,
            "diff_lq1": diff_lq1, "diff_lk1": diff_lk1, "diff_lq2": diff_lq2,
            "diff_lk2": diff_lk2, "diff_subln_w": diff_subln_w, "w_out": w_out,
            "w_mlp1": w_mlp1, "w_mlp2": w_mlp2, "final_norm_w": final_norm_w}


def reference(x, c, w_ada, b_ada, norm1_w, norm2_w, w_in, ret_gn_w, diff_lq1, diff_lk1,
              diff_lq2, diff_lk2, diff_subln_w, w_out, w_mlp1, w_mlp2, final_norm_w):
    B, T, _ = x.shape
    f32 = jnp.float32
    log_gamma = jnp.log(1.0 - 2.0 ** (-5.0 - jnp.arange(N_RET_HEADS, dtype=f32)))
    split_pts = list(np.cumsum(IN_SPLITS)[:-1])
    silu_c = jax.nn.silu(c)

    for l in range(DEPTH):
        mod = silu_c @ w_ada[l] + b_ada[l]
        sh1, sc1, g1, sh2, sc2, g2 = jnp.split(mod, 6, axis=-1)

        h = modulate(rms_norm(x, norm1_w[l]), sh1, sc1)
        proj = h @ w_in[l]
        rq, rk, rv, rg, dq, dk, dv = jnp.split(proj, split_pts, axis=-1)

        def heads(t, d):
            return t.reshape(B, T, -1, d).transpose(0, 2, 1, 3).astype(f32)
        rq_h = rotary(heads(rq, RET_DK), RET_DK, RET_THETA)
        rk_h = rotary(heads(rk, RET_DK), RET_DK, RET_THETA) * (RET_DK ** -0.5)
        rv_h = heads(rv, RET_DV)
        y_ret = retention_chunkwise(rq_h, rk_h, rv_h, log_gamma)
        mu = jnp.mean(y_ret, axis=-1, keepdims=True)
        var = jnp.mean(jnp.square(y_ret - mu), axis=-1, keepdims=True)
        y_ret = (y_ret - mu) * lax.rsqrt(var + GN_EPS)
        y_ret = y_ret.transpose(0, 2, 1, 3).reshape(B, T, RET_WIDTH) * ret_gn_w[l].astype(f32)
        y_ret = y_ret * jax.nn.silu(rg.astype(f32))

        dq_h = dq.reshape(B, T, N_DIFF_HEADS, 2, DIFF_DQK).transpose(0, 2, 3, 1, 4).astype(f32)
        dk_h = dk.reshape(B, T, N_DIFF_HEADS, 2, DIFF_DQK).transpose(0, 2, 3, 1, 4).astype(f32)
        dq_h = rotary(dq_h, DIFF_ROT_DIM, ROPE_THETA)
        dk_h = rotary(dk_h, DIFF_ROT_DIM, ROPE_THETA)
        dv_h = heads(dv, DIFF_DV)
        lam_init = lambda_init(l)
        lam = (jnp.exp(jnp.sum(diff_lq1[l].astype(f32) * diff_lk1[l].astype(f32)))
               - jnp.exp(jnp.sum(diff_lq2[l].astype(f32) * diff_lk2[l].astype(f32)))
               + lam_init)
        y_dif = diff_attention(dq_h, dk_h, dv_h, lam)
        y_dif = y_dif * lax.rsqrt(jnp.mean(y_dif * y_dif, axis=-1, keepdims=True) + GN_EPS)
        y_dif = y_dif.transpose(0, 2, 1, 3).reshape(B, T, DIFF_WIDTH)
        y_dif = y_dif * diff_subln_w[l].astype(f32) * (1.0 - lam_init)

        mixed = jnp.concatenate([y_ret, y_dif], axis=-1).astype(x.dtype) @ w_out[l]
        x = x + g1[:, None, :] * mixed

        h2 = modulate(rms_norm(x, norm2_w[l]), sh2, sc2)
        mlp = jnp.square(jax.nn.relu(h2 @ w_mlp1[l])) @ w_mlp2[l]
        x = x + g2[:, None, :] * mlp

    return rms_norm(x, final_norm_w)
```

```python
import functools
import math

import jax
import jax.numpy as jnp
from jax import lax
from jax.experimental import pallas as pl
from jax.experimental.pallas import tpu as pltpu

F32 = jnp.float32
BF16 = jnp.bfloat16

HEAD_DIM = 128
N_RET_HEADS = 8
N_DIFF_HEADS = 8
DIFF_DQK = HEAD_DIM // 2
DIFF_ROT_DIM = DIFF_DQK // 4
RET_CHUNK = 128
ROPE_THETA = 500000.0
RET_THETA = 10000.0
NORM_EPS = 1e-6
GN_EPS = 1e-5

V7X_VMEM_LIMIT_BYTES = 58 * 1024 * 1024


def _silu(v):
    return v * jax.nn.sigmoid(v)


def _ada_kernel(c_ref, w_ref, b_ref, o_ref):
    s = _silu(c_ref[...]).astype(BF16)
    o_ref[...] = jnp.dot(s, w_ref[...].astype(BF16), preferred_element_type=F32) + b_ref[...]


def _ada(c_pad, w_ada, b_ada, *, bn=1024):
    rows, d = c_pad.shape
    n = w_ada.shape[1]
    return pl.pallas_call(
        _ada_kernel,
        out_shape=jax.ShapeDtypeStruct((rows, n), F32),
        grid=(n // bn,),
        in_specs=[
            pl.BlockSpec((rows, d), lambda j: (0, 0)),
            pl.BlockSpec((d, bn), lambda j: (0, j)),
            pl.BlockSpec((1, bn), lambda j: (0, j)),
        ],
        out_specs=pl.BlockSpec((rows, bn), lambda j: (0, j)),
        compiler_params=pltpu.CompilerParams(
            dimension_semantics=("arbitrary",), vmem_limit_bytes=V7X_VMEM_LIMIT_BYTES),
        name="ada",
    )(c_pad, w_ada, b_ada)


_PROJ_RC = 256
_PROJ_CC = 256


def _proj_kernel(x_ref, sh_ref, sc_ref, nw_ref, w_ref, rc_ref, rs_ref, dc_ref, da_ref, db_ref,
                 o_ref, h_ref, *, bm, bn):
    j = pl.program_id(1)
    n_rc = bm // _PROJ_RC

    @pl.when(j == 0)
    def _():
        def norm_rows(r, carry):
            rows = pl.ds(pl.multiple_of(r * _PROJ_RC, _PROJ_RC), _PROJ_RC)
            x = x_ref[rows, :]
            ms = jnp.mean(x * x, axis=-1, keepdims=True)
            y = x * lax.rsqrt(ms + NORM_EPS) * nw_ref[...]
            h_ref[rows, :] = (y * (1.0 + sc_ref[0]) + sh_ref[0]).astype(BF16)
            return carry
        lax.fori_loop(0, n_rc, norm_rows, 0)

    def run(epilogue):
        def body(r, carry):
            rows = pl.ds(pl.multiple_of(r * _PROJ_RC, _PROJ_RC), _PROJ_RC)
            h = h_ref[rows, :]
            for cc in range(bn // _PROJ_CC):
                acc = jnp.dot(h, w_ref[:, cc * _PROJ_CC:(cc + 1) * _PROJ_CC],
                              preferred_element_type=F32)
                for hb in range(_PROJ_CC // HEAD_DIM):
                    a = acc[:, hb * HEAD_DIM:(hb + 1) * HEAD_DIM]
                    c0 = cc * _PROJ_CC + hb * HEAD_DIM
                    o_ref[rows, c0:c0 + HEAD_DIM] = epilogue(a, rows).astype(o_ref.dtype)
            return carry
        lax.fori_loop(0, n_rc, body, 0)

    @pl.when(j < 2)
    def _():
        scale = jnp.where(j == 1, HEAD_DIM ** -0.5, 1.0).astype(F32)

        def ep(a, rows):
            rot = a * rc_ref[rows, :] + pltpu.roll(a, HEAD_DIM // 2, 1) * rs_ref[rows, :]
            return rot * scale
        run(ep)

    @pl.when(jnp.logical_or(j == 4, j == 5))
    def _():
        scale = jnp.where(j == 4, DIFF_DQK ** -0.5, 1.0).astype(F32)
        half = DIFF_ROT_DIM // 2

        def ep(a, rows):
            rot = (a * dc_ref[rows, :]
                   + pltpu.roll(a, HEAD_DIM - half, 1) * da_ref[rows, :]
                   + pltpu.roll(a, half, 1) * db_ref[rows, :])
            return rot * scale
        run(ep)

    @pl.when(jnp.logical_or(jnp.logical_or(j == 2, j == 3), j == 6))
    def _():
        run(lambda a, rows: a)


def _proj(x2d, mod3, norm_w, w_in_bf16, tabs, *, seq, bm=1024, bn=1024):
    bt, d = x2d.shape
    n = w_in_bf16.shape[1]
    tiles_per_seq = seq // bm
    tab_spec = pl.BlockSpec((bm, HEAD_DIM), lambda m, j: (m % tiles_per_seq, 0))
    return pl.pallas_call(
        functools.partial(_proj_kernel, bm=bm, bn=bn),
        out_shape=jax.ShapeDtypeStruct((bt, n), BF16),
        grid=(bt // bm, n // bn),
        in_specs=[
            pl.BlockSpec((bm, d), lambda m, j: (m, 0)),
            pl.BlockSpec((1, 1, d), lambda m, j: ((m // tiles_per_seq) * 6 + 0, 0, 0)),
            pl.BlockSpec((1, 1, d), lambda m, j: ((m // tiles_per_seq) * 6 + 1, 0, 0)),
            pl.BlockSpec((1, d), lambda m, j: (0, 0)),
            pl.BlockSpec((d, bn), lambda m, j: (0, j)),
            tab_spec, tab_spec, tab_spec, tab_spec, tab_spec,
        ],
        out_specs=pl.BlockSpec((bm, bn), lambda m, j: (m, j)),
        scratch_shapes=[pltpu.VMEM((bm, d), BF16)],
        compiler_params=pltpu.CompilerParams(
            dimension_semantics=("arbitrary", "arbitrary"), vmem_limit_bytes=V7X_VMEM_LIMIT_BYTES),
        name="proj",
    )(x2d, mod3, mod3, norm_w, w_in_bf16, *tabs)


def _rotary_tables(seq):
    pos = jnp.arange(seq, dtype=F32)
    inv = RET_THETA ** (-jnp.arange(0, HEAD_DIM, 2, dtype=F32) / HEAD_DIM)
    ang = pos[:, None] * inv[None, :]
    cos, sin = jnp.cos(ang), jnp.sin(ang)
    ret_c = jnp.concatenate([cos, cos], axis=-1)
    ret_s = jnp.concatenate([-sin, sin], axis=-1)
    half = DIFF_ROT_DIM // 2
    inv = ROPE_THETA ** (-jnp.arange(0, DIFF_ROT_DIM, 2, dtype=F32) / DIFF_ROT_DIM)
    ang = pos[:, None] * inv[None, :]
    cos, sin = jnp.cos(ang), jnp.sin(ang)
    ones = jnp.ones((seq, DIFF_DQK - DIFF_ROT_DIM), F32)
    zeros = jnp.zeros((seq, DIFF_DQK - DIFF_ROT_DIM), F32)
    zh = jnp.zeros((seq, half), F32)
    comp_c = jnp.concatenate([cos, cos, ones], axis=-1)
    comp_a = jnp.concatenate([-sin, zh, zeros], axis=-1)
    comp_b = jnp.concatenate([zh, sin, zeros], axis=-1)
    dif_c = jnp.concatenate([comp_c, comp_c], axis=-1)
    dif_a = jnp.concatenate([comp_a, comp_a], axis=-1)
    dif_b = jnp.concatenate([comp_b, comp_b], axis=-1)
    return ret_c, ret_s, dif_c, dif_a, dif_b


def _ret_kernel(q_ref, k_ref, v_ref, g_ref, gw_ref, o_ref, s_ref, *, seq):
    C = RET_CHUNK
    nc = seq // C
    head = (pl.program_id(0) % N_RET_HEADS).astype(F32)
    ii = lax.broadcasted_iota(jnp.int32, (C, C), 0).astype(F32)
    jj = lax.broadcasted_iota(jnp.int32, (C, C), 1).astype(F32)
    lg = jnp.log(1.0 - jnp.exp2(jnp.zeros((C, C), F32) - 5.0 - head))
    rel = ii - jj
    decay = jnp.where(rel >= 0, jnp.exp(jnp.maximum(rel, 0.0) * lg), 0.0)
    xi = jnp.exp((ii + 1.0) * lg)
    zeta = jnp.exp((C - 1.0 - ii) * lg)
    chunk_decay = jnp.exp(C * lg)

    def scan_body(c, state):
        rows = pl.ds(pl.multiple_of(c * C, C), C)
        s_ref[c] = state.astype(BF16)
        kz = (k_ref[rows, :].astype(F32) * zeta).astype(BF16)
        kv = lax.dot_general(kz, v_ref[rows, :], (((0,), (0,)), ((), ())),
                             preferred_element_type=F32)
        return state * chunk_decay + kv
    lax.fori_loop(0, nc, scan_body, jnp.zeros((C, C), F32), unroll=2)

    gw = gw_ref[...]

    def out_body(c, carry):
        rows = pl.ds(pl.multiple_of(c * C, C), C)
        q = q_ref[rows, :]
        inner = lax.dot_general(q, k_ref[rows, :], (((1,), (1,)), ((), ())),
                                preferred_element_type=F32) * decay
        o = jnp.dot(inner.astype(BF16), v_ref[rows, :], preferred_element_type=F32)
        o = o + jnp.dot(q, s_ref[c], preferred_element_type=F32) * xi
        mu = jnp.mean(o, axis=-1, keepdims=True)
        d = o - mu
        var = jnp.mean(d * d, axis=-1, keepdims=True)
        y = d * lax.rsqrt(var + GN_EPS) * gw
        o_ref[rows, :] = (y * _silu(g_ref[rows, :].astype(F32))).astype(o_ref.dtype)
        return carry
    lax.fori_loop(0, nc, out_body, 0, unroll=2)


def _retention(proj, gn_w, *, batch, seq):
    nh = N_RET_HEADS
    blk = lambda off: pl.BlockSpec((seq, HEAD_DIM), lambda s: (s // nh, off + s % nh))
    return pl.pallas_call(
        functools.partial(_ret_kernel, seq=seq),
        out_shape=jax.ShapeDtypeStruct((batch * seq, nh * HEAD_DIM), BF16),
        grid=(batch * nh,),
        in_specs=[blk(0), blk(nh), blk(2 * nh), blk(3 * nh),
                  pl.BlockSpec((1, HEAD_DIM), lambda s: (0, s % nh))],
        out_specs=pl.BlockSpec((seq, HEAD_DIM), lambda s: (s // nh, s % nh)),
        scratch_shapes=[pltpu.VMEM((seq // RET_CHUNK, RET_CHUNK, HEAD_DIM), BF16)],
        compiler_params=pltpu.CompilerParams(
            dimension_semantics=("arbitrary",), vmem_limit_bytes=V7X_VMEM_LIMIT_BYTES),
        name="retention",
    )(proj, proj, proj, proj, gn_w)


def _attn_kernel(q_ref, k_ref, v_ref, lq1_ref, lk1_ref, lq2_ref, lk2_ref, sw_ref, o_ref,
                 *, blk, lam_init):
    i = pl.program_id(1)
    q = q_ref[...]
    lane = lax.broadcasted_iota(jnp.int32, q.shape, 1)
    zero = jnp.zeros_like(q)
    qs = (jnp.where(lane < DIFF_DQK, q, zero), jnp.where(lane >= DIFF_DQK, q, zero))

    def step(j, carry, masked):
        rows = pl.ds(pl.multiple_of(j * blk, blk), blk)
        kb = k_ref[rows, :]
        vb = v_ref[rows, :]
        out = []
        for comp in range(2):
            m, l, acc = carry[comp]
            s = lax.dot_general(qs[comp], kb, (((1,), (1,)), ((), ())),
                                preferred_element_type=F32)
            if masked:
                r = lax.broadcasted_iota(jnp.int32, s.shape, 0)
                c = lax.broadcasted_iota(jnp.int32, s.shape, 1)
                s = jnp.where(c <= r, s, -jnp.inf)
            m_new = jnp.maximum(m, jnp.max(s, axis=-1, keepdims=True))
            alpha = jnp.exp(m - m_new)
            p = jnp.exp(s - m_new)
            l = alpha * l + jnp.sum(p, axis=-1, keepdims=True)
            acc = alpha * acc + jnp.dot(p.astype(BF16), vb, preferred_element_type=F32)
            out.append((m_new, l, acc))
        return tuple(out)

    init = tuple((jnp.full((blk, 1), -jnp.inf, F32), jnp.zeros((blk, 1), F32),
                  jnp.zeros((blk, HEAD_DIM), F32)) for _ in range(2))
    carry = lax.fori_loop(0, i, lambda j, c: step(j, c, False), init)
    (_, l0, a0), (_, l1, a1) = step(i, carry, True)

    lam = (jnp.exp(jnp.sum(lq1_ref[...] * lk1_ref[...], axis=-1, keepdims=True))
           - jnp.exp(jnp.sum(lq2_ref[...] * lk2_ref[...], axis=-1, keepdims=True))
           + lam_init)
    y = a0 / l0 - lam * (a1 / l1)
    y = y * lax.rsqrt(jnp.mean(y * y, axis=-1, keepdims=True) + GN_EPS)
    o_ref[...] = (y * sw_ref[...] * (1.0 - lam_init)).astype(o_ref.dtype)


def _diff_attention(proj, lq1, lk1, lq2, lk2, subln_w, *, batch, seq, lam_init, blk=512):
    nh = N_DIFF_HEADS
    nq = seq // blk
    q_off, k_off, v_off = 4 * nh, 5 * nh, 6 * nh
    lam_spec = pl.BlockSpec((1, DIFF_DQK), lambda s, i: (0, 0))
    return pl.pallas_call(
        functools.partial(_attn_kernel, blk=blk, lam_init=lam_init),
        out_shape=jax.ShapeDtypeStruct((batch * seq, nh * HEAD_DIM), BF16),
        grid=(batch * nh, nq),
        in_specs=[
            pl.BlockSpec((blk, HEAD_DIM), lambda s, i: ((s // nh) * nq + i, q_off + s % nh)),
            pl.BlockSpec((seq, HEAD_DIM), lambda s, i: (s // nh, k_off + s % nh)),
            pl.BlockSpec((seq, HEAD_DIM), lambda s, i: (s // nh, v_off + s % nh)),
            lam_spec, lam_spec, lam_spec, lam_spec,
            pl.BlockSpec((1, HEAD_DIM), lambda s, i: (0, s % nh)),
        ],
        out_specs=pl.BlockSpec((blk, HEAD_DIM), lambda s, i: ((s // nh) * nq + i, s % nh)),
        compiler_params=pltpu.CompilerParams(
            dimension_semantics=("arbitrary", "arbitrary"), vmem_limit_bytes=V7X_VMEM_LIMIT_BYTES),
        name="diff_attn",
    )(proj, proj, proj, lq1, lk1, lq2, lk2, subln_w)


_OUT_RC = 256


def _outproj_kernel(yr_ref, yd_ref, w_ref, x_ref, g_ref, sh_ref, sc_ref, nw_ref,
                    x1_ref, h2_ref, *, bm, wr):
    def body(r, carry):
        rows = pl.ds(pl.multiple_of(r * _OUT_RC, _OUT_RC), _OUT_RC)
        mixed = (jnp.dot(yr_ref[rows, :], w_ref[:wr, :], preferred_element_type=F32)
                 + jnp.dot(yd_ref[rows, :], w_ref[wr:, :], preferred_element_type=F32))
        x1 = x_ref[rows, :] + g_ref[0] * mixed
        x1_ref[rows, :] = x1
        ms = jnp.mean(x1 * x1, axis=-1, keepdims=True)
        y = x1 * lax.rsqrt(ms + NORM_EPS) * nw_ref[...]
        h2_ref[rows, :] = (y * (1.0 + sc_ref[0]) + sh_ref[0]).astype(BF16)
        return carry
    lax.fori_loop(0, bm // _OUT_RC, body, 0)


def _outproj(y_ret, y_dif, w_out_bf16, x2d, mod3, norm_w, *, seq, bm=512):
    bt, d = x2d.shape
    wr = y_ret.shape[1]
    wd = y_dif.shape[1]
    tiles_per_seq = seq // bm
    mod_spec = lambda idx: pl.BlockSpec(
        (1, 1, d), lambda m: ((m // tiles_per_seq) * 6 + idx, 0, 0))
    return pl.pallas_call(
        functools.partial(_outproj_kernel, bm=bm, wr=wr),
        out_shape=(jax.ShapeDtypeStruct((bt, d), F32), jax.ShapeDtypeStruct((bt, d), BF16)),
        grid=(bt // bm,),
        in_specs=[
            pl.BlockSpec((bm, wr), lambda m: (m, 0)),
            pl.BlockSpec((bm, wd), lambda m: (m, 0)),
            pl.BlockSpec((wr + wd, d), lambda m: (0, 0)),
            pl.BlockSpec((bm, d), lambda m: (m, 0)),
            mod_spec(2), mod_spec(3), mod_spec(4),
            pl.BlockSpec((1, d), lambda m: (0, 0)),
        ],
        out_specs=(pl.BlockSpec((bm, d), lambda m: (m, 0)), pl.BlockSpec((bm, d), lambda m: (m, 0))),
        compiler_params=pltpu.CompilerParams(
            dimension_semantics=("arbitrary",), vmem_limit_bytes=V7X_VMEM_LIMIT_BYTES),
        name="outproj",
    )(y_ret, y_dif, w_out_bf16, x2d, mod3, mod3, mod3, norm_w)


_MLP_RC = 256


def _mlp_kernel(h_ref, x1_ref, w1_ref, w2_ref, g_ref, fw_ref, o_ref, *, bm, final_norm):
    f = pl.program_id(1)
    nf = pl.num_programs(1)

    def body(r, carry):
        rows = pl.ds(pl.multiple_of(r * _MLP_RC, _MLP_RC), _MLP_RC)
        a = jnp.dot(h_ref[rows, :], w1_ref[...], preferred_element_type=F32)
        a = jnp.maximum(a, 0.0)
        a = (a * a).astype(BF16)
        part = jnp.dot(a, w2_ref[...], preferred_element_type=F32)

        @pl.when(f == 0)
        def _():
            o_ref[rows, :] = part

        @pl.when(f > 0)
        def _():
            o_ref[rows, :] += part
        return carry
    lax.fori_loop(0, bm // _MLP_RC, body, 0)

    @pl.when(f == nf - 1)
    def _():
        def fin(r, carry):
            rows = pl.ds(pl.multiple_of(r * _MLP_RC, _MLP_RC), _MLP_RC)
            x2 = x1_ref[rows, :] + g_ref[0] * o_ref[rows, :]
            if final_norm:
                ms = jnp.mean(x2 * x2, axis=-1, keepdims=True)
                x2 = x2 * lax.rsqrt(ms + NORM_EPS) * fw_ref[...]
            o_ref[rows, :] = x2
            return carry
        lax.fori_loop(0, bm // _MLP_RC, fin, 0)


def _mlp(h2, x1, w1_bf16, w2_bf16, mod3, final_w, *, seq, final_norm, bm=1024, bf=512):
    bt, d = x1.shape
    dff = w1_bf16.shape[1]
    tiles_per_seq = seq // bm
    return pl.pallas_call(
        functools.partial(_mlp_kernel, bm=bm, final_norm=final_norm),
        out_shape=jax.ShapeDtypeStruct((bt, d), F32),
        grid=(bt // bm, dff // bf),
        in_specs=[
            pl.BlockSpec((bm, d), lambda m, f: (m, 0)),
            pl.BlockSpec((bm, d), lambda m, f: (m, 0)),
            pl.BlockSpec((d, bf), lambda m, f: (0, f)),
            pl.BlockSpec((bf, d), lambda m, f: (f, 0)),
            pl.BlockSpec((1, 1, d), lambda m, f: ((m // tiles_per_seq) * 6 + 5, 0, 0)),
            pl.BlockSpec((1, d), lambda m, f: (0, 0)),
        ],
        out_specs=pl.BlockSpec((bm, d), lambda m, f: (m, 0)),
        compiler_params=pltpu.CompilerParams(
            dimension_semantics=("arbitrary", "arbitrary"), vmem_limit_bytes=V7X_VMEM_LIMIT_BYTES),
        name="mlp",
    )(h2, x1, w1_bf16, w2_bf16, mod3, final_w)


def kernel(x, c, w_ada, b_ada, norm1_w, norm2_w, w_in, ret_gn_w, diff_lq1, diff_lk1, diff_lq2,
           diff_lk2, diff_subln_w, w_out, w_mlp1, w_mlp2, final_norm_w):
    batch, seq, d = x.shape
    depth = w_ada.shape[0]
    c_pad = jnp.pad(c, ((0, 8 - batch), (0, 0)))
    tabs = _rotary_tables(seq)
    x2d = x.reshape(batch * seq, d)
    final_w = final_norm_w.reshape(1, d)

    for l in range(depth):
        lam_init = 0.8 - 0.6 * math.exp(-0.3 * l)
        mod = _ada(c_pad, w_ada[l], b_ada[l].reshape(1, -1))
        mod3 = mod[:batch].reshape(batch * 6, 1, d)
        proj = _proj(x2d, mod3, norm1_w[l].reshape(1, d), w_in[l].astype(BF16), tabs, seq=seq)
        y_ret = _retention(proj, ret_gn_w[l].reshape(1, -1), batch=batch, seq=seq)
        y_dif = _diff_attention(
            proj, diff_lq1[l].reshape(1, -1), diff_lk1[l].reshape(1, -1),
            diff_lq2[l].reshape(1, -1), diff_lk2[l].reshape(1, -1),
            diff_subln_w[l].reshape(1, -1), batch=batch, seq=seq, lam_init=lam_init)
        x1, h2 = _outproj(y_ret, y_dif, w_out[l].astype(BF16), x2d, mod3,
                          norm2_w[l].reshape(1, d), seq=seq)
        x2d = _mlp(h2, x1, w_mlp1[l].astype(BF16), w_mlp2[l].astype(BF16), mod3, final_w,
                   seq=seq, final_norm=(l == depth - 1))
    if depth == 0:
        raise ValueError("depth must be positive")
    return x2d.reshape(batch, seq, d)
```

```python
import functools
import math

import jax
import jax.numpy as jnp
from jax import lax
from jax.experimental import pallas as pl
from jax.experimental.pallas import tpu as pltpu

F32 = jnp.float32
BF16 = jnp.bfloat16

HEAD_DIM = 128
N_RET_HEADS = 8
N_DIFF_HEADS = 8
DIFF_DQK = HEAD_DIM // 2
DIFF_ROT_DIM = DIFF_DQK // 4
RET_CHUNK = 128
ROPE_THETA = 500000.0
RET_THETA = 10000.0
NORM_EPS = 1e-6
GN_EPS = 1e-5

V7X_VMEM_LIMIT_BYTES = 58 * 1024 * 1024


def _silu(v):
    return v * jax.nn.sigmoid(v)


def _ada_kernel(c_ref, w_ref, b_ref, o_ref):
    s = _silu(c_ref[...]).astype(BF16)
    o_ref[...] = jnp.dot(s, w_ref[...].astype(BF16), preferred_element_type=F32) + b_ref[...]


def _ada(c_pad, w_ada, b_ada, *, bn=1024):
    rows, d = c_pad.shape
    n = w_ada.shape[1]
    return pl.pallas_call(
        _ada_kernel,
        out_shape=jax.ShapeDtypeStruct((rows, n), F32),
        grid=(n // bn,),
        in_specs=[
            pl.BlockSpec((rows, d), lambda j: (0, 0)),
            pl.BlockSpec((d, bn), lambda j: (0, j)),
            pl.BlockSpec((1, bn), lambda j: (0, j)),
        ],
        out_specs=pl.BlockSpec((rows, bn), lambda j: (0, j)),
        compiler_params=pltpu.CompilerParams(
            dimension_semantics=("arbitrary",), vmem_limit_bytes=V7X_VMEM_LIMIT_BYTES),
        name="ada",
    )(c_pad, w_ada, b_ada)


_PROJ_RC = 512
_PROJ_CC = 256


def _proj_kernel(x_ref, sh_ref, sc_ref, nw_ref, w_ref, rc_ref, rs_ref, dc_ref, da_ref, db_ref,
                 o_ref, h_ref, *, bm, bn):
    j = pl.program_id(1)
    n_rc = bm // _PROJ_RC

    @pl.when(j == 0)
    def _():
        def norm_rows(r, carry):
            rows = pl.ds(pl.multiple_of(r * _PROJ_RC, _PROJ_RC), _PROJ_RC)
            x = x_ref[rows, :]
            ms = jnp.mean(x * x, axis=-1, keepdims=True)
            y = x * lax.rsqrt(ms + NORM_EPS) * nw_ref[...]
            h_ref[rows, :] = (y * (1.0 + sc_ref[0]) + sh_ref[0]).astype(BF16)
            return carry
        lax.fori_loop(0, n_rc, norm_rows, 0)

    def run(epilogue):
        def body(r, carry):
            rows = pl.ds(pl.multiple_of(r * _PROJ_RC, _PROJ_RC), _PROJ_RC)
            h = h_ref[rows, :]
            for cc in range(bn // _PROJ_CC):
                acc = jnp.dot(h, w_ref[:, cc * _PROJ_CC:(cc + 1) * _PROJ_CC],
                              preferred_element_type=F32)
                for hb in range(_PROJ_CC // HEAD_DIM):
                    a = acc[:, hb * HEAD_DIM:(hb + 1) * HEAD_DIM]
                    c0 = cc * _PROJ_CC + hb * HEAD_DIM
                    o_ref[rows, c0:c0 + HEAD_DIM] = epilogue(a, rows).astype(o_ref.dtype)
            return carry
        lax.fori_loop(0, n_rc, body, 0)

    @pl.when(j < 2)
    def _():
        scale = jnp.where(j == 1, HEAD_DIM ** -0.5, 1.0).astype(F32)

        def ep(a, rows):
            rot = a * rc_ref[rows, :] + pltpu.roll(a, HEAD_DIM // 2, 1) * rs_ref[rows, :]
            return rot * scale
        run(ep)

    @pl.when(jnp.logical_or(j == 4, j == 5))
    def _():
        scale = jnp.where(j == 4, DIFF_DQK ** -0.5 * math.log2(math.e), 1.0).astype(F32)
        half = DIFF_ROT_DIM // 2

        def ep(a, rows):
            rot = (a * dc_ref[rows, :]
                   + pltpu.roll(a, HEAD_DIM - half, 1) * da_ref[rows, :]
                   + pltpu.roll(a, half, 1) * db_ref[rows, :])
            return rot * scale
        run(ep)

    @pl.when(jnp.logical_or(jnp.logical_or(j == 2, j == 3), j == 6))
    def _():
        run(lambda a, rows: a)


def _proj(x2d, mod3, norm_w, w_in_bf16, tabs, *, seq, bm=1024, bn=1024):
    bt, d = x2d.shape
    n = w_in_bf16.shape[1]
    tiles_per_seq = seq // bm
    tab_spec = pl.BlockSpec((bm, HEAD_DIM), lambda m, j: (m % tiles_per_seq, 0))
    return pl.pallas_call(
        functools.partial(_proj_kernel, bm=bm, bn=bn),
        out_shape=jax.ShapeDtypeStruct((bt, n), BF16),
        grid=(bt // bm, n // bn),
        in_specs=[
            pl.BlockSpec((bm, d), lambda m, j: (m, 0)),
            pl.BlockSpec((1, 1, d), lambda m, j: ((m // tiles_per_seq) * 6 + 0, 0, 0)),
            pl.BlockSpec((1, 1, d), lambda m, j: ((m // tiles_per_seq) * 6 + 1, 0, 0)),
            pl.BlockSpec((1, d), lambda m, j: (0, 0)),
            pl.BlockSpec((d, bn), lambda m, j: (0, j)),
            tab_spec, tab_spec, tab_spec, tab_spec, tab_spec,
        ],
        out_specs=pl.BlockSpec((bm, bn), lambda m, j: (m, j)),
        scratch_shapes=[pltpu.VMEM((bm, d), BF16)],
        compiler_params=pltpu.CompilerParams(
            dimension_semantics=("arbitrary", "arbitrary"), vmem_limit_bytes=V7X_VMEM_LIMIT_BYTES),
        name="proj",
    )(x2d, mod3, mod3, norm_w, w_in_bf16, *tabs)


def _rotary_tables(seq):
    pos = jnp.arange(seq, dtype=F32)[:, None]
    lane = jnp.arange(HEAD_DIM)
    half = HEAD_DIM // 2
    inv = RET_THETA ** (-(2 * (lane % half)).astype(F32) / HEAD_DIM)
    ang = pos * inv[None, :]
    ret_c = jnp.cos(ang)
    ret_s = jnp.sin(ang) * jnp.where(lane < half, -1.0, 1.0)[None, :]
    half = DIFF_ROT_DIM // 2
    within = lane % DIFF_DQK
    inv = ROPE_THETA ** (-(2 * (within % half)).astype(F32) / DIFF_ROT_DIM)
    ang = pos * jnp.where(within < DIFF_ROT_DIM, inv, 0.0)[None, :]
    sin = jnp.sin(ang)
    dif_c = jnp.cos(ang)
    dif_a = sin * jnp.where(within < half, -1.0, 0.0)[None, :]
    dif_b = sin * jnp.where((within >= half) & (within < DIFF_ROT_DIM), 1.0, 0.0)[None, :]
    return ret_c, ret_s, dif_c, dif_a, dif_b


def _ret_kernel(q_ref, k_ref, v_ref, g_ref, gw_ref, o_ref, s_ref, *, seq):
    C = RET_CHUNK
    nc = seq // C
    head = (pl.program_id(0) % N_RET_HEADS).astype(F32)
    ii = lax.broadcasted_iota(jnp.int32, (C, C), 0).astype(F32)
    jj = lax.broadcasted_iota(jnp.int32, (C, C), 1).astype(F32)
    lg = jnp.log(1.0 - jnp.exp2(jnp.zeros((C, C), F32) - 5.0 - head))
    rel = ii - jj
    decay = jnp.where(rel >= 0, jnp.exp(jnp.maximum(rel, 0.0) * lg), 0.0)
    xi = jnp.exp((ii + 1.0) * lg)
    zeta = jnp.exp((C - 1.0 - ii) * lg)
    chunk_decay = jnp.exp(C * lg)

    def scan_body(c, state):
        rows = pl.ds(pl.multiple_of(c * C, C), C)
        s_ref[c] = state.astype(BF16)
        kz = (k_ref[rows, :].astype(F32) * zeta).astype(BF16)
        kv = lax.dot_general(kz, v_ref[rows, :], (((0,), (0,)), ((), ())),
                             preferred_element_type=F32)
        return state * chunk_decay + kv
    lax.fori_loop(0, nc, scan_body, jnp.zeros((C, C), F32), unroll=8)

    gw = gw_ref[...]

    def out_body(c, carry):
        rows = pl.ds(pl.multiple_of(c * C, C), C)
        q = q_ref[rows, :]
        inner = lax.dot_general(q, k_ref[rows, :], (((1,), (1,)), ((), ())),
                                preferred_element_type=F32) * decay
        o = jnp.dot(inner.astype(BF16), v_ref[rows, :], preferred_element_type=F32)
        o = o + jnp.dot(q, s_ref[c], preferred_element_type=F32) * xi
        mu = jnp.mean(o, axis=-1, keepdims=True)
        d = o - mu
        var = jnp.mean(d * d, axis=-1, keepdims=True)
        y = d * lax.rsqrt(var + GN_EPS) * gw
        o_ref[rows, :] = (y * _silu(g_ref[rows, :].astype(F32))).astype(o_ref.dtype)
        return carry
    lax.fori_loop(0, nc, out_body, 0, unroll=8)


def _retention(proj, gn_w, *, batch, seq):
    nh = N_RET_HEADS
    blk = lambda off: pl.BlockSpec((seq, HEAD_DIM), lambda s: (s // nh, off + s % nh))
    return pl.pallas_call(
        functools.partial(_ret_kernel, seq=seq),
        out_shape=jax.ShapeDtypeStruct((batch * seq, nh * HEAD_DIM), BF16),
        grid=(batch * nh,),
        in_specs=[blk(0), blk(nh), blk(2 * nh), blk(3 * nh),
                  pl.BlockSpec((1, HEAD_DIM), lambda s: (0, s % nh))],
        out_specs=pl.BlockSpec((seq, HEAD_DIM), lambda s: (s // nh, s % nh)),
        scratch_shapes=[pltpu.VMEM((seq // RET_CHUNK, RET_CHUNK, HEAD_DIM), BF16)],
        compiler_params=pltpu.CompilerParams(
            dimension_semantics=("arbitrary",), vmem_limit_bytes=V7X_VMEM_LIMIT_BYTES),
        name="retention",
    )(proj, proj, proj, proj, gn_w)


_ATTN_STRIP = 256


def _attn_kernel(q_ref, k_ref, v_ref, lq1_ref, lk1_ref, lq2_ref, lk2_ref, sw_ref, o_ref,
                 vt_ref, m_ref, l_ref, acc_ref, sa_ref, sb_ref, *, blk, lam_init):
    nblk = vt_ref.shape[0]
    nstrip = blk // _ATTN_STRIP
    assert nblk % 2 == 0

    for jb in range(nblk):
        vt_ref[jb] = v_ref[jb * blk:(jb + 1) * blk, :].T

    lane = lax.broadcasted_iota(jnp.int32, (_ATTN_STRIP, HEAD_DIM), 1)
    comp_mask = (lane < DIFF_DQK, lane >= DIFF_DQK)
    chains = [(comp, s) for comp in range(2) for s in range(nstrip)]

    def rows_of(i, s):
        start = i * blk + s * _ATTN_STRIP
        if not isinstance(start, int):
            start = pl.multiple_of(start, _ATTN_STRIP)
        return pl.ds(start, _ATTN_STRIP)

    def scores(item, s_ref):
        i, j = item
        jrow = j * blk
        if not isinstance(jrow, int):
            jrow = pl.multiple_of(jrow, blk)
        kb = k_ref[pl.ds(jrow, blk), :]
        for c, (comp, s) in enumerate(chains):
            qv = q_ref[rows_of(i, s), :]
            qv = jnp.where(comp_mask[comp], qv, jnp.zeros_like(qv))
            s_ref[c] = lax.dot_general(kb, qv, (((1,), (1,)), ((), ())),
                                       preferred_element_type=F32)

    def softmax_av(item, s_ref, diagonal):
        i, j = item
        vtb = vt_ref[j]
        for c, (comp, s) in enumerate(chains):
            st = s_ref[c]
            idx = i * nstrip + s
            if diagonal:
                kpos = lax.broadcasted_iota(jnp.int32, st.shape, 0)
                qpos = lax.broadcasted_iota(jnp.int32, st.shape, 1) + s * _ATTN_STRIP
                st = jnp.where(kpos <= qpos, st, -jnp.inf)
                m_new = jnp.max(st, axis=0, keepdims=True)
                p = jnp.exp2(st - m_new)
                l_ref[comp, idx] = jnp.sum(p, axis=0, keepdims=True)
                acc_ref[comp, idx] = jnp.dot(vtb, p.astype(BF16), preferred_element_type=F32)
            else:
                m_old = m_ref[comp, idx]
                m_new = jnp.maximum(m_old, jnp.max(st, axis=0, keepdims=True))
                alpha = jnp.exp2(m_old - m_new)
                p = jnp.exp2(st - m_new)
                l_ref[comp, idx] = alpha * l_ref[comp, idx] + jnp.sum(p, axis=0, keepdims=True)
                acc_ref[comp, idx] = alpha * acc_ref[comp, idx] + jnp.dot(
                    vtb, p.astype(BF16), preferred_element_type=F32)
            m_ref[comp, idx] = m_new

    def next_item(item):
        i, j = item
        wrap = j + 1 >= i
        return jnp.where(wrap, i + 1, i), jnp.where(wrap, 0, j + 1)

    scores((0, 0), sa_ref)
    for t in range(nblk // 2):
        scores((2 * t + 1, 2 * t + 1), sb_ref)
        softmax_av((2 * t, 2 * t), sa_ref, True)
        scores((2 * t + 2, 2 * t + 2) if 2 * t + 2 < nblk else (1, 0), sa_ref)
        softmax_av((2 * t + 1, 2 * t + 1), sb_ref, True)

    n_off = nblk * (nblk - 1) // 2
    assert n_off % 2 == 0

    def pair(t, item):
        item_b = next_item(item)
        item_c = next_item(item_b)
        scores(item_b, sb_ref)
        softmax_av(item, sa_ref, False)
        scores(item_c, sa_ref)
        softmax_av(item_b, sb_ref, False)
        return item_c
    item = lax.fori_loop(0, n_off // 2 - 1, pair, (jnp.int32(1), jnp.int32(0)))
    item_b = next_item(item)
    scores(item_b, sb_ref)
    softmax_av(item, sa_ref, False)
    softmax_av(item_b, sb_ref, False)

    lam = (jnp.exp(jnp.sum(lq1_ref[...] * lk1_ref[...], axis=-1, keepdims=True))
           - jnp.exp(jnp.sum(lq2_ref[...] * lk2_ref[...], axis=-1, keepdims=True))
           + lam_init)
    sw = sw_ref[...] * (1.0 - lam_init)

    def finish(idx, carry):
        yt = acc_ref[0, idx] / l_ref[0, idx] - lam * (acc_ref[1, idx] / l_ref[1, idx])
        yt = yt * lax.rsqrt(jnp.mean(yt * yt, axis=0, keepdims=True) + GN_EPS)
        rows = pl.ds(pl.multiple_of(idx * _ATTN_STRIP, _ATTN_STRIP), _ATTN_STRIP)
        o_ref[rows, :] = (yt.T * sw).astype(o_ref.dtype)
        return carry
    lax.fori_loop(0, nblk * nstrip, finish, 0)


def _diff_attention(proj, lq1, lk1, lq2, lk2, subln_w, *, batch, seq, lam_init, blk=512):
    nh = N_DIFF_HEADS
    nblk = seq // blk
    nchain = 2 * blk // _ATTN_STRIP
    nstrips = seq // _ATTN_STRIP
    q_off, k_off, v_off = 4 * nh, 5 * nh, 6 * nh
    head_blk = lambda off: pl.BlockSpec((seq, HEAD_DIM), lambda s: (s // nh, off + s % nh))
    lam_spec = pl.BlockSpec((1, DIFF_DQK), lambda s: (0, 0))
    return pl.pallas_call(
        functools.partial(_attn_kernel, blk=blk, lam_init=lam_init),
        out_shape=jax.ShapeDtypeStruct((batch * seq, nh * HEAD_DIM), BF16),
        grid=(batch * nh,),
        in_specs=[
            head_blk(q_off), head_blk(k_off), head_blk(v_off),
            lam_spec, lam_spec, lam_spec, lam_spec,
            pl.BlockSpec((1, HEAD_DIM), lambda s: (0, s % nh)),
        ],
        out_specs=pl.BlockSpec((seq, HEAD_DIM), lambda s: (s // nh, s % nh)),
        scratch_shapes=[
            pltpu.VMEM((nblk, HEAD_DIM, blk), BF16),
            pltpu.VMEM((2, nstrips, 1, _ATTN_STRIP), F32),
            pltpu.VMEM((2, nstrips, 1, _ATTN_STRIP), F32),
            pltpu.VMEM((2, nstrips, HEAD_DIM, _ATTN_STRIP), F32),
            pltpu.VMEM((nchain, blk, _ATTN_STRIP), F32),
            pltpu.VMEM((nchain, blk, _ATTN_STRIP), F32),
        ],
        compiler_params=pltpu.CompilerParams(
            dimension_semantics=("arbitrary",), vmem_limit_bytes=V7X_VMEM_LIMIT_BYTES),
        name="diff_attn",
    )(proj, proj, proj, lq1, lk1, lq2, lk2, subln_w)


_OUT_RC = 512


def _outproj_kernel(yr_ref, yd_ref, w_ref, x_ref, g_ref, sh_ref, sc_ref, nw_ref,
                    x1_ref, h2_ref, *, bm, wr):
    def body(r, carry):
        rows = pl.ds(pl.multiple_of(r * _OUT_RC, _OUT_RC), _OUT_RC)
        mixed = (jnp.dot(yr_ref[rows, :], w_ref[:wr, :], preferred_element_type=F32)
                 + jnp.dot(yd_ref[rows, :], w_ref[wr:, :], preferred_element_type=F32))
        x1 = x_ref[rows, :] + g_ref[0] * mixed
        x1_ref[rows, :] = x1
        ms = jnp.mean(x1 * x1, axis=-1, keepdims=True)
        y = x1 * lax.rsqrt(ms + NORM_EPS) * nw_ref[...]
        h2_ref[rows, :] = (y * (1.0 + sc_ref[0]) + sh_ref[0]).astype(BF16)
        return carry
    lax.fori_loop(0, bm // _OUT_RC, body, 0)


def _outproj(y_ret, y_dif, w_out_bf16, x2d, mod3, norm_w, *, seq, bm=512):
    bt, d = x2d.shape
    wr = y_ret.shape[1]
    wd = y_dif.shape[1]
    tiles_per_seq = seq // bm
    mod_spec = lambda idx: pl.BlockSpec(
        (1, 1, d), lambda m: ((m // tiles_per_seq) * 6 + idx, 0, 0))
    return pl.pallas_call(
        functools.partial(_outproj_kernel, bm=bm, wr=wr),
        out_shape=(jax.ShapeDtypeStruct((bt, d), F32), jax.ShapeDtypeStruct((bt, d), BF16)),
        grid=(bt // bm,),
        in_specs=[
            pl.BlockSpec((bm, wr), lambda m: (m, 0)),
            pl.BlockSpec((bm, wd), lambda m: (m, 0)),
            pl.BlockSpec((wr + wd, d), lambda m: (0, 0)),
            pl.BlockSpec((bm, d), lambda m: (m, 0)),
            mod_spec(2), mod_spec(3), mod_spec(4),
            pl.BlockSpec((1, d), lambda m: (0, 0)),
        ],
        out_specs=(pl.BlockSpec((bm, d), lambda m: (m, 0)), pl.BlockSpec((bm, d), lambda m: (m, 0))),
        compiler_params=pltpu.CompilerParams(
            dimension_semantics=("arbitrary",), vmem_limit_bytes=V7X_VMEM_LIMIT_BYTES),
        name="outproj",
    )(y_ret, y_dif, w_out_bf16, x2d, mod3, mod3, mod3, norm_w)


_MLP_RC = 512


def _mlp_kernel(h_ref, x1_ref, w1_ref, w2_ref, g_ref, fw_ref, o_ref, *, bm, final_norm):
    f = pl.program_id(1)
    nf = pl.num_programs(1)

    @pl.when(f == 0)
    def _():
        o_ref[...] = jnp.zeros_like(o_ref)

    def body(r, carry):
        rows = pl.ds(pl.multiple_of(r * _MLP_RC, _MLP_RC), _MLP_RC)
        a = jnp.dot(h_ref[rows, :], w1_ref[...], preferred_element_type=F32)
        a = jnp.maximum(a, 0.0)
        a = (a * a).astype(BF16)
        o_ref[rows, :] += jnp.dot(a, w2_ref[...], preferred_element_type=F32)
        return carry
    lax.fori_loop(0, bm // _MLP_RC, body, 0)

    @pl.when(f == nf - 1)
    def _():
        def fin(r, carry):
            rows = pl.ds(pl.multiple_of(r * _MLP_RC, _MLP_RC), _MLP_RC)
            x2 = x1_ref[rows, :] + g_ref[0] * o_ref[rows, :]
            if final_norm:
                ms = jnp.mean(x2 * x2, axis=-1, keepdims=True)
                x2 = x2 * lax.rsqrt(ms + NORM_EPS) * fw_ref[...]
            o_ref[rows, :] = x2
            return carry
        lax.fori_loop(0, bm // _MLP_RC, fin, 0)


def _mlp(h2, x1, w1_bf16, w2_bf16, mod3, final_w, *, seq, final_norm, bm=1024, bf=512):
    bt, d = x1.shape
    dff = w1_bf16.shape[1]
    tiles_per_seq = seq // bm
    return pl.pallas_call(
        functools.partial(_mlp_kernel, bm=bm, final_norm=final_norm),
        out_shape=jax.ShapeDtypeStruct((bt, d), F32),
        grid=(bt // bm, dff // bf),
        in_specs=[
            pl.BlockSpec((bm, d), lambda m, f: (m, 0)),
            pl.BlockSpec((bm, d), lambda m, f: (m, 0)),
            pl.BlockSpec((d, bf), lambda m, f: (0, f)),
            pl.BlockSpec((bf, d), lambda m, f: (f, 0)),
            pl.BlockSpec((1, 1, d), lambda m, f: ((m // tiles_per_seq) * 6 + 5, 0, 0)),
            pl.BlockSpec((1, d), lambda m, f: (0, 0)),
        ],
        out_specs=pl.BlockSpec((bm, d), lambda m, f: (m, 0)),
        compiler_params=pltpu.CompilerParams(
            dimension_semantics=("arbitrary", "arbitrary"), vmem_limit_bytes=V7X_VMEM_LIMIT_BYTES),
        name="mlp",
    )(h2, x1, w1_bf16, w2_bf16, mod3, final_w)


def kernel(x, c, w_ada, b_ada, norm1_w, norm2_w, w_in, ret_gn_w, diff_lq1, diff_lk1, diff_lq2,
           diff_lk2, diff_subln_w, w_out, w_mlp1, w_mlp2, final_norm_w):
    batch, seq, d = x.shape
    depth = w_ada.shape[0]
    c_pad = jnp.pad(c, ((0, 8 - batch), (0, 0)))
    tabs = _rotary_tables(seq)
    x2d = x.reshape(batch * seq, d)
    final_w = final_norm_w.reshape(1, d)

    for l in range(depth):
        lam_init = 0.8 - 0.6 * math.exp(-0.3 * l)
        mod = _ada(c_pad, w_ada[l], b_ada[l].reshape(1, -1))
        mod3 = mod[:batch].reshape(batch * 6, 1, d)
        proj = _proj(x2d, mod3, norm1_w[l].reshape(1, d), w_in[l].astype(BF16), tabs, seq=seq)
        y_ret = _retention(proj, ret_gn_w[l].reshape(1, -1), batch=batch, seq=seq)
        y_dif = _diff_attention(
            proj, diff_lq1[l].reshape(1, -1), diff_lk1[l].reshape(1, -1),
            diff_lq2[l].reshape(1, -1), diff_lk2[l].reshape(1, -1),
            diff_subln_w[l].reshape(1, -1), batch=batch, seq=seq, lam_init=lam_init)
        x1, h2 = _outproj(y_ret, y_dif, w_out[l].astype(BF16), x2d, mod3,
                          norm2_w[l].reshape(1, d), seq=seq)
        x2d = _mlp(h2, x1, w_mlp1[l].astype(BF16), w_mlp2[l].astype(BF16), mod3, final_w,
                   seq=seq, final_norm=(l == depth - 1))
    if depth == 0:
        raise ValueError("depth must be positive")
    return x2d.reshape(batch, seq, d)
```

```python
import functools
import math

import jax
import jax.numpy as jnp
from jax import lax
from jax.experimental import pallas as pl
from jax.experimental.pallas import tpu as pltpu

F32 = jnp.float32
BF16 = jnp.bfloat16

HEAD_DIM = 128
N_RET_HEADS = 8
N_DIFF_HEADS = 8
DIFF_DQK = HEAD_DIM // 2
DIFF_ROT_DIM = DIFF_DQK // 4
RET_CHUNK = 128
ROPE_THETA = 500000.0
RET_THETA = 10000.0
NORM_EPS = 1e-6
GN_EPS = 1e-5

V7X_VMEM_LIMIT_BYTES = 58 * 1024 * 1024


def _silu(v):
    return v * jax.nn.sigmoid(v)


def _ada_kernel(c_ref, w_ref, b_ref, o_ref):
    s = _silu(c_ref[...]).astype(BF16)
    o_ref[...] = jnp.dot(s, w_ref[...].astype(BF16), preferred_element_type=F32) + b_ref[...]


def _ada(c_pad, w_ada, b_ada, *, bn=1024):
    rows, d = c_pad.shape
    n = w_ada.shape[1]
    return pl.pallas_call(
        _ada_kernel,
        out_shape=jax.ShapeDtypeStruct((rows, n), F32),
        grid=(n // bn,),
        in_specs=[
            pl.BlockSpec((rows, d), lambda j: (0, 0)),
            pl.BlockSpec((d, bn), lambda j: (0, j)),
            pl.BlockSpec((1, bn), lambda j: (0, j)),
        ],
        out_specs=pl.BlockSpec((rows, bn), lambda j: (0, j)),
        compiler_params=pltpu.CompilerParams(
            dimension_semantics=("arbitrary",), vmem_limit_bytes=V7X_VMEM_LIMIT_BYTES),
        name="ada",
    )(c_pad, w_ada, b_ada)


_PROJ_RC = 512
_PROJ_CC = 256
_PROJ_NORM_ROWS = 256


def _proj_kernel(x_ref, sh_ref, sc_ref, nw_ref, w_ref, rc_ref, rs_ref, dc_ref, da_ref, db_ref,
                 o_ref, h_ref, wb_ref, *, bm, bn):
    j = pl.program_id(1)
    n_rc = bm // _PROJ_RC

    def norm_modulate():
        for r in range(bm // _PROJ_NORM_ROWS):
            rows = slice(r * _PROJ_NORM_ROWS, (r + 1) * _PROJ_NORM_ROWS)
            x = x_ref[rows, :]
            ms = jnp.mean(x * x, axis=-1, keepdims=True)
            y = x * lax.rsqrt(ms + NORM_EPS) * nw_ref[...]
            h_ref[rows, :] = (y * (1.0 + sc_ref[0]) + sh_ref[0]).astype(BF16)

    def run(epilogue):
        for cc in range(bn // _PROJ_CC):
            cols = slice(cc * _PROJ_CC, (cc + 1) * _PROJ_CC)
            wb_ref[:, cols] = w_ref[:, cols].astype(BF16)
            for r in range(n_rc):
                rows = slice(r * _PROJ_RC, (r + 1) * _PROJ_RC)
                acc = jnp.dot(h_ref[rows, :], wb_ref[:, cols], preferred_element_type=F32)
                for hb in range(_PROJ_CC // HEAD_DIM):
                    a = acc[:, hb * HEAD_DIM:(hb + 1) * HEAD_DIM]
                    c0 = cc * _PROJ_CC + hb * HEAD_DIM
                    o_ref[rows, c0:c0 + HEAD_DIM] = epilogue(a, rows).astype(o_ref.dtype)

    def ret_rotary(a, rows):
        return a * rc_ref[rows, :] + pltpu.roll(a, HEAD_DIM // 2, 1) * rs_ref[rows, :]

    @pl.when(j == 0)
    def _():
        norm_modulate()
        run(ret_rotary)

    @pl.when(j == 1)
    def _():
        run(lambda a, rows: ret_rotary(a, rows) * (HEAD_DIM ** -0.5))

    @pl.when(jnp.logical_or(j == 4, j == 5))
    def _():
        scale = jnp.where(j == 4, DIFF_DQK ** -0.5 * math.log2(math.e), 1.0).astype(F32)
        half = DIFF_ROT_DIM // 2

        def ep(a, rows):
            rot = (a * dc_ref[rows, :]
                   + pltpu.roll(a, HEAD_DIM - half, 1) * da_ref[rows, :]
                   + pltpu.roll(a, half, 1) * db_ref[rows, :])
            return rot * scale
        run(ep)

    @pl.when(jnp.logical_or(jnp.logical_or(j == 2, j == 3), j == 6))
    def _():
        run(lambda a, rows: a)


def _proj(x2d, mod3, norm_w, w_in, tabs, *, seq, bm=1024, bn=1024):
    bt, d = x2d.shape
    n = w_in.shape[1]
    tiles_per_seq = seq // bm
    tab_spec = pl.BlockSpec((bm, HEAD_DIM), lambda m, j: (m % tiles_per_seq, 0))
    return pl.pallas_call(
        functools.partial(_proj_kernel, bm=bm, bn=bn),
        out_shape=jax.ShapeDtypeStruct((bt, n), BF16),
        grid=(bt // bm, n // bn),
        in_specs=[
            pl.BlockSpec((bm, d), lambda m, j: (m, 0)),
            pl.BlockSpec((1, 1, d), lambda m, j: ((m // tiles_per_seq) * 6 + 0, 0, 0)),
            pl.BlockSpec((1, 1, d), lambda m, j: ((m // tiles_per_seq) * 6 + 1, 0, 0)),
            pl.BlockSpec((1, d), lambda m, j: (0, 0)),
            pl.BlockSpec((d, bn), lambda m, j: (0, j)),
            tab_spec, tab_spec, tab_spec, tab_spec, tab_spec,
        ],
        out_specs=pl.BlockSpec((bm, bn), lambda m, j: (m, j)),
        scratch_shapes=[pltpu.VMEM((bm, d), BF16), pltpu.VMEM((d, bn), BF16)],
        compiler_params=pltpu.CompilerParams(
            dimension_semantics=("arbitrary", "arbitrary"), vmem_limit_bytes=V7X_VMEM_LIMIT_BYTES),
        name="proj",
    )(x2d, mod3, mod3, norm_w, w_in, *tabs)


def _rotary_tables(seq):
    pos = jnp.arange(seq, dtype=F32)[:, None]
    lane = jnp.arange(HEAD_DIM)
    half = HEAD_DIM // 2
    inv = RET_THETA ** (-(2 * (lane % half)).astype(F32) / HEAD_DIM)
    ang = pos * inv[None, :]
    ret_c = jnp.cos(ang)
    ret_s = jnp.sin(ang) * jnp.where(lane < half, -1.0, 1.0)[None, :]
    half = DIFF_ROT_DIM // 2
    within = lane % DIFF_DQK
    inv = ROPE_THETA ** (-(2 * (within % half)).astype(F32) / DIFF_ROT_DIM)
    ang = pos * jnp.where(within < DIFF_ROT_DIM, inv, 0.0)[None, :]
    sin = jnp.sin(ang)
    dif_c = jnp.cos(ang)
    dif_a = sin * jnp.where(within < half, -1.0, 0.0)[None, :]
    dif_b = sin * jnp.where((within >= half) & (within < DIFF_ROT_DIM), 1.0, 0.0)[None, :]
    return ret_c, ret_s, dif_c, dif_a, dif_b


def _ret_kernel(q_ref, k_ref, v_ref, g_ref, gw_ref, o_ref, s_ref, *, seq):
    C = RET_CHUNK
    nc = seq // C
    head = (pl.program_id(0) % N_RET_HEADS).astype(F32)
    ii = lax.broadcasted_iota(jnp.int32, (C, C), 0).astype(F32)
    jj = lax.broadcasted_iota(jnp.int32, (C, C), 1).astype(F32)
    lg = jnp.log(1.0 - jnp.exp2(jnp.zeros((C, C), F32) - 5.0 - head))
    rel = ii - jj
    decay = jnp.where(rel >= 0, jnp.exp(jnp.maximum(rel, 0.0) * lg), 0.0)
    xi = jnp.exp((ii + 1.0) * lg)
    zeta = jnp.exp((C - 1.0 - ii) * lg)
    chunk_decay = jnp.exp(C * lg)

    def scan_body(c, state):
        rows = pl.ds(pl.multiple_of(c * C, C), C)
        s_ref[c] = state.astype(BF16)
        kz = (k_ref[rows, :].astype(F32) * zeta).astype(BF16)
        kv = lax.dot_general(kz, v_ref[rows, :], (((0,), (0,)), ((), ())),
                             preferred_element_type=F32)
        return state * chunk_decay + kv
    lax.fori_loop(0, nc, scan_body, jnp.zeros((C, C), F32), unroll=8)

    gw = gw_ref[...]

    def out_body(c, carry):
        rows = pl.ds(pl.multiple_of(c * C, C), C)
        q = q_ref[rows, :]
        inner = lax.dot_general(q, k_ref[rows, :], (((1,), (1,)), ((), ())),
                                preferred_element_type=F32) * decay
        o = jnp.dot(inner.astype(BF16), v_ref[rows, :], preferred_element_type=F32)
        o = o + jnp.dot(q, s_ref[c], preferred_element_type=F32) * xi
        mu = jnp.mean(o, axis=-1, keepdims=True)
        d = o - mu
        var = jnp.mean(d * d, axis=-1, keepdims=True)
        y = d * lax.rsqrt(var + GN_EPS) * gw
        o_ref[rows, :] = (y * _silu(g_ref[rows, :].astype(F32))).astype(o_ref.dtype)
        return carry
    lax.fori_loop(0, nc, out_body, 0, unroll=8)


def _retention(proj, gn_w, *, batch, seq):
    nh = N_RET_HEADS
    blk = lambda off: pl.BlockSpec((seq, HEAD_DIM), lambda s: (s // nh, off + s % nh))
    return pl.pallas_call(
        functools.partial(_ret_kernel, seq=seq),
        out_shape=jax.ShapeDtypeStruct((batch * seq, nh * HEAD_DIM), BF16),
        grid=(batch * nh,),
        in_specs=[blk(0), blk(nh), blk(2 * nh), blk(3 * nh),
                  pl.BlockSpec((1, HEAD_DIM), lambda s: (0, s % nh))],
        out_specs=pl.BlockSpec((seq, HEAD_DIM), lambda s: (s // nh, s % nh)),
        scratch_shapes=[pltpu.VMEM((seq // RET_CHUNK, RET_CHUNK, HEAD_DIM), BF16)],
        compiler_params=pltpu.CompilerParams(
            dimension_semantics=("arbitrary",), vmem_limit_bytes=V7X_VMEM_LIMIT_BYTES),
        name="retention",
    )(proj, proj, proj, proj, gn_w)


_ATTN_STRIP = 256


def _attn_kernel(q_ref, k_ref, v_ref, lq1_ref, lk1_ref, lq2_ref, lk2_ref, sw_ref, *rest,
                 blk, lam_init, n_cast):
    cast_src = rest[:n_cast]
    o_ref = rest[n_cast]
    cast_dst = rest[n_cast + 1:2 * n_cast + 1]
    vt_ref, m_ref, l_ref, acc_ref, sa_ref, sb_ref = rest[2 * n_cast + 1:]
    _attn_body(q_ref, k_ref, v_ref, lq1_ref, lk1_ref, lq2_ref, lk2_ref, sw_ref, cast_src,
               o_ref, cast_dst, vt_ref, m_ref, l_ref, acc_ref, sa_ref, sb_ref,
               blk=blk, lam_init=lam_init)


def _attn_body(q_ref, k_ref, v_ref, lq1_ref, lk1_ref, lq2_ref, lk2_ref, sw_ref, cast_src,
               o_ref, cast_dst, vt_ref, m_ref, l_ref, acc_ref, sa_ref, sb_ref, *, blk, lam_init):
    nblk = vt_ref.shape[0]
    nstrip = blk // _ATTN_STRIP
    assert nblk % 2 == 0

    for jb in range(nblk):
        vt_ref[jb] = v_ref[jb * blk:(jb + 1) * blk, :].T

    lane = lax.broadcasted_iota(jnp.int32, (_ATTN_STRIP, HEAD_DIM), 1)
    comp_mask = (lane < DIFF_DQK, lane >= DIFF_DQK)
    chains = [(comp, s) for comp in range(2) for s in range(nstrip)]

    def rows_of(i, s):
        start = i * blk + s * _ATTN_STRIP
        if not isinstance(start, int):
            start = pl.multiple_of(start, _ATTN_STRIP)
        return pl.ds(start, _ATTN_STRIP)

    def keys_needed(s, diagonal):
        return (s + 1) * _ATTN_STRIP if diagonal else blk

    def scores(item, s_ref, diagonal=False):
        i, j = item
        jrow = j * blk
        if not isinstance(jrow, int):
            jrow = pl.multiple_of(jrow, blk)
        for c, (comp, s) in enumerate(chains):
            nk = keys_needed(s, diagonal)
            qv = q_ref[rows_of(i, s), :]
            qv = jnp.where(comp_mask[comp], qv, jnp.zeros_like(qv))
            s_ref[c, :nk, :] = lax.dot_general(k_ref[pl.ds(jrow, nk), :], qv,
                                               (((1,), (1,)), ((), ())),
                                               preferred_element_type=F32)

    def softmax_av(item, s_ref, diagonal):
        i, j = item
        vtb = vt_ref[j]
        for c, (comp, s) in enumerate(chains):
            nk = keys_needed(s, diagonal)
            st = s_ref[c, :nk, :]
            idx = i * nstrip + s
            if diagonal:
                kpos = lax.broadcasted_iota(jnp.int32, st.shape, 0)
                qpos = lax.broadcasted_iota(jnp.int32, st.shape, 1) + s * _ATTN_STRIP
                st = jnp.where(kpos <= qpos, st, -jnp.inf)
                m_new = jnp.max(st, axis=0, keepdims=True)
                p = jnp.exp2(st - m_new)
                l_ref[comp, idx] = jnp.sum(p, axis=0, keepdims=True)
                acc_ref[comp, idx] = jnp.dot(vtb[:, :nk], p.astype(BF16),
                                             preferred_element_type=F32)
            else:
                m_old = m_ref[comp, idx]
                m_new = jnp.maximum(m_old, jnp.max(st, axis=0, keepdims=True))
                alpha = jnp.exp2(m_old - m_new)
                p = jnp.exp2(st - m_new)
                l_ref[comp, idx] = alpha * l_ref[comp, idx] + jnp.sum(p, axis=0, keepdims=True)
                acc_ref[comp, idx] = alpha * acc_ref[comp, idx] + jnp.dot(
                    vtb, p.astype(BF16), preferred_element_type=F32)
            m_ref[comp, idx] = m_new

    def next_item(item):
        i, j = item
        wrap = j + 1 >= i
        return jnp.where(wrap, i + 1, i), jnp.where(wrap, 0, j + 1)

    n_parts = nblk // 2
    scores((0, 0), sa_ref, True)
    for t in range(n_parts):
        for src, dst in zip(cast_src, cast_dst):
            part = src.shape[0] // n_parts
            dst[t * part:(t + 1) * part, :] = src[t * part:(t + 1) * part, :].astype(dst.dtype)
        scores((2 * t + 1, 2 * t + 1), sb_ref, True)
        softmax_av((2 * t, 2 * t), sa_ref, True)
        if 2 * t + 2 < nblk:
            scores((2 * t + 2, 2 * t + 2), sa_ref, True)
        else:
            scores((1, 0), sa_ref)
        softmax_av((2 * t + 1, 2 * t + 1), sb_ref, True)

    n_off = nblk * (nblk - 1) // 2
    assert n_off % 2 == 0

    def pair(t, item):
        item_b = next_item(item)
        item_c = next_item(item_b)
        scores(item_b, sb_ref)
        softmax_av(item, sa_ref, False)
        scores(item_c, sa_ref)
        softmax_av(item_b, sb_ref, False)
        return item_c
    item = lax.fori_loop(0, n_off // 2 - 1, pair, (jnp.int32(1), jnp.int32(0)))
    item_b = next_item(item)
    scores(item_b, sb_ref)
    softmax_av(item, sa_ref, False)
    softmax_av(item_b, sb_ref, False)

    lam = (jnp.exp(jnp.sum(lq1_ref[...] * lk1_ref[...], axis=-1, keepdims=True))
           - jnp.exp(jnp.sum(lq2_ref[...] * lk2_ref[...], axis=-1, keepdims=True))
           + lam_init)
    sw = sw_ref[...] * (1.0 - lam_init)

    def finish(idx, carry):
        yt = acc_ref[0, idx] / l_ref[0, idx] - lam * (acc_ref[1, idx] / l_ref[1, idx])
        yt = yt * lax.rsqrt(jnp.mean(yt * yt, axis=0, keepdims=True) + GN_EPS)
        rows = pl.ds(pl.multiple_of(idx * _ATTN_STRIP, _ATTN_STRIP), _ATTN_STRIP)
        o_ref[rows, :] = (yt.T * sw).astype(o_ref.dtype)
        return carry
    lax.fori_loop(0, nblk * nstrip, finish, 0)


def _diff_attention(proj, lq1, lk1, lq2, lk2, subln_w, cast_weights, *, batch, seq, lam_init,
                    blk=512):
    nh = N_DIFF_HEADS
    nblk = seq // blk
    nchain = 2 * blk // _ATTN_STRIP
    nstrips = seq // _ATTN_STRIP
    steps = batch * nh
    q_off, k_off, v_off = 4 * nh, 5 * nh, 6 * nh
    head_blk = lambda off: pl.BlockSpec((seq, HEAD_DIM), lambda s: (s // nh, off + s % nh))
    lam_spec = pl.BlockSpec((1, DIFF_DQK), lambda s: (0, 0))
    cast_specs = [pl.BlockSpec((w.shape[0] // steps, w.shape[1]), lambda s: (s, 0))
                  for w in cast_weights]
    outs = pl.pallas_call(
        functools.partial(_attn_kernel, blk=blk, lam_init=lam_init, n_cast=len(cast_weights)),
        out_shape=[jax.ShapeDtypeStruct((batch * seq, nh * HEAD_DIM), BF16)]
        + [jax.ShapeDtypeStruct(w.shape, BF16) for w in cast_weights],
        grid=(steps,),
        in_specs=[
            head_blk(q_off), head_blk(k_off), head_blk(v_off),
            lam_spec, lam_spec, lam_spec, lam_spec,
            pl.BlockSpec((1, HEAD_DIM), lambda s: (0, s % nh)),
        ] + cast_specs,
        out_specs=[pl.BlockSpec((seq, HEAD_DIM), lambda s: (s // nh, s % nh))] + cast_specs,
        scratch_shapes=[
            pltpu.VMEM((nblk, HEAD_DIM, blk), BF16),
            pltpu.VMEM((2, nstrips, 1, _ATTN_STRIP), F32),
            pltpu.VMEM((2, nstrips, 1, _ATTN_STRIP), F32),
            pltpu.VMEM((2, nstrips, HEAD_DIM, _ATTN_STRIP), F32),
            pltpu.VMEM((nchain, blk, _ATTN_STRIP), F32),
            pltpu.VMEM((nchain, blk, _ATTN_STRIP), F32),
        ],
        compiler_params=pltpu.CompilerParams(
            dimension_semantics=("arbitrary",), vmem_limit_bytes=V7X_VMEM_LIMIT_BYTES),
        name="diff_attn",
    )(proj, proj, proj, lq1, lk1, lq2, lk2, subln_w, *cast_weights)
    return outs[0], outs[1:]


_OUT_RC = 512


def _outproj_kernel(yr_ref, yd_ref, w_ref, x_ref, g_ref, sh_ref, sc_ref, nw_ref,
                    x1_ref, h2_ref, *, bm, wr):
    def body(r, carry):
        rows = pl.ds(pl.multiple_of(r * _OUT_RC, _OUT_RC), _OUT_RC)
        mixed = (jnp.dot(yr_ref[rows, :], w_ref[:wr, :], preferred_element_type=F32)
                 + jnp.dot(yd_ref[rows, :], w_ref[wr:, :], preferred_element_type=F32))
        x1 = x_ref[rows, :] + g_ref[0] * mixed
        x1_ref[rows, :] = x1
        ms = jnp.mean(x1 * x1, axis=-1, keepdims=True)
        y = x1 * lax.rsqrt(ms + NORM_EPS) * nw_ref[...]
        h2_ref[rows, :] = (y * (1.0 + sc_ref[0]) + sh_ref[0]).astype(BF16)
        return carry
    lax.fori_loop(0, bm // _OUT_RC, body, 0)


def _outproj(y_ret, y_dif, w_out_bf16, x2d, mod3, norm_w, *, seq, bm=512):
    bt, d = x2d.shape
    wr = y_ret.shape[1]
    wd = y_dif.shape[1]
    tiles_per_seq = seq // bm
    mod_spec = lambda idx: pl.BlockSpec(
        (1, 1, d), lambda m: ((m // tiles_per_seq) * 6 + idx, 0, 0))
    return pl.pallas_call(
        functools.partial(_outproj_kernel, bm=bm, wr=wr),
        out_shape=(jax.ShapeDtypeStruct((bt, d), F32), jax.ShapeDtypeStruct((bt, d), BF16)),
        grid=(bt // bm,),
        in_specs=[
            pl.BlockSpec((bm, wr), lambda m: (m, 0)),
            pl.BlockSpec((bm, wd), lambda m: (m, 0)),
            pl.BlockSpec((wr + wd, d), lambda m: (0, 0)),
            pl.BlockSpec((bm, d), lambda m: (m, 0)),
            mod_spec(2), mod_spec(3), mod_spec(4),
            pl.BlockSpec((1, d), lambda m: (0, 0)),
        ],
        out_specs=(pl.BlockSpec((bm, d), lambda m: (m, 0)), pl.BlockSpec((bm, d), lambda m: (m, 0))),
        compiler_params=pltpu.CompilerParams(
            dimension_semantics=("arbitrary",), vmem_limit_bytes=V7X_VMEM_LIMIT_BYTES),
        name="outproj",
    )(y_ret, y_dif, w_out_bf16, x2d, mod3, mod3, mod3, norm_w)


_MLP_RC = 512


def _mlp_kernel(h_ref, x1_ref, w1_ref, w2_ref, g_ref, fw_ref, o_ref, *, bm, final_norm):
    f = pl.program_id(1)
    nf = pl.num_programs(1)

    @pl.when(f == 0)
    def _():
        o_ref[...] = jnp.zeros_like(o_ref)

    def body(r, carry):
        rows = pl.ds(pl.multiple_of(r * _MLP_RC, _MLP_RC), _MLP_RC)
        a = jnp.dot(h_ref[rows, :], w1_ref[...], preferred_element_type=F32)
        a = jnp.maximum(a, 0.0)
        a = (a * a).astype(BF16)
        o_ref[rows, :] += jnp.dot(a, w2_ref[...], preferred_element_type=F32)
        return carry
    lax.fori_loop(0, bm // _MLP_RC, body, 0)

    @pl.when(f == nf - 1)
    def _():
        def fin(r, carry):
            rows = pl.ds(pl.multiple_of(r * _MLP_RC, _MLP_RC), _MLP_RC)
            x2 = x1_ref[rows, :] + g_ref[0] * o_ref[rows, :]
            if final_norm:
                ms = jnp.mean(x2 * x2, axis=-1, keepdims=True)
                x2 = x2 * lax.rsqrt(ms + NORM_EPS) * fw_ref[...]
            o_ref[rows, :] = x2
            return carry
        lax.fori_loop(0, bm // _MLP_RC, fin, 0)


def _mlp(h2, x1, w1_bf16, w2_bf16, mod3, final_w, *, seq, final_norm, bm=1024, bf=512):
    bt, d = x1.shape
    dff = w1_bf16.shape[1]
    tiles_per_seq = seq // bm
    return pl.pallas_call(
        functools.partial(_mlp_kernel, bm=bm, final_norm=final_norm),
        out_shape=jax.ShapeDtypeStruct((bt, d), F32),
        grid=(bt // bm, dff // bf),
        in_specs=[
            pl.BlockSpec((bm, d), lambda m, f: (m, 0)),
            pl.BlockSpec((bm, d), lambda m, f: (m, 0)),
            pl.BlockSpec((d, bf), lambda m, f: (0, f)),
            pl.BlockSpec((bf, d), lambda m, f: (f, 0)),
            pl.BlockSpec((1, 1, d), lambda m, f: ((m // tiles_per_seq) * 6 + 5, 0, 0)),
            pl.BlockSpec((1, d), lambda m, f: (0, 0)),
        ],
        out_specs=pl.BlockSpec((bm, d), lambda m, f: (m, 0)),
        compiler_params=pltpu.CompilerParams(
            dimension_semantics=("arbitrary", "arbitrary"), vmem_limit_bytes=V7X_VMEM_LIMIT_BYTES),
        name="mlp",
    )(h2, x1, w1_bf16, w2_bf16, mod3, final_w)


def kernel(x, c, w_ada, b_ada, norm1_w, norm2_w, w_in, ret_gn_w, diff_lq1, diff_lk1, diff_lq2,
           diff_lk2, diff_subln_w, w_out, w_mlp1, w_mlp2, final_norm_w):
    batch, seq, d = x.shape
    depth = w_ada.shape[0]
    c_pad = jnp.pad(c, ((0, 8 - batch), (0, 0)))
    tabs = _rotary_tables(seq)
    x2d = x.reshape(batch * seq, d)
    final_w = final_norm_w.reshape(1, d)

    for l in range(depth):
        lam_init = 0.8 - 0.6 * math.exp(-0.3 * l)
        mod = _ada(c_pad, w_ada[l], b_ada[l].reshape(1, -1))
        mod3 = mod[:batch].reshape(batch * 6, 1, d)
        proj = _proj(x2d, mod3, norm1_w[l].reshape(1, d), w_in[l], tabs, seq=seq)
        y_ret = _retention(proj, ret_gn_w[l].reshape(1, -1), batch=batch, seq=seq)
        y_dif, (w_out_b, w1_b, w2_b) = _diff_attention(
            proj, diff_lq1[l].reshape(1, -1), diff_lk1[l].reshape(1, -1),
            diff_lq2[l].reshape(1, -1), diff_lk2[l].reshape(1, -1),
            diff_subln_w[l].reshape(1, -1), (w_out[l], w_mlp1[l], w_mlp2[l]),
            batch=batch, seq=seq, lam_init=lam_init)
        x1, h2 = _outproj(y_ret, y_dif, w_out_b, x2d, mod3, norm2_w[l].reshape(1, d), seq=seq)
        x2d = _mlp(h2, x1, w1_b, w2_b, mod3, final_w, seq=seq, final_norm=(l == depth - 1))
    if depth == 0:
        raise ValueError("depth must be positive")
    return x2d.reshape(batch, seq, d)
```

```python
import functools
import math

import jax
import jax.numpy as jnp
from jax import lax
from jax.experimental import pallas as pl
from jax.experimental.pallas import tpu as pltpu

F32 = jnp.float32
BF16 = jnp.bfloat16

HEAD_DIM = 128
N_RET_HEADS = 8
N_DIFF_HEADS = 8
DIFF_DQK = HEAD_DIM // 2
DIFF_ROT_DIM = DIFF_DQK // 4
RET_CHUNK = 128
ROPE_THETA = 500000.0
RET_THETA = 10000.0
NORM_EPS = 1e-6
GN_EPS = 1e-5

V7X_VMEM_LIMIT_BYTES = 58 * 1024 * 1024


def _silu(v):
    return v * jax.nn.sigmoid(v)


def _ada_kernel(c_ref, w_ref, b_ref, o_ref):
    s = _silu(c_ref[...]).astype(BF16)
    o_ref[...] = jnp.dot(s, w_ref[...].astype(BF16), preferred_element_type=F32) + b_ref[...]


def _ada(c_pad, w_ada, b_ada, *, bn=1024):
    rows, d = c_pad.shape
    n = w_ada.shape[1]
    return pl.pallas_call(
        _ada_kernel,
        out_shape=jax.ShapeDtypeStruct((rows, n), F32),
        grid=(n // bn,),
        in_specs=[
            pl.BlockSpec((rows, d), lambda j: (0, 0)),
            pl.BlockSpec((d, bn), lambda j: (0, j)),
            pl.BlockSpec((1, bn), lambda j: (0, j)),
        ],
        out_specs=pl.BlockSpec((rows, bn), lambda j: (0, j)),
        compiler_params=pltpu.CompilerParams(
            dimension_semantics=("arbitrary",), vmem_limit_bytes=V7X_VMEM_LIMIT_BYTES),
        name="ada",
    )(c_pad, w_ada, b_ada)


_PROJ_RC = 512
_PROJ_CC = 256
_PROJ_NORM_ROWS = 256


def _proj_kernel(x_ref, sh_ref, sc_ref, nw_ref, w_ref, rc_ref, rs_ref, dc_ref, da_ref, db_ref,
                 o_ref, h_ref, wb_ref, *, bm, bn):
    j = pl.program_id(1)
    n_rc = bm // _PROJ_RC

    def norm_modulate():
        for r in range(bm // _PROJ_NORM_ROWS):
            rows = slice(r * _PROJ_NORM_ROWS, (r + 1) * _PROJ_NORM_ROWS)
            x = x_ref[rows, :]
            ms = jnp.mean(x * x, axis=-1, keepdims=True)
            y = x * lax.rsqrt(ms + NORM_EPS) * nw_ref[...]
            h_ref[rows, :] = (y * (1.0 + sc_ref[0]) + sh_ref[0]).astype(BF16)

    def run(epilogue):
        for cc in range(bn // _PROJ_CC):
            cols = slice(cc * _PROJ_CC, (cc + 1) * _PROJ_CC)
            wb_ref[:, cols] = w_ref[:, cols].astype(BF16)
            for r in range(n_rc):
                rows = slice(r * _PROJ_RC, (r + 1) * _PROJ_RC)
                acc = jnp.dot(h_ref[rows, :], wb_ref[:, cols], preferred_element_type=F32)
                for hb in range(_PROJ_CC // HEAD_DIM):
                    a = acc[:, hb * HEAD_DIM:(hb + 1) * HEAD_DIM]
                    c0 = cc * _PROJ_CC + hb * HEAD_DIM
                    o_ref[rows, c0:c0 + HEAD_DIM] = epilogue(a, rows).astype(o_ref.dtype)

    def ret_rotary(a, rows):
        return a * rc_ref[rows, :] + pltpu.roll(a, HEAD_DIM // 2, 1) * rs_ref[rows, :]

    @pl.when(j == 0)
    def _():
        norm_modulate()
        run(ret_rotary)

    @pl.when(j == 1)
    def _():
        run(lambda a, rows: ret_rotary(a, rows) * (HEAD_DIM ** -0.5))

    @pl.when(jnp.logical_or(j == 4, j == 5))
    def _():
        scale = jnp.where(j == 4, DIFF_DQK ** -0.5 * math.log2(math.e), 1.0).astype(F32)
        half = DIFF_ROT_DIM // 2

        def ep(a, rows):
            rot = (a * dc_ref[rows, :]
                   + pltpu.roll(a, HEAD_DIM - half, 1) * da_ref[rows, :]
                   + pltpu.roll(a, half, 1) * db_ref[rows, :])
            return rot * scale
        run(ep)

    @pl.when(jnp.logical_or(jnp.logical_or(j == 2, j == 3), j == 6))
    def _():
        run(lambda a, rows: a)


def _proj(x2d, mod3, norm_w, w_in, tabs, *, seq, bm=1024, bn=1024):
    bt, d = x2d.shape
    n = w_in.shape[1]
    tiles_per_seq = seq // bm
    tab_spec = pl.BlockSpec((bm, HEAD_DIM), lambda m, j: (m % tiles_per_seq, 0))
    return pl.pallas_call(
        functools.partial(_proj_kernel, bm=bm, bn=bn),
        out_shape=jax.ShapeDtypeStruct((bt, n), BF16),
        grid=(bt // bm, n // bn),
        in_specs=[
            pl.BlockSpec((bm, d), lambda m, j: (m, 0)),
            pl.BlockSpec((1, 1, d), lambda m, j: ((m // tiles_per_seq) * 6 + 0, 0, 0)),
            pl.BlockSpec((1, 1, d), lambda m, j: ((m // tiles_per_seq) * 6 + 1, 0, 0)),
            pl.BlockSpec((1, d), lambda m, j: (0, 0)),
            pl.BlockSpec((d, bn), lambda m, j: (0, j)),
            tab_spec, tab_spec, tab_spec, tab_spec, tab_spec,
        ],
        out_specs=pl.BlockSpec((bm, bn), lambda m, j: (m, j)),
        scratch_shapes=[pltpu.VMEM((bm, d), BF16), pltpu.VMEM((d, bn), BF16)],
        compiler_params=pltpu.CompilerParams(
            dimension_semantics=("arbitrary", "arbitrary"), vmem_limit_bytes=V7X_VMEM_LIMIT_BYTES),
        name="proj",
    )(x2d, mod3, mod3, norm_w, w_in, *tabs)


def _rotary_tables(seq):
    pos = jnp.arange(seq, dtype=F32)[:, None]
    lane = jnp.arange(HEAD_DIM)
    half = HEAD_DIM // 2
    inv = RET_THETA ** (-(2 * (lane % half)).astype(F32) / HEAD_DIM)
    ang = pos * inv[None, :]
    ret_c = jnp.cos(ang)
    ret_s = jnp.sin(ang) * jnp.where(lane < half, -1.0, 1.0)[None, :]
    half = DIFF_ROT_DIM // 2
    within = lane % DIFF_DQK
    inv = ROPE_THETA ** (-(2 * (within % half)).astype(F32) / DIFF_ROT_DIM)
    ang = pos * jnp.where(within < DIFF_ROT_DIM, inv, 0.0)[None, :]
    sin = jnp.sin(ang)
    dif_c = jnp.cos(ang)
    dif_a = sin * jnp.where(within < half, -1.0, 0.0)[None, :]
    dif_b = sin * jnp.where((within >= half) & (within < DIFF_ROT_DIM), 1.0, 0.0)[None, :]
    return ret_c, ret_s, dif_c, dif_a, dif_b


def _ret_kernel(q_ref, k_ref, v_ref, g_ref, gw_ref, o_ref, s_ref, *, seq):
    C = RET_CHUNK
    nc = seq // C
    head = (pl.program_id(0) % N_RET_HEADS).astype(F32)
    ii = lax.broadcasted_iota(jnp.int32, (C, C), 0).astype(F32)
    jj = lax.broadcasted_iota(jnp.int32, (C, C), 1).astype(F32)
    lg = jnp.log(1.0 - jnp.exp2(jnp.zeros((C, C), F32) - 5.0 - head))
    rel = ii - jj
    decay = jnp.where(rel >= 0, jnp.exp(jnp.maximum(rel, 0.0) * lg), 0.0)
    xi = jnp.exp((ii + 1.0) * lg)
    zeta = jnp.exp((C - 1.0 - ii) * lg)
    chunk_decay = jnp.exp(C * lg)

    def scan_body(c, state):
        rows = pl.ds(pl.multiple_of(c * C, C), C)
        s_ref[c] = state.astype(BF16)
        kz = (k_ref[rows, :].astype(F32) * zeta).astype(BF16)
        kv = lax.dot_general(kz, v_ref[rows, :], (((0,), (0,)), ((), ())),
                             preferred_element_type=F32)
        return state * chunk_decay + kv
    lax.fori_loop(0, nc, scan_body, jnp.zeros((C, C), F32), unroll=8)

    gw = gw_ref[...]

    def out_body(c, carry):
        rows = pl.ds(pl.multiple_of(c * C, C), C)
        q = q_ref[rows, :]
        inner = lax.dot_general(q, k_ref[rows, :], (((1,), (1,)), ((), ())),
                                preferred_element_type=F32) * decay
        o = jnp.dot(inner.astype(BF16), v_ref[rows, :], preferred_element_type=F32)
        o = o + jnp.dot(q, s_ref[c], preferred_element_type=F32) * xi
        mu = jnp.mean(o, axis=-1, keepdims=True)
        d = o - mu
        var = jnp.mean(d * d, axis=-1, keepdims=True)
        y = d * lax.rsqrt(var + GN_EPS) * gw
        o_ref[rows, :] = (y * _silu(g_ref[rows, :].astype(F32))).astype(o_ref.dtype)
        return carry
    lax.fori_loop(0, nc, out_body, 0, unroll=8)


def _retention(proj, gn_w, *, batch, seq):
    nh = N_RET_HEADS
    blk = lambda off: pl.BlockSpec((seq, HEAD_DIM), lambda s: (s // nh, off + s % nh))
    return pl.pallas_call(
        functools.partial(_ret_kernel, seq=seq),
        out_shape=jax.ShapeDtypeStruct((batch * seq, nh * HEAD_DIM), BF16),
        grid=(batch * nh,),
        in_specs=[blk(0), blk(nh), blk(2 * nh), blk(3 * nh),
                  pl.BlockSpec((1, HEAD_DIM), lambda s: (0, s % nh))],
        out_specs=pl.BlockSpec((seq, HEAD_DIM), lambda s: (s // nh, s % nh)),
        scratch_shapes=[pltpu.VMEM((seq // RET_CHUNK, RET_CHUNK, HEAD_DIM), BF16)],
        compiler_params=pltpu.CompilerParams(
            dimension_semantics=("arbitrary",), vmem_limit_bytes=V7X_VMEM_LIMIT_BYTES),
        name="retention",
    )(proj, proj, proj, proj, gn_w)


_ATTN_STRIP = 256


def _attn_kernel(q_ref, k_ref, v_ref, lq1_ref, lk1_ref, lq2_ref, lk2_ref, sw_ref, *rest,
                 blk, lam_init, n_cast):
    cast_src = rest[:n_cast]
    o_ref = rest[n_cast]
    cast_dst = rest[n_cast + 1:2 * n_cast + 1]
    vt_ref, m_ref, l_ref, acc_ref, sa_ref, sb_ref = rest[2 * n_cast + 1:]
    _attn_body(q_ref, k_ref, v_ref, lq1_ref, lk1_ref, lq2_ref, lk2_ref, sw_ref, cast_src,
               o_ref, cast_dst, vt_ref, m_ref, l_ref, acc_ref, sa_ref, sb_ref,
               blk=blk, lam_init=lam_init)


def _attn_body(q_ref, k_ref, v_ref, lq1_ref, lk1_ref, lq2_ref, lk2_ref, sw_ref, cast_src,
               o_ref, cast_dst, vt_ref, m_ref, l_ref, acc_ref, sa_ref, sb_ref, *, blk, lam_init):
    nblk = vt_ref.shape[0]
    nstrip = blk // _ATTN_STRIP
    assert nblk % 2 == 0

    for jb in range(nblk):
        vt_ref[jb] = v_ref[jb * blk:(jb + 1) * blk, :].T

    lane = lax.broadcasted_iota(jnp.int32, (_ATTN_STRIP, HEAD_DIM), 1)
    comp_mask = (lane < DIFF_DQK, lane >= DIFF_DQK)
    chains = [(comp, s) for comp in range(2) for s in range(nstrip)]

    def rows_of(i, s):
        start = i * blk + s * _ATTN_STRIP
        if not isinstance(start, int):
            start = pl.multiple_of(start, _ATTN_STRIP)
        return pl.ds(start, _ATTN_STRIP)

    def keys_needed(s, diagonal):
        return (s + 1) * _ATTN_STRIP if diagonal else blk

    def scores(item, s_ref, diagonal=False):
        i, j = item
        jrow = j * blk
        if not isinstance(jrow, int):
            jrow = pl.multiple_of(jrow, blk)
        for c, (comp, s) in enumerate(chains):
            nk = keys_needed(s, diagonal)
            qv = q_ref[rows_of(i, s), :]
            qv = jnp.where(comp_mask[comp], qv, jnp.zeros_like(qv))
            s_ref[c, :nk, :] = lax.dot_general(k_ref[pl.ds(jrow, nk), :], qv,
                                               (((1,), (1,)), ((), ())),
                                               preferred_element_type=F32)

    def softmax_av(item, s_ref, diagonal):
        i, j = item
        vtb = vt_ref[j]
        for c, (comp, s) in enumerate(chains):
            nk = keys_needed(s, diagonal)
            st = s_ref[c, :nk, :]
            idx = i * nstrip + s
            if diagonal:
                kpos = lax.broadcasted_iota(jnp.int32, st.shape, 0)
                qpos = lax.broadcasted_iota(jnp.int32, st.shape, 1) + s * _ATTN_STRIP
                st = jnp.where(kpos <= qpos, st, -jnp.inf)
                m_new = jnp.max(st, axis=0, keepdims=True)
                p = jnp.exp2(st - m_new)
                l_ref[comp, idx] = jnp.sum(p, axis=0, keepdims=True)
                acc_ref[comp, idx] = jnp.dot(vtb[:, :nk], p.astype(BF16),
                                             preferred_element_type=F32)
            else:
                m_old = m_ref[comp, idx]
                m_new = jnp.maximum(m_old, jnp.max(st, axis=0, keepdims=True))
                alpha = jnp.exp2(m_old - m_new)
                p = jnp.exp2(st - m_new)
                l_ref[comp, idx] = alpha * l_ref[comp, idx] + jnp.sum(p, axis=0, keepdims=True)
                acc_ref[comp, idx] = alpha * acc_ref[comp, idx] + jnp.dot(
                    vtb, p.astype(BF16), preferred_element_type=F32)
            m_ref[comp, idx] = m_new

    def next_item(item):
        i, j = item
        wrap = j + 1 >= i
        return jnp.where(wrap, i + 1, i), jnp.where(wrap, 0, j + 1)

    n_parts = nblk // 2
    scores((0, 0), sa_ref, True)
    for t in range(n_parts):
        for src, dst in zip(cast_src, cast_dst):
            part = src.shape[0] // n_parts
            dst[t * part:(t + 1) * part, :] = src[t * part:(t + 1) * part, :].astype(dst.dtype)
        scores((2 * t + 1, 2 * t + 1), sb_ref, True)
        softmax_av((2 * t, 2 * t), sa_ref, True)
        if 2 * t + 2 < nblk:
            scores((2 * t + 2, 2 * t + 2), sa_ref, True)
        else:
            scores((1, 0), sa_ref)
        softmax_av((2 * t + 1, 2 * t + 1), sb_ref, True)

    n_off = nblk * (nblk - 1) // 2
    assert n_off % 2 == 0

    def pair(t, item):
        item_b = next_item(item)
        item_c = next_item(item_b)
        scores(item_b, sb_ref)
        softmax_av(item, sa_ref, False)
        scores(item_c, sa_ref)
        softmax_av(item_b, sb_ref, False)
        return item_c
    item = lax.fori_loop(0, n_off // 2 - 1, pair, (jnp.int32(1), jnp.int32(0)))
    item_b = next_item(item)
    scores(item_b, sb_ref)
    softmax_av(item, sa_ref, False)
    softmax_av(item_b, sb_ref, False)

    lam = (jnp.exp(jnp.sum(lq1_ref[...] * lk1_ref[...], axis=-1, keepdims=True))
           - jnp.exp(jnp.sum(lq2_ref[...] * lk2_ref[...], axis=-1, keepdims=True))
           + lam_init)
    sw = sw_ref[...] * (1.0 - lam_init)

    def finish(idx, carry):
        yt = acc_ref[0, idx] / l_ref[0, idx] - lam * (acc_ref[1, idx] / l_ref[1, idx])
        yt = yt * lax.rsqrt(jnp.mean(yt * yt, axis=0, keepdims=True) + GN_EPS)
        rows = pl.ds(pl.multiple_of(idx * _ATTN_STRIP, _ATTN_STRIP), _ATTN_STRIP)
        o_ref[rows, :] = (yt.T * sw).astype(o_ref.dtype)
        return carry
    lax.fori_loop(0, nblk * nstrip, finish, 0)


def _diff_attention(proj, lq1, lk1, lq2, lk2, subln_w, cast_weights, *, batch, seq, lam_init,
                    blk=512):
    nh = N_DIFF_HEADS
    nblk = seq // blk
    nchain = 2 * blk // _ATTN_STRIP
    nstrips = seq // _ATTN_STRIP
    steps = batch * nh
    q_off, k_off, v_off = 4 * nh, 5 * nh, 6 * nh
    head_blk = lambda off: pl.BlockSpec((seq, HEAD_DIM), lambda s: (s // nh, off + s % nh))
    lam_spec = pl.BlockSpec((1, DIFF_DQK), lambda s: (0, 0))
    cast_specs = [pl.BlockSpec((w.shape[0] // steps, w.shape[1]), lambda s: (s, 0))
                  for w in cast_weights]
    outs = pl.pallas_call(
        functools.partial(_attn_kernel, blk=blk, lam_init=lam_init, n_cast=len(cast_weights)),
        out_shape=[jax.ShapeDtypeStruct((batch * seq, nh * HEAD_DIM), BF16)]
        + [jax.ShapeDtypeStruct(w.shape, BF16) for w in cast_weights],
        grid=(steps,),
        in_specs=[
            head_blk(q_off), head_blk(k_off), head_blk(v_off),
            lam_spec, lam_spec, lam_spec, lam_spec,
            pl.BlockSpec((1, HEAD_DIM), lambda s: (0, s % nh)),
        ] + cast_specs,
        out_specs=[pl.BlockSpec((seq, HEAD_DIM), lambda s: (s // nh, s % nh))] + cast_specs,
        scratch_shapes=[
            pltpu.VMEM((nblk, HEAD_DIM, blk), BF16),
            pltpu.VMEM((2, nstrips, 1, _ATTN_STRIP), F32),
            pltpu.VMEM((2, nstrips, 1, _ATTN_STRIP), F32),
            pltpu.VMEM((2, nstrips, HEAD_DIM, _ATTN_STRIP), F32),
            pltpu.VMEM((nchain, blk, _ATTN_STRIP), F32),
            pltpu.VMEM((nchain, blk, _ATTN_STRIP), F32),
        ],
        compiler_params=pltpu.CompilerParams(
            dimension_semantics=("arbitrary",), vmem_limit_bytes=V7X_VMEM_LIMIT_BYTES),
        name="diff_attn",
    )(proj, proj, proj, lq1, lk1, lq2, lk2, subln_w, *cast_weights)
    return outs[0], outs[1:]


_OUT_RC = 512


def _outproj_kernel(yr_ref, yd_ref, w_ref, x_ref, g_ref, x1_ref, *, bm, wr):
    for r in range(bm // _OUT_RC):
        rows = slice(r * _OUT_RC, (r + 1) * _OUT_RC)
        mixed = (jnp.dot(yr_ref[rows, :], w_ref[:wr, :], preferred_element_type=F32)
                 + jnp.dot(yd_ref[rows, :], w_ref[wr:, :], preferred_element_type=F32))
        x1_ref[rows, :] = x_ref[rows, :] + g_ref[0] * mixed


def _outproj(y_ret, y_dif, w_out_bf16, x2d, mod3, *, seq, bm=1024, bn=512):
    bt, d = x2d.shape
    wr = y_ret.shape[1]
    wd = y_dif.shape[1]
    tiles_per_seq = seq // bm
    return pl.pallas_call(
        functools.partial(_outproj_kernel, bm=bm, wr=wr),
        out_shape=jax.ShapeDtypeStruct((bt, d), F32),
        grid=(bt // bm, d // bn),
        in_specs=[
            pl.BlockSpec((bm, wr), lambda m, n: (m, 0)),
            pl.BlockSpec((bm, wd), lambda m, n: (m, 0)),
            pl.BlockSpec((wr + wd, bn), lambda m, n: (0, n)),
            pl.BlockSpec((bm, bn), lambda m, n: (m, n)),
            pl.BlockSpec((1, 1, bn), lambda m, n: ((m // tiles_per_seq) * 6 + 2, 0, n)),
        ],
        out_specs=pl.BlockSpec((bm, bn), lambda m, n: (m, n)),
        compiler_params=pltpu.CompilerParams(
            dimension_semantics=("arbitrary", "arbitrary"), vmem_limit_bytes=V7X_VMEM_LIMIT_BYTES),
        name="outproj",
    )(y_ret, y_dif, w_out_bf16, x2d, mod3)


_MLP_RC = 512
_MLP_NORM_ROWS = 256
_MLP_FINAL_ROWS = 16


def _mlp_kernel(x1_ref, sh_ref, sc_ref, nw_ref, w1_ref, w2_ref, g_ref, fw_ref, o_ref, h_ref,
                *, bm, final_norm):
    f = pl.program_id(1)
    nf = pl.num_programs(1)

    def ffn(first):
        for r in range(bm // _MLP_RC):
            rows = slice(r * _MLP_RC, (r + 1) * _MLP_RC)
            a = jnp.dot(h_ref[rows, :], w1_ref[...], preferred_element_type=F32)
            a = jnp.maximum(a, 0.0)
            a = (a * a).astype(BF16)
            part = jnp.dot(a, w2_ref[...], preferred_element_type=F32)
            if first:
                o_ref[rows, :] = part
            else:
                o_ref[rows, :] += part

    @pl.when(f == 0)
    def _():
        for r in range(bm // _MLP_NORM_ROWS):
            rows = slice(r * _MLP_NORM_ROWS, (r + 1) * _MLP_NORM_ROWS)
            x = x1_ref[rows, :]
            ms = jnp.mean(x * x, axis=-1, keepdims=True)
            y = x * lax.rsqrt(ms + NORM_EPS) * nw_ref[...]
            h_ref[rows, :] = (y * (1.0 + sc_ref[0]) + sh_ref[0]).astype(BF16)
        ffn(True)

    @pl.when(f > 0)
    def _():
        ffn(False)

    @pl.when(f == nf - 1)
    def _():
        for r in range(bm // _MLP_FINAL_ROWS):
            rows = slice(r * _MLP_FINAL_ROWS, (r + 1) * _MLP_FINAL_ROWS)
            x2 = x1_ref[rows, :] + g_ref[0] * o_ref[rows, :]
            if final_norm:
                ms = jnp.mean(x2 * x2, axis=-1, keepdims=True)
                x2 = x2 * lax.rsqrt(ms + NORM_EPS) * fw_ref[...]
            o_ref[rows, :] = x2


def _mlp(x1, w1_bf16, w2_bf16, mod3, norm_w, final_w, *, seq, final_norm, bm=1024, bf=512):
    bt, d = x1.shape
    dff = w1_bf16.shape[1]
    tiles_per_seq = seq // bm
    mod_spec = lambda idx: pl.BlockSpec(
        (1, 1, d), lambda m, f: ((m // tiles_per_seq) * 6 + idx, 0, 0))
    return pl.pallas_call(
        functools.partial(_mlp_kernel, bm=bm, final_norm=final_norm),
        out_shape=jax.ShapeDtypeStruct((bt, d), F32),
        grid=(bt // bm, dff // bf),
        in_specs=[
            pl.BlockSpec((bm, d), lambda m, f: (m, 0)),
            mod_spec(3), mod_spec(4),
            pl.BlockSpec((1, d), lambda m, f: (0, 0)),
            pl.BlockSpec((d, bf), lambda m, f: (0, f)),
            pl.BlockSpec((bf, d), lambda m, f: (f, 0)),
            mod_spec(5),
            pl.BlockSpec((1, d), lambda m, f: (0, 0)),
        ],
        out_specs=pl.BlockSpec((bm, d), lambda m, f: (m, 0)),
        scratch_shapes=[pltpu.VMEM((bm, d), BF16)],
        compiler_params=pltpu.CompilerParams(
            dimension_semantics=("arbitrary", "arbitrary"), vmem_limit_bytes=V7X_VMEM_LIMIT_BYTES),
        name="mlp",
    )(x1, mod3, mod3, norm_w, w1_bf16, w2_bf16, mod3, final_w)


def kernel(x, c, w_ada, b_ada, norm1_w, norm2_w, w_in, ret_gn_w, diff_lq1, diff_lk1, diff_lq2,
           diff_lk2, diff_subln_w, w_out, w_mlp1, w_mlp2, final_norm_w):
    batch, seq, d = x.shape
    depth = w_ada.shape[0]
    c_pad = jnp.pad(c, ((0, 8 - batch), (0, 0)))
    tabs = _rotary_tables(seq)
    x2d = x.reshape(batch * seq, d)
    final_w = final_norm_w.reshape(1, d)

    for l in range(depth):
        lam_init = 0.8 - 0.6 * math.exp(-0.3 * l)
        mod = _ada(c_pad, w_ada[l], b_ada[l].reshape(1, -1))
        mod3 = mod[:batch].reshape(batch * 6, 1, d)
        proj = _proj(x2d, mod3, norm1_w[l].reshape(1, d), w_in[l], tabs, seq=seq)
        y_ret = _retention(proj, ret_gn_w[l].reshape(1, -1), batch=batch, seq=seq)
        y_dif, (w_out_b, w1_b, w2_b) = _diff_attention(
            proj, diff_lq1[l].reshape(1, -1), diff_lk1[l].reshape(1, -1),
            diff_lq2[l].reshape(1, -1), diff_lk2[l].reshape(1, -1),
            diff_subln_w[l].reshape(1, -1), (w_out[l], w_mlp1[l], w_mlp2[l]),
            batch=batch, seq=seq, lam_init=lam_init)
        x1 = _outproj(y_ret, y_dif, w_out_b, x2d, mod3, seq=seq)
        x2d = _mlp(x1, w1_b, w2_b, mod3, norm2_w[l].reshape(1, d), final_w, seq=seq,
                   final_norm=(l == depth - 1))
    if depth == 0:
        raise ValueError("depth must be positive")
    return x2d.reshape(batch, seq, d)
```

```python
import functools
import math

import jax
import jax.numpy as jnp
from jax import lax
from jax.experimental import pallas as pl
from jax.experimental.pallas import tpu as pltpu

F32 = jnp.float32
BF16 = jnp.bfloat16

HEAD_DIM = 128
N_RET_HEADS = 8
N_DIFF_HEADS = 8
DIFF_DQK = HEAD_DIM // 2
DIFF_ROT_DIM = DIFF_DQK // 4
RET_CHUNK = 128
ROPE_THETA = 500000.0
RET_THETA = 10000.0
NORM_EPS = 1e-6
GN_EPS = 1e-5

V7X_VMEM_LIMIT_BYTES = 58 * 1024 * 1024


def _silu(v):
    return v * jax.nn.sigmoid(v)


def _ada_kernel(c_ref, w_ref, b_ref, rot_ref, o_ref, *tab_refs, tab_rows):
    s = _silu(c_ref[...]).astype(BF16)
    o_ref[...] = jnp.dot(s, w_ref[...].astype(BF16), preferred_element_type=F32) + b_ref[...]
    row0 = pl.program_id(0) * tab_rows
    pos = (lax.broadcasted_iota(jnp.int32, (tab_rows, HEAD_DIM), 0) + row0).astype(F32)
    ang_ret = pos * rot_ref[0:1, :]
    ang_dif = pos * rot_ref[2:3, :]
    sin_dif = jnp.sin(ang_dif)
    ret_c_ref, ret_s_ref, dif_c_ref, dif_a_ref, dif_b_ref = tab_refs
    ret_c_ref[...] = jnp.cos(ang_ret)
    ret_s_ref[...] = jnp.sin(ang_ret) * rot_ref[1:2, :]
    dif_c_ref[...] = jnp.cos(ang_dif)
    dif_a_ref[...] = sin_dif * rot_ref[3:4, :]
    dif_b_ref[...] = sin_dif * rot_ref[4:5, :]


def _ada(c_pad, w_ada, b_ada, rot, *, seq, bn=768):
    rows, d = c_pad.shape
    n = w_ada.shape[1]
    steps = n // bn
    tab_rows = seq // steps
    assert steps * bn == n and steps * tab_rows == seq and tab_rows % 8 == 0
    tab_spec = pl.BlockSpec((tab_rows, HEAD_DIM), lambda j: (j, 0))
    outs = pl.pallas_call(
        functools.partial(_ada_kernel, tab_rows=tab_rows),
        out_shape=[jax.ShapeDtypeStruct((rows, n), F32)]
        + [jax.ShapeDtypeStruct((seq, HEAD_DIM), F32)] * 5,
        grid=(steps,),
        in_specs=[
            pl.BlockSpec((rows, d), lambda j: (0, 0)),
            pl.BlockSpec((d, bn), lambda j: (0, j)),
            pl.BlockSpec((1, bn), lambda j: (0, j)),
            pl.BlockSpec(rot.shape, lambda j: (0, 0)),
        ],
        out_specs=[pl.BlockSpec((rows, bn), lambda j: (0, j))] + [tab_spec] * 5,
        compiler_params=pltpu.CompilerParams(
            dimension_semantics=("arbitrary",), vmem_limit_bytes=V7X_VMEM_LIMIT_BYTES),
        name="ada",
    )(c_pad, w_ada, b_ada, rot)
    return outs[0], tuple(outs[1:])


_PROJ_RC = 512
_PROJ_CC = 256
_PROJ_NORM_ROWS = 256


def _proj_kernel(x_ref, sh_ref, sc_ref, nw_ref, w_ref, rc_ref, rs_ref, dc_ref, da_ref, db_ref,
                 o_ref, h_ref, wb_ref, *, bm, bn):
    j = pl.program_id(1)
    n_rc = bm // _PROJ_RC

    def norm_modulate():
        for r in range(bm // _PROJ_NORM_ROWS):
            rows = slice(r * _PROJ_NORM_ROWS, (r + 1) * _PROJ_NORM_ROWS)
            x = x_ref[rows, :]
            ms = jnp.mean(x * x, axis=-1, keepdims=True)
            y = x * lax.rsqrt(ms + NORM_EPS) * nw_ref[...]
            h_ref[rows, :] = (y * (1.0 + sc_ref[0]) + sh_ref[0]).astype(BF16)

    def run(epilogue):
        for cc in range(bn // _PROJ_CC):
            cols = slice(cc * _PROJ_CC, (cc + 1) * _PROJ_CC)
            wb_ref[:, cols] = w_ref[:, cols].astype(BF16)
            for r in range(n_rc):
                rows = slice(r * _PROJ_RC, (r + 1) * _PROJ_RC)
                acc = jnp.dot(h_ref[rows, :], wb_ref[:, cols], preferred_element_type=F32)
                for hb in range(_PROJ_CC // HEAD_DIM):
                    a = acc[:, hb * HEAD_DIM:(hb + 1) * HEAD_DIM]
                    c0 = cc * _PROJ_CC + hb * HEAD_DIM
                    o_ref[rows, c0:c0 + HEAD_DIM] = epilogue(a, rows).astype(o_ref.dtype)

    def ret_rotary(a, rows):
        return a * rc_ref[rows, :] + pltpu.roll(a, HEAD_DIM // 2, 1) * rs_ref[rows, :]

    @pl.when(j == 0)
    def _():
        norm_modulate()
        run(ret_rotary)

    @pl.when(j == 1)
    def _():
        run(lambda a, rows: ret_rotary(a, rows) * (HEAD_DIM ** -0.5))

    @pl.when(jnp.logical_or(j == 4, j == 5))
    def _():
        scale = jnp.where(j == 4, DIFF_DQK ** -0.5 * math.log2(math.e), 1.0).astype(F32)
        half = DIFF_ROT_DIM // 2

        def ep(a, rows):
            rot = (a * dc_ref[rows, :]
                   + pltpu.roll(a, HEAD_DIM - half, 1) * da_ref[rows, :]
                   + pltpu.roll(a, half, 1) * db_ref[rows, :])
            return rot * scale
        run(ep)

    @pl.when(jnp.logical_or(jnp.logical_or(j == 2, j == 3), j == 6))
    def _():
        run(lambda a, rows: a)


def _proj(x2d, mod3, norm_w, w_in, tabs, *, seq, bm=1024, bn=1024):
    bt, d = x2d.shape
    n = w_in.shape[1]
    tiles_per_seq = seq // bm
    tab_spec = pl.BlockSpec((bm, HEAD_DIM), lambda m, j: (m % tiles_per_seq, 0))
    return pl.pallas_call(
        functools.partial(_proj_kernel, bm=bm, bn=bn),
        out_shape=jax.ShapeDtypeStruct((bt, n), BF16),
        grid=(bt // bm, n // bn),
        in_specs=[
            pl.BlockSpec((bm, d), lambda m, j: (m, 0)),
            pl.BlockSpec((1, 1, d), lambda m, j: ((m // tiles_per_seq) * 6 + 0, 0, 0)),
            pl.BlockSpec((1, 1, d), lambda m, j: ((m // tiles_per_seq) * 6 + 1, 0, 0)),
            pl.BlockSpec((1, d), lambda m, j: (0, 0)),
            pl.BlockSpec((d, bn), lambda m, j: (0, j)),
            tab_spec, tab_spec, tab_spec, tab_spec, tab_spec,
        ],
        out_specs=pl.BlockSpec((bm, bn), lambda m, j: (m, j)),
        scratch_shapes=[pltpu.VMEM((bm, d), BF16), pltpu.VMEM((d, bn), BF16)],
        compiler_params=pltpu.CompilerParams(
            dimension_semantics=("arbitrary", "arbitrary"), vmem_limit_bytes=V7X_VMEM_LIMIT_BYTES),
        name="proj",
    )(x2d, mod3, mod3, norm_w, w_in, *tabs)


def _rotary_lane_constants():
    lane = jnp.arange(HEAD_DIM)
    half = HEAD_DIM // 2
    inv_ret = RET_THETA ** (-(2 * (lane % half)).astype(F32) / HEAD_DIM)
    sign_ret = jnp.where(lane < half, -1.0, 1.0)
    half = DIFF_ROT_DIM // 2
    within = lane % DIFF_DQK
    inv_dif = ROPE_THETA ** (-(2 * (within % half)).astype(F32) / DIFF_ROT_DIM)
    inv_dif = jnp.where(within < DIFF_ROT_DIM, inv_dif, 0.0)
    sel_a = jnp.where(within < half, -1.0, 0.0)
    sel_b = jnp.where((within >= half) & (within < DIFF_ROT_DIM), 1.0, 0.0)
    zero = jnp.zeros((HEAD_DIM,), F32)
    return jnp.stack([inv_ret, sign_ret, inv_dif, sel_a, sel_b, zero, zero, zero]).astype(F32)


def _ret_kernel(q_ref, k_ref, v_ref, g_ref, gw_ref, o_ref, s_ref, *, seq):
    C = RET_CHUNK
    nc = seq // C
    head = (pl.program_id(0) % N_RET_HEADS).astype(F32)
    ii = lax.broadcasted_iota(jnp.int32, (C, C), 0).astype(F32)
    jj = lax.broadcasted_iota(jnp.int32, (C, C), 1).astype(F32)
    lg = jnp.log(1.0 - jnp.exp2(jnp.zeros((C, C), F32) - 5.0 - head))
    rel = ii - jj
    decay = jnp.where(rel >= 0, jnp.exp(jnp.maximum(rel, 0.0) * lg), 0.0)
    xi = jnp.exp((ii + 1.0) * lg)
    zeta = jnp.exp((C - 1.0 - ii) * lg)
    chunk_decay = jnp.exp(C * lg)

    def scan_body(c, state):
        rows = pl.ds(pl.multiple_of(c * C, C), C)
        s_ref[c] = state.astype(BF16)
        kz = (k_ref[rows, :].astype(F32) * zeta).astype(BF16)
        kv = lax.dot_general(kz, v_ref[rows, :], (((0,), (0,)), ((), ())),
                             preferred_element_type=F32)
        return state * chunk_decay + kv
    lax.fori_loop(0, nc, scan_body, jnp.zeros((C, C), F32), unroll=8)

    gw = gw_ref[...]

    def out_body(c, carry):
        rows = pl.ds(pl.multiple_of(c * C, C), C)
        q = q_ref[rows, :]
        inner = lax.dot_general(q, k_ref[rows, :], (((1,), (1,)), ((), ())),
                                preferred_element_type=F32) * decay
        o = jnp.dot(inner.astype(BF16), v_ref[rows, :], preferred_element_type=F32)
        o = o + jnp.dot(q, s_ref[c], preferred_element_type=F32) * xi
        mu = jnp.mean(o, axis=-1, keepdims=True)
        d = o - mu
        var = jnp.mean(d * d, axis=-1, keepdims=True)
        y = d * lax.rsqrt(var + GN_EPS) * gw
        o_ref[rows, :] = (y * _silu(g_ref[rows, :].astype(F32))).astype(o_ref.dtype)
        return carry
    lax.fori_loop(0, nc, out_body, 0, unroll=8)


def _retention(proj, gn_w, *, batch, seq):
    nh = N_RET_HEADS
    blk = lambda off: pl.BlockSpec((seq, HEAD_DIM), lambda s: (s // nh, off + s % nh))
    return pl.pallas_call(
        functools.partial(_ret_kernel, seq=seq),
        out_shape=jax.ShapeDtypeStruct((batch * seq, nh * HEAD_DIM), BF16),
        grid=(batch * nh,),
        in_specs=[blk(0), blk(nh), blk(2 * nh), blk(3 * nh),
                  pl.BlockSpec((1, HEAD_DIM), lambda s: (0, s % nh))],
        out_specs=pl.BlockSpec((seq, HEAD_DIM), lambda s: (s // nh, s % nh)),
        scratch_shapes=[pltpu.VMEM((seq // RET_CHUNK, RET_CHUNK, HEAD_DIM), BF16)],
        compiler_params=pltpu.CompilerParams(
            dimension_semantics=("arbitrary",), vmem_limit_bytes=V7X_VMEM_LIMIT_BYTES),
        name="retention",
    )(proj, proj, proj, proj, gn_w)


_ATTN_STRIP = 256


def _attn_kernel(q_ref, k_ref, v_ref, lq1_ref, lk1_ref, lq2_ref, lk2_ref, sw_ref, *rest,
                 blk, lam_init, n_cast):
    cast_src = rest[:n_cast]
    o_ref = rest[n_cast]
    cast_dst = rest[n_cast + 1:2 * n_cast + 1]
    vt_ref, qt_ref, m_ref, l_ref, acc_ref, sa_ref, sb_ref = rest[2 * n_cast + 1:]
    _attn_body(q_ref, k_ref, v_ref, lq1_ref, lk1_ref, lq2_ref, lk2_ref, sw_ref, cast_src,
               o_ref, cast_dst, vt_ref, qt_ref, m_ref, l_ref, acc_ref, sa_ref, sb_ref,
               blk=blk, lam_init=lam_init)


def _attn_body(q_ref, k_ref, v_ref, lq1_ref, lk1_ref, lq2_ref, lk2_ref, sw_ref, cast_src,
               o_ref, cast_dst, vt_ref, qt_ref, m_ref, l_ref, acc_ref, sa_ref, sb_ref, *, blk, lam_init):
    nblk = vt_ref.shape[0]
    nstrip = blk // _ATTN_STRIP
    assert nblk % 2 == 0

    for jb in range(nblk):
        vt_ref[jb] = v_ref[jb * blk:(jb + 1) * blk, :].T

    feat = lax.broadcasted_iota(jnp.int32, (HEAD_DIM, _ATTN_STRIP), 0)
    for idx in range(nblk * nstrip):
        qt = q_ref[idx * _ATTN_STRIP:(idx + 1) * _ATTN_STRIP, :].T
        zero = jnp.zeros_like(qt)
        qt_ref[0, idx] = jnp.where(feat < DIFF_DQK, qt, zero)
        qt_ref[1, idx] = jnp.where(feat >= DIFF_DQK, qt, zero)

    chains = [(comp, s) for comp in range(2) for s in range(nstrip)]

    def keys_needed(s, diagonal):
        return (s + 1) * _ATTN_STRIP if diagonal else blk

    def scores(item, s_ref, diagonal=False):
        i, j = item
        jrow = j * blk
        if not isinstance(jrow, int):
            jrow = pl.multiple_of(jrow, blk)
        for c, (comp, s) in enumerate(chains):
            nk = keys_needed(s, diagonal)
            s_ref[c, :nk, :] = jnp.dot(k_ref[pl.ds(jrow, nk), :], qt_ref[comp, i * nstrip + s],
                                       preferred_element_type=F32)

    def softmax_av(item, s_ref, diagonal):
        i, j = item
        vtb = vt_ref[j]
        for c, (comp, s) in enumerate(chains):
            nk = keys_needed(s, diagonal)
            st = s_ref[c, :nk, :]
            idx = i * nstrip + s
            if diagonal:
                kpos = lax.broadcasted_iota(jnp.int32, st.shape, 0)
                qpos = lax.broadcasted_iota(jnp.int32, st.shape, 1) + s * _ATTN_STRIP
                st = jnp.where(kpos <= qpos, st, -jnp.inf)
                m_new = jnp.max(st, axis=0, keepdims=True)
                p = jnp.exp2(st - m_new)
                l_ref[comp, idx] = jnp.sum(p, axis=0, keepdims=True)
                acc_ref[comp, idx] = jnp.dot(vtb[:, :nk], p.astype(BF16),
                                             preferred_element_type=F32)
            else:
                m_old = m_ref[comp, idx]
                m_new = jnp.maximum(m_old, jnp.max(st, axis=0, keepdims=True))
                alpha = jnp.exp2(m_old - m_new)
                p = jnp.exp2(st - m_new)
                l_ref[comp, idx] = alpha * l_ref[comp, idx] + jnp.sum(p, axis=0, keepdims=True)
                acc_ref[comp, idx] = alpha * acc_ref[comp, idx] + jnp.dot(
                    vtb, p.astype(BF16), preferred_element_type=F32)
            m_ref[comp, idx] = m_new

    def next_item(item):
        i, j = item
        wrap = j + 1 >= i
        return jnp.where(wrap, i + 1, i), jnp.where(wrap, 0, j + 1)

    n_parts = nblk // 2
    scores((0, 0), sa_ref, True)
    for t in range(n_parts):
        for src, dst in zip(cast_src, cast_dst):
            part = src.shape[0] // n_parts
            dst[t * part:(t + 1) * part, :] = src[t * part:(t + 1) * part, :].astype(dst.dtype)
        scores((2 * t + 1, 2 * t + 1), sb_ref, True)
        softmax_av((2 * t, 2 * t), sa_ref, True)
        if 2 * t + 2 < nblk:
            scores((2 * t + 2, 2 * t + 2), sa_ref, True)
        else:
            scores((1, 0), sa_ref)
        softmax_av((2 * t + 1, 2 * t + 1), sb_ref, True)

    n_off = nblk * (nblk - 1) // 2
    assert n_off % 2 == 0

    def pair(t, item):
        item_b = next_item(item)
        item_c = next_item(item_b)
        scores(item_b, sb_ref)
        softmax_av(item, sa_ref, False)
        scores(item_c, sa_ref)
        softmax_av(item_b, sb_ref, False)
        return item_c
    item = lax.fori_loop(0, n_off // 2 - 1, pair, (jnp.int32(1), jnp.int32(0)))
    item_b = next_item(item)
    scores(item_b, sb_ref)
    softmax_av(item, sa_ref, False)
    softmax_av(item_b, sb_ref, False)

    lam = (jnp.exp(jnp.sum(lq1_ref[...] * lk1_ref[...], axis=-1, keepdims=True))
           - jnp.exp(jnp.sum(lq2_ref[...] * lk2_ref[...], axis=-1, keepdims=True))
           + lam_init)
    sw = sw_ref[...] * (1.0 - lam_init)

    def finish(idx, carry):
        yt = acc_ref[0, idx] / l_ref[0, idx] - lam * (acc_ref[1, idx] / l_ref[1, idx])
        yt = yt * lax.rsqrt(jnp.mean(yt * yt, axis=0, keepdims=True) + GN_EPS)
        rows = pl.ds(pl.multiple_of(idx * _ATTN_STRIP, _ATTN_STRIP), _ATTN_STRIP)
        o_ref[rows, :] = (yt.T * sw).astype(o_ref.dtype)
        return carry
    lax.fori_loop(0, nblk * nstrip, finish, 0)


def _diff_attention(proj, lq1, lk1, lq2, lk2, subln_w, cast_weights, *, batch, seq, lam_init,
                    blk=512):
    nh = N_DIFF_HEADS
    nblk = seq // blk
    nchain = 2 * blk // _ATTN_STRIP
    nstrips = seq // _ATTN_STRIP
    steps = batch * nh
    q_off, k_off, v_off = 4 * nh, 5 * nh, 6 * nh
    head_blk = lambda off: pl.BlockSpec((seq, HEAD_DIM), lambda s: (s // nh, off + s % nh))
    lam_spec = pl.BlockSpec((1, DIFF_DQK), lambda s: (0, 0))
    cast_specs = [pl.BlockSpec((w.shape[0] // steps, w.shape[1]), lambda s: (s, 0))
                  for w in cast_weights]
    outs = pl.pallas_call(
        functools.partial(_attn_kernel, blk=blk, lam_init=lam_init, n_cast=len(cast_weights)),
        out_shape=[jax.ShapeDtypeStruct((batch * seq, nh * HEAD_DIM), BF16)]
        + [jax.ShapeDtypeStruct(w.shape, BF16) for w in cast_weights],
        grid=(steps,),
        in_specs=[
            head_blk(q_off), head_blk(k_off), head_blk(v_off),
            lam_spec, lam_spec, lam_spec, lam_spec,
            pl.BlockSpec((1, HEAD_DIM), lambda s: (0, s % nh)),
        ] + cast_specs,
        out_specs=[pl.BlockSpec((seq, HEAD_DIM), lambda s: (s // nh, s % nh))] + cast_specs,
        scratch_shapes=[
            pltpu.VMEM((nblk, HEAD_DIM, blk), BF16),
            pltpu.VMEM((2, nstrips, HEAD_DIM, _ATTN_STRIP), BF16),
            pltpu.VMEM((2, nstrips, 1, _ATTN_STRIP), F32),
            pltpu.VMEM((2, nstrips, 1, _ATTN_STRIP), F32),
            pltpu.VMEM((2, nstrips, HEAD_DIM, _ATTN_STRIP), F32),
            pltpu.VMEM((nchain, blk, _ATTN_STRIP), F32),
            pltpu.VMEM((nchain, blk, _ATTN_STRIP), F32),
        ],
        compiler_params=pltpu.CompilerParams(
            dimension_semantics=("arbitrary",), vmem_limit_bytes=V7X_VMEM_LIMIT_BYTES),
        name="diff_attn",
    )(proj, proj, proj, lq1, lk1, lq2, lk2, subln_w, *cast_weights)
    return outs[0], outs[1:]


_OUT_RC = 512


def _outproj_kernel(yr_ref, yd_ref, w_ref, x_ref, g_ref, x1_ref, *, bm, wr):
    for r in range(bm // _OUT_RC):
        rows = slice(r * _OUT_RC, (r + 1) * _OUT_RC)
        mixed = (jnp.dot(yr_ref[rows, :], w_ref[:wr, :], preferred_element_type=F32)
                 + jnp.dot(yd_ref[rows, :], w_ref[wr:, :], preferred_element_type=F32))
        x1_ref[rows, :] = x_ref[rows, :] + g_ref[0] * mixed


def _outproj(y_ret, y_dif, w_out_bf16, x2d, mod3, *, seq, bm=512, bn=2048):
    bt, d = x2d.shape
    wr = y_ret.shape[1]
    wd = y_dif.shape[1]
    tiles_per_seq = seq // bm
    return pl.pallas_call(
        functools.partial(_outproj_kernel, bm=bm, wr=wr),
        out_shape=jax.ShapeDtypeStruct((bt, d), F32),
        grid=(bt // bm, d // bn),
        in_specs=[
            pl.BlockSpec((bm, wr), lambda m, n: (m, 0)),
            pl.BlockSpec((bm, wd), lambda m, n: (m, 0)),
            pl.BlockSpec((wr + wd, bn), lambda m, n: (0, n)),
            pl.BlockSpec((bm, bn), lambda m, n: (m, n)),
            pl.BlockSpec((1, 1, bn), lambda m, n: ((m // tiles_per_seq) * 6 + 2, 0, n)),
        ],
        out_specs=pl.BlockSpec((bm, bn), lambda m, n: (m, n)),
        compiler_params=pltpu.CompilerParams(
            dimension_semantics=("arbitrary", "arbitrary"), vmem_limit_bytes=V7X_VMEM_LIMIT_BYTES),
        name="outproj",
    )(y_ret, y_dif, w_out_bf16, x2d, mod3)


_MLP_RC = 512
_MLP_NORM_ROWS = 256
_MLP_FINAL_ROWS = 16


def _mlp_kernel(x1_ref, sh_ref, sc_ref, nw_ref, w1_ref, w2_ref, g_ref, fw_ref, o_ref, h_ref,
                *, bm, final_norm):
    f = pl.program_id(1)
    nf = pl.num_programs(1)

    def ffn(first):
        for r in range(bm // _MLP_RC):
            rows = slice(r * _MLP_RC, (r + 1) * _MLP_RC)
            a = jnp.dot(h_ref[rows, :], w1_ref[...], preferred_element_type=F32)
            a = jnp.maximum(a, 0.0)
            a = (a * a).astype(BF16)
            part = jnp.dot(a, w2_ref[...], preferred_element_type=F32)
            if first:
                o_ref[rows, :] = part
            else:
                o_ref[rows, :] += part

    @pl.when(f == 0)
    def _():
        for r in range(bm // _MLP_NORM_ROWS):
            rows = slice(r * _MLP_NORM_ROWS, (r + 1) * _MLP_NORM_ROWS)
            x = x1_ref[rows, :]
            ms = jnp.mean(x * x, axis=-1, keepdims=True)
            y = x * lax.rsqrt(ms + NORM_EPS) * nw_ref[...]
            h_ref[rows, :] = (y * (1.0 + sc_ref[0]) + sh_ref[0]).astype(BF16)
        ffn(True)

    @pl.when(f > 0)
    def _():
        ffn(False)

    @pl.when(f == nf - 1)
    def _():
        for r in range(bm // _MLP_FINAL_ROWS):
            rows = slice(r * _MLP_FINAL_ROWS, (r + 1) * _MLP_FINAL_ROWS)
            x2 = x1_ref[rows, :] + g_ref[0] * o_ref[rows, :]
            if final_norm:
                ms = jnp.mean(x2 * x2, axis=-1, keepdims=True)
                x2 = x2 * lax.rsqrt(ms + NORM_EPS) * fw_ref[...]
            o_ref[rows, :] = x2


def _mlp(x1, w1_bf16, w2_bf16, mod3, norm_w, final_w, *, seq, final_norm, bm=1024, bf=512):
    bt, d = x1.shape
    dff = w1_bf16.shape[1]
    tiles_per_seq = seq // bm
    mod_spec = lambda idx: pl.BlockSpec(
        (1, 1, d), lambda m, f: ((m // tiles_per_seq) * 6 + idx, 0, 0))
    return pl.pallas_call(
        functools.partial(_mlp_kernel, bm=bm, final_norm=final_norm),
        out_shape=jax.ShapeDtypeStruct((bt, d), F32),
        grid=(bt // bm, dff // bf),
        in_specs=[
            pl.BlockSpec((bm, d), lambda m, f: (m, 0)),
            mod_spec(3), mod_spec(4),
            pl.BlockSpec((1, d), lambda m, f: (0, 0)),
            pl.BlockSpec((d, bf), lambda m, f: (0, f)),
            pl.BlockSpec((bf, d), lambda m, f: (f, 0)),
            mod_spec(5),
            pl.BlockSpec((1, d), lambda m, f: (0, 0)),
        ],
        out_specs=pl.BlockSpec((bm, d), lambda m, f: (m, 0)),
        scratch_shapes=[pltpu.VMEM((bm, d), BF16)],
        compiler_params=pltpu.CompilerParams(
            dimension_semantics=("arbitrary", "arbitrary"), vmem_limit_bytes=V7X_VMEM_LIMIT_BYTES),
        name="mlp",
    )(x1, mod3, mod3, norm_w, w1_bf16, w2_bf16, mod3, final_w)


def kernel(x, c, w_ada, b_ada, norm1_w, norm2_w, w_in, ret_gn_w, diff_lq1, diff_lk1, diff_lq2,
           diff_lk2, diff_subln_w, w_out, w_mlp1, w_mlp2, final_norm_w):
    batch, seq, d = x.shape
    depth = w_ada.shape[0]
    c_pad = jnp.pad(c, ((0, 8 - batch), (0, 0)))
    rot = _rotary_lane_constants()
    x2d = x.reshape(batch * seq, d)
    final_w = final_norm_w.reshape(1, d)

    for l in range(depth):
        lam_init = 0.8 - 0.6 * math.exp(-0.3 * l)
        mod, tabs = _ada(c_pad, w_ada[l], b_ada[l].reshape(1, -1), rot, seq=seq)
        mod3 = mod[:batch].reshape(batch * 6, 1, d)
        proj = _proj(x2d, mod3, norm1_w[l].reshape(1, d), w_in[l], tabs, seq=seq)
        y_ret = _retention(proj, ret_gn_w[l].reshape(1, -1), batch=batch, seq=seq)
        y_dif, (w_out_b, w1_b, w2_b) = _diff_attention(
            proj, diff_lq1[l].reshape(1, -1), diff_lk1[l].reshape(1, -1),
            diff_lq2[l].reshape(1, -1), diff_lk2[l].reshape(1, -1),
            diff_subln_w[l].reshape(1, -1), (w_out[l], w_mlp1[l], w_mlp2[l]),
            batch=batch, seq=seq, lam_init=lam_init)
        x1 = _outproj(y_ret, y_dif, w_out_b, x2d, mod3, seq=seq)
        x2d = _mlp(x1, w1_b, w2_b, mod3, norm2_w[l].reshape(1, d), final_w, seq=seq,
                   final_norm=(l == depth - 1))
    if depth == 0:
        raise ValueError("depth must be positive")
    return x2d.reshape(batch, seq, d)
```

```python
import functools
import math

import jax
import jax.numpy as jnp
from jax import lax
from jax.experimental import pallas as pl
from jax.experimental.pallas import tpu as pltpu

F32 = jnp.float32
BF16 = jnp.bfloat16

HEAD_DIM = 128
N_RET_HEADS = 8
N_DIFF_HEADS = 8
DIFF_DQK = HEAD_DIM // 2
DIFF_ROT_DIM = DIFF_DQK // 4
RET_CHUNK = 128
ROPE_THETA = 500000.0
RET_THETA = 10000.0
NORM_EPS = 1e-6
GN_EPS = 1e-5

V7X_VMEM_LIMIT_BYTES = 58 * 1024 * 1024
F32_SUBLANES = 8


def _silu(v):
    return v * jax.nn.sigmoid(v)


def _ada_kernel(c_ref, w_ref, b_ref, rot_ref, o_ref, ret_c_ref, ret_s_ref, dif_c_ref,
                dif_a_ref, dif_b_ref, base_ref, *, tab_rows):
    s = _silu(c_ref[...]).astype(BF16)
    o_ref[...] = jnp.dot(s, w_ref[...].astype(BF16), preferred_element_type=F32) + b_ref[...]

    freq = (rot_ref[0:1, :], rot_ref[2:3, :])

    @pl.when(pl.program_id(0) == 0)
    def _():
        r = lax.broadcasted_iota(jnp.int32, (tab_rows, HEAD_DIM), 0).astype(F32)
        for k, w in enumerate(freq):
            base_ref[2 * k] = jnp.cos(r * w)
            base_ref[2 * k + 1] = jnp.sin(r * w)

    p0 = jnp.full((F32_SUBLANES, HEAD_DIM), pl.program_id(0) * tab_rows, jnp.int32).astype(F32)
    cos_sin = []
    for k, w in enumerate(freq):
        off_c = jnp.cos(p0 * w)[0:1, :]
        off_s = jnp.sin(p0 * w)[0:1, :]
        base_c, base_s = base_ref[2 * k], base_ref[2 * k + 1]
        cos_sin.append((off_c * base_c - off_s * base_s, off_s * base_c + off_c * base_s))
    (cos_ret, sin_ret), (cos_dif, sin_dif) = cos_sin
    ret_c_ref[...] = cos_ret
    ret_s_ref[...] = sin_ret * rot_ref[1:2, :]
    dif_c_ref[...] = cos_dif
    dif_a_ref[...] = sin_dif * rot_ref[3:4, :]
    dif_b_ref[...] = sin_dif * rot_ref[4:5, :]


def _ada(c_pad, w_ada, b_ada, rot, *, seq, n, bn=256):
    rows, d = c_pad.shape
    steps = n // bn
    tab_rows = seq // steps
    assert steps * bn == n and steps * tab_rows == seq and tab_rows % F32_SUBLANES == 0
    tab_spec = pl.BlockSpec((tab_rows, HEAD_DIM), lambda j: (j, 0))
    outs = pl.pallas_call(
        functools.partial(_ada_kernel, tab_rows=tab_rows),
        out_shape=[jax.ShapeDtypeStruct((rows, n), F32)]
        + [jax.ShapeDtypeStruct((seq, HEAD_DIM), F32)] * 5,
        grid=(steps,),
        in_specs=[
            pl.BlockSpec((rows, d), lambda j: (0, 0)),
            pl.BlockSpec((d, bn), lambda j: (0, j)),
            pl.BlockSpec((1, bn), lambda j: (0, j)),
            pl.BlockSpec(rot.shape, lambda j: (0, 0)),
        ],
        out_specs=[pl.BlockSpec((rows, bn), lambda j: (0, j))] + [tab_spec] * 5,
        scratch_shapes=[pltpu.VMEM((4, tab_rows, HEAD_DIM), F32)],
        compiler_params=pltpu.CompilerParams(
            dimension_semantics=("arbitrary",), vmem_limit_bytes=V7X_VMEM_LIMIT_BYTES),
        name="ada",
    )(c_pad, w_ada, b_ada, rot)
    return outs[0], tuple(outs[1:])


_PROJ_RC = 512
_PROJ_CC = 256
_PROJ_NORM_ROWS = 256


def _proj_kernel(x_ref, sh_ref, sc_ref, nw_ref, w_ref, rc_ref, rs_ref, dc_ref, da_ref, db_ref,
                 o_ref, h_ref, wb_ref, *, bm, bn):
    j = pl.program_id(1)
    n_rc = bm // _PROJ_RC

    def norm_modulate():
        for r in range(bm // _PROJ_NORM_ROWS):
            rows = slice(r * _PROJ_NORM_ROWS, (r + 1) * _PROJ_NORM_ROWS)
            x = x_ref[rows, :]
            ms = jnp.mean(x * x, axis=-1, keepdims=True)
            y = x * lax.rsqrt(ms + NORM_EPS) * nw_ref[...]
            h_ref[rows, :] = (y * (1.0 + sc_ref[0]) + sh_ref[0]).astype(BF16)

    def run(epilogue):
        for cc in range(bn // _PROJ_CC):
            cols = slice(cc * _PROJ_CC, (cc + 1) * _PROJ_CC)
            wb_ref[:, cols] = w_ref[:, cols].astype(BF16)
            for r in range(n_rc):
                rows = slice(r * _PROJ_RC, (r + 1) * _PROJ_RC)
                acc = jnp.dot(h_ref[rows, :], wb_ref[:, cols], preferred_element_type=F32)
                for hb in range(_PROJ_CC // HEAD_DIM):
                    a = acc[:, hb * HEAD_DIM:(hb + 1) * HEAD_DIM]
                    c0 = cc * _PROJ_CC + hb * HEAD_DIM
                    o_ref[rows, c0:c0 + HEAD_DIM] = epilogue(a, rows).astype(o_ref.dtype)

    def ret_rotary(a, rows):
        return a * rc_ref[rows, :] + pltpu.roll(a, HEAD_DIM // 2, 1) * rs_ref[rows, :]

    @pl.when(j == 0)
    def _():
        norm_modulate()
        run(ret_rotary)

    @pl.when(j == 1)
    def _():
        run(lambda a, rows: ret_rotary(a, rows) * (HEAD_DIM ** -0.5))

    @pl.when(jnp.logical_or(j == 4, j == 5))
    def _():
        scale = jnp.where(j == 4, DIFF_DQK ** -0.5 * math.log2(math.e), 1.0).astype(F32)
        half = DIFF_ROT_DIM // 2

        def ep(a, rows):
            rot = (a * dc_ref[rows, :]
                   + pltpu.roll(a, HEAD_DIM - half, 1) * da_ref[rows, :]
                   + pltpu.roll(a, half, 1) * db_ref[rows, :])
            return rot * scale
        run(ep)

    @pl.when(jnp.logical_or(jnp.logical_or(j == 2, j == 3), j == 6))
    def _():
        run(lambda a, rows: a)


def _proj(x2d, mod3, norm_w, w_in, tabs, *, seq, bm=1024, bn=1024):
    bt, d = x2d.shape
    n = w_in.shape[1]
    tiles_per_seq = seq // bm
    tab_spec = pl.BlockSpec((bm, HEAD_DIM), lambda m, j: (m % tiles_per_seq, 0))
    return pl.pallas_call(
        functools.partial(_proj_kernel, bm=bm, bn=bn),
        out_shape=jax.ShapeDtypeStruct((bt, n), BF16),
        grid=(bt // bm, n // bn),
        in_specs=[
            pl.BlockSpec((bm, d), lambda m, j: (m, 0)),
            pl.BlockSpec((1, 1, d), lambda m, j: ((m // tiles_per_seq) * 2 + 0, 0, 0)),
            pl.BlockSpec((1, 1, d), lambda m, j: ((m // tiles_per_seq) * 2 + 1, 0, 0)),
            pl.BlockSpec((1, d), lambda m, j: (0, 0)),
            pl.BlockSpec((d, bn), lambda m, j: (0, j)),
            tab_spec, tab_spec, tab_spec, tab_spec, tab_spec,
        ],
        out_specs=pl.BlockSpec((bm, bn), lambda m, j: (m, j)),
        scratch_shapes=[pltpu.VMEM((bm, d), BF16), pltpu.VMEM((d, bn), BF16)],
        compiler_params=pltpu.CompilerParams(
            dimension_semantics=("arbitrary", "arbitrary"), vmem_limit_bytes=V7X_VMEM_LIMIT_BYTES),
        name="proj",
    )(x2d, mod3, mod3, norm_w, w_in, *tabs)


def _rotary_lane_constants():
    lane = jnp.arange(HEAD_DIM)
    half = HEAD_DIM // 2
    inv_ret = RET_THETA ** (-(2 * (lane % half)).astype(F32) / HEAD_DIM)
    sign_ret = jnp.where(lane < half, -1.0, 1.0)
    half = DIFF_ROT_DIM // 2
    within = lane % DIFF_DQK
    inv_dif = ROPE_THETA ** (-(2 * (within % half)).astype(F32) / DIFF_ROT_DIM)
    inv_dif = jnp.where(within < DIFF_ROT_DIM, inv_dif, 0.0)
    sel_a = jnp.where(within < half, -1.0, 0.0)
    sel_b = jnp.where((within >= half) & (within < DIFF_ROT_DIM), 1.0, 0.0)
    rows = [inv_ret, sign_ret, inv_dif, sel_a, sel_b]
    rows += [jnp.zeros((HEAD_DIM,), F32)] * (F32_SUBLANES - len(rows))
    return jnp.stack(rows).astype(F32)


def _ret_kernel(q_ref, k_ref, v_ref, g_ref, gw_ref, o_ref, s_ref, *, seq):
    C = RET_CHUNK
    nc = seq // C
    head = (pl.program_id(0) % N_RET_HEADS).astype(F32)
    ii = lax.broadcasted_iota(jnp.int32, (C, C), 0).astype(F32)
    jj = lax.broadcasted_iota(jnp.int32, (C, C), 1).astype(F32)
    lg = jnp.log(1.0 - jnp.exp2(jnp.zeros((C, C), F32) - 5.0 - head))
    rel = ii - jj
    decay = jnp.where(rel >= 0, jnp.exp(jnp.maximum(rel, 0.0) * lg), 0.0)
    xi = jnp.exp((ii + 1.0) * lg)
    zeta = jnp.exp((C - 1.0 - ii) * lg)
    chunk_decay = jnp.exp(C * lg)

    def scan_body(c, state):
        rows = pl.ds(pl.multiple_of(c * C, C), C)
        s_ref[c] = state.astype(BF16)
        kz = (k_ref[rows, :].astype(F32) * zeta).astype(BF16)
        kv = lax.dot_general(kz, v_ref[rows, :], (((0,), (0,)), ((), ())),
                             preferred_element_type=F32)
        return state * chunk_decay + kv
    lax.fori_loop(0, nc, scan_body, jnp.zeros((C, C), F32), unroll=8)

    gw = gw_ref[...]

    def out_body(c, carry):
        rows = pl.ds(pl.multiple_of(c * C, C), C)
        q = q_ref[rows, :]
        inner = lax.dot_general(q, k_ref[rows, :], (((1,), (1,)), ((), ())),
                                preferred_element_type=F32) * decay
        o = jnp.dot(inner.astype(BF16), v_ref[rows, :], preferred_element_type=F32)
        o = o + jnp.dot(q, s_ref[c], preferred_element_type=F32) * xi
        mu = jnp.mean(o, axis=-1, keepdims=True)
        d = o - mu
        var = jnp.mean(d * d, axis=-1, keepdims=True)
        y = d * lax.rsqrt(var + GN_EPS) * gw
        o_ref[rows, :] = (y * _silu(g_ref[rows, :].astype(F32))).astype(o_ref.dtype)
        return carry
    lax.fori_loop(0, nc, out_body, 0, unroll=16)


def _retention(proj, gn_w, *, batch, seq):
    nh = N_RET_HEADS
    blk = lambda off: pl.BlockSpec((seq, HEAD_DIM), lambda s: (s // nh, off + s % nh))
    return pl.pallas_call(
        functools.partial(_ret_kernel, seq=seq),
        out_shape=jax.ShapeDtypeStruct((batch * seq, nh * HEAD_DIM), BF16),
        grid=(batch * nh,),
        in_specs=[blk(0), blk(nh), blk(2 * nh), blk(3 * nh),
                  pl.BlockSpec((1, HEAD_DIM), lambda s: (0, s % nh))],
        out_specs=pl.BlockSpec((seq, HEAD_DIM), lambda s: (s // nh, s % nh)),
        scratch_shapes=[pltpu.VMEM((seq // RET_CHUNK, RET_CHUNK, HEAD_DIM), BF16)],
        compiler_params=pltpu.CompilerParams(
            dimension_semantics=("arbitrary",), vmem_limit_bytes=V7X_VMEM_LIMIT_BYTES),
        name="retention",
    )(proj, proj, proj, proj, gn_w)


_ATTN_STRIP = 256


_ATTN_N_IN = 11


def _attn_kernel(*refs, blk, lam_init, n_cast):
    (q_ref, k_ref, v_ref, lq1_ref, lk1_ref, lq2_ref, lk2_ref, sw_ref,
     c_ref, wada_ref, bada_ref) = refs[:_ATTN_N_IN]
    cast_src = refs[_ATTN_N_IN:_ATTN_N_IN + n_cast]
    o_ref, mod_ref = refs[_ATTN_N_IN + n_cast:_ATTN_N_IN + n_cast + 2]
    cast_dst = refs[_ATTN_N_IN + n_cast + 2:_ATTN_N_IN + 2 * n_cast + 2]
    vt_ref, qt_ref, m_ref, l_ref, acc_ref, sa_ref, sb_ref = refs[_ATTN_N_IN + 2 * n_cast + 2:]

    def side_work(part, n_parts):
        if part == 0:
            s = _silu(c_ref[...]).astype(BF16)
            mod_ref[...] = jnp.dot(s, wada_ref[...].astype(BF16),
                                   preferred_element_type=F32) + bada_ref[...]
        for src, dst in zip(cast_src, cast_dst):
            rows = src.shape[0] // n_parts
            dst[part * rows:(part + 1) * rows, :] = src[part * rows:(part + 1) * rows, :].astype(
                dst.dtype)

    _attn_body(q_ref, k_ref, v_ref, lq1_ref, lk1_ref, lq2_ref, lk2_ref, sw_ref, o_ref,
               vt_ref, qt_ref, m_ref, l_ref, acc_ref, sa_ref, sb_ref, side_work,
               blk=blk, lam_init=lam_init)


def _attn_body(q_ref, k_ref, v_ref, lq1_ref, lk1_ref, lq2_ref, lk2_ref, sw_ref, o_ref,
               vt_ref, qt_ref, m_ref, l_ref, acc_ref, sa_ref, sb_ref, side_work, *, blk, lam_init):
    nblk = vt_ref.shape[0]
    nstrip = blk // _ATTN_STRIP
    assert nblk % 2 == 0

    for jb in range(nblk):
        vt_ref[jb] = v_ref[jb * blk:(jb + 1) * blk, :].T

    feat = lax.broadcasted_iota(jnp.int32, (HEAD_DIM, _ATTN_STRIP), 0)
    for idx in range(nblk * nstrip):
        qt = q_ref[idx * _ATTN_STRIP:(idx + 1) * _ATTN_STRIP, :].T
        zero = jnp.zeros_like(qt)
        qt_ref[0, idx] = jnp.where(feat < DIFF_DQK, qt, zero)
        qt_ref[1, idx] = jnp.where(feat >= DIFF_DQK, qt, zero)

    chains = [(comp, s) for comp in range(2) for s in range(nstrip)]

    def keys_needed(s, diagonal):
        return (s + 1) * _ATTN_STRIP if diagonal else blk

    def scores(item, s_ref, diagonal=False):
        i, j = item
        jrow = j * blk
        if not isinstance(jrow, int):
            jrow = pl.multiple_of(jrow, blk)
        for c, (comp, s) in enumerate(chains):
            nk = keys_needed(s, diagonal)
            s_ref[c, :nk, :] = jnp.dot(k_ref[pl.ds(jrow, nk), :], qt_ref[comp, i * nstrip + s],
                                       preferred_element_type=F32)

    def softmax_av(item, s_ref, diagonal):
        i, j = item
        vtb = vt_ref[j]
        for c, (comp, s) in enumerate(chains):
            nk = keys_needed(s, diagonal)
            st = s_ref[c, :nk, :]
            idx = i * nstrip + s
            if diagonal:
                kpos = lax.broadcasted_iota(jnp.int32, st.shape, 0)
                qpos = lax.broadcasted_iota(jnp.int32, st.shape, 1) + s * _ATTN_STRIP
                st = jnp.where(kpos <= qpos, st, -jnp.inf)
                m_new = jnp.max(st, axis=0, keepdims=True)
                p = jnp.exp2(st - m_new)
                l_ref[comp, idx] = jnp.sum(p, axis=0, keepdims=True)
                acc_ref[comp, idx] = jnp.dot(vtb[:, :nk], p.astype(BF16),
                                             preferred_element_type=F32)
            else:
                m_old = m_ref[comp, idx]
                m_new = jnp.maximum(m_old, jnp.max(st, axis=0, keepdims=True))
                alpha = jnp.exp2(m_old - m_new)
                p = jnp.exp2(st - m_new)
                l_ref[comp, idx] = alpha * l_ref[comp, idx] + jnp.sum(p, axis=0, keepdims=True)
                acc_ref[comp, idx] = alpha * acc_ref[comp, idx] + jnp.dot(
                    vtb, p.astype(BF16), preferred_element_type=F32)
            m_ref[comp, idx] = m_new

    def next_item(item):
        i, j = item
        wrap = j + 1 >= i
        return jnp.where(wrap, i + 1, i), jnp.where(wrap, 0, j + 1)

    n_parts = nblk // 2
    scores((0, 0), sa_ref, True)
    for t in range(n_parts):
        side_work(t, n_parts)
        scores((2 * t + 1, 2 * t + 1), sb_ref, True)
        softmax_av((2 * t, 2 * t), sa_ref, True)
        if 2 * t + 2 < nblk:
            scores((2 * t + 2, 2 * t + 2), sa_ref, True)
        else:
            scores((1, 0), sa_ref)
        softmax_av((2 * t + 1, 2 * t + 1), sb_ref, True)

    n_off = nblk * (nblk - 1) // 2
    assert n_off % 2 == 0

    def pair(t, item):
        item_b = next_item(item)
        item_c = next_item(item_b)
        scores(item_b, sb_ref)
        softmax_av(item, sa_ref, False)
        scores(item_c, sa_ref)
        softmax_av(item_b, sb_ref, False)
        return item_c
    item = lax.fori_loop(0, n_off // 2 - 1, pair, (jnp.int32(1), jnp.int32(0)))
    item_b = next_item(item)
    scores(item_b, sb_ref)
    softmax_av(item, sa_ref, False)
    softmax_av(item_b, sb_ref, False)

    lam = (jnp.exp(jnp.sum(lq1_ref[...] * lk1_ref[...], axis=-1, keepdims=True))
           - jnp.exp(jnp.sum(lq2_ref[...] * lk2_ref[...], axis=-1, keepdims=True))
           + lam_init)
    sw = sw_ref[...] * (1.0 - lam_init)

    def finish(idx, carry):
        yt = acc_ref[0, idx] / l_ref[0, idx] - lam * (acc_ref[1, idx] / l_ref[1, idx])
        yt = yt * lax.rsqrt(jnp.mean(yt * yt, axis=0, keepdims=True) + GN_EPS)
        rows = pl.ds(pl.multiple_of(idx * _ATTN_STRIP, _ATTN_STRIP), _ATTN_STRIP)
        o_ref[rows, :] = (yt.T * sw).astype(o_ref.dtype)
        return carry
    lax.fori_loop(0, nblk * nstrip, finish, 0)


def _diff_attention(proj, lq1, lk1, lq2, lk2, subln_w, c_pad, w_ada, b_ada, ada_col0,
                    cast_weights, *, batch, seq, lam_init, blk=512):
    nh = N_DIFF_HEADS
    nblk = seq // blk
    nchain = 2 * blk // _ATTN_STRIP
    nstrips = seq // _ATTN_STRIP
    steps = batch * nh
    q_off, k_off, v_off = 4 * nh, 5 * nh, 6 * nh
    head_blk = lambda off: pl.BlockSpec((seq, HEAD_DIM), lambda s: (s // nh, off + s % nh))
    lam_spec = pl.BlockSpec((1, DIFF_DQK), lambda s: (0, 0))
    rows, d = c_pad.shape
    ada_cols = w_ada.shape[1] - ada_col0
    ada_bn = ada_cols // steps
    assert ada_bn * steps == ada_cols and ada_col0 % ada_bn == 0
    ada_blk0 = ada_col0 // ada_bn
    cast_specs = [pl.BlockSpec((w.shape[0] // steps, w.shape[1]), lambda s: (s, 0))
                  for w in cast_weights]
    outs = pl.pallas_call(
        functools.partial(_attn_kernel, blk=blk, lam_init=lam_init, n_cast=len(cast_weights)),
        out_shape=[jax.ShapeDtypeStruct((batch * seq, nh * HEAD_DIM), BF16),
                   jax.ShapeDtypeStruct((rows, ada_cols), F32)]
        + [jax.ShapeDtypeStruct(w.shape, BF16) for w in cast_weights],
        grid=(steps,),
        in_specs=[
            head_blk(q_off), head_blk(k_off), head_blk(v_off),
            lam_spec, lam_spec, lam_spec, lam_spec,
            pl.BlockSpec((1, HEAD_DIM), lambda s: (0, s % nh)),
            pl.BlockSpec((rows, d), lambda s: (0, 0)),
            pl.BlockSpec((d, ada_bn), lambda s: (0, ada_blk0 + s)),
            pl.BlockSpec((1, ada_bn), lambda s: (0, ada_blk0 + s)),
        ] + cast_specs,
        out_specs=[pl.BlockSpec((seq, HEAD_DIM), lambda s: (s // nh, s % nh)),
                   pl.BlockSpec((rows, ada_bn), lambda s: (0, s))] + cast_specs,
        scratch_shapes=[
            pltpu.VMEM((nblk, HEAD_DIM, blk), BF16),
            pltpu.VMEM((2, nstrips, HEAD_DIM, _ATTN_STRIP), BF16),
            pltpu.VMEM((2, nstrips, 1, _ATTN_STRIP), F32),
            pltpu.VMEM((2, nstrips, 1, _ATTN_STRIP), F32),
            pltpu.VMEM((2, nstrips, HEAD_DIM, _ATTN_STRIP), F32),
            pltpu.VMEM((nchain, blk, _ATTN_STRIP), F32),
            pltpu.VMEM((nchain, blk, _ATTN_STRIP), F32),
        ],
        compiler_params=pltpu.CompilerParams(
            dimension_semantics=("arbitrary",), vmem_limit_bytes=V7X_VMEM_LIMIT_BYTES),
        name="diff_attn",
    )(proj, proj, proj, lq1, lk1, lq2, lk2, subln_w, c_pad, w_ada, b_ada, *cast_weights)
    return outs[0], outs[1], outs[2:]


_OUT_RC = 512


def _outproj_kernel(yr_ref, yd_ref, w_ref, x_ref, g_ref, x1_ref, *, bm, wr):
    for r in range(bm // _OUT_RC):
        rows = slice(r * _OUT_RC, (r + 1) * _OUT_RC)
        mixed = (jnp.dot(yr_ref[rows, :], w_ref[:wr, :], preferred_element_type=F32)
                 + jnp.dot(yd_ref[rows, :], w_ref[wr:, :], preferred_element_type=F32))
        x1_ref[rows, :] = x_ref[rows, :] + g_ref[0] * mixed


def _outproj(y_ret, y_dif, w_out_bf16, x2d, mod3, *, seq, bm=512, bn=2048):
    bt, d = x2d.shape
    wr = y_ret.shape[1]
    wd = y_dif.shape[1]
    tiles_per_seq = seq // bm
    return pl.pallas_call(
        functools.partial(_outproj_kernel, bm=bm, wr=wr),
        out_shape=jax.ShapeDtypeStruct((bt, d), F32),
        grid=(bt // bm, d // bn),
        in_specs=[
            pl.BlockSpec((bm, wr), lambda m, n: (m, 0)),
            pl.BlockSpec((bm, wd), lambda m, n: (m, 0)),
            pl.BlockSpec((wr + wd, bn), lambda m, n: (0, n)),
            pl.BlockSpec((bm, bn), lambda m, n: (m, n)),
            pl.BlockSpec((1, 1, bn), lambda m, n: ((m // tiles_per_seq) * 6 + 2, 0, n)),
        ],
        out_specs=pl.BlockSpec((bm, bn), lambda m, n: (m, n)),
        compiler_params=pltpu.CompilerParams(
            dimension_semantics=("arbitrary", "arbitrary"), vmem_limit_bytes=V7X_VMEM_LIMIT_BYTES),
        name="outproj",
    )(y_ret, y_dif, w_out_bf16, x2d, mod3)


_MLP_RC = 512
_MLP_NORM_ROWS = 256
_MLP_FINAL_ROWS = 16


def _mlp_kernel(x1_ref, sh_ref, sc_ref, nw_ref, w1_ref, w2_ref, g_ref, fw_ref, o_ref, h_ref,
                *, bm, final_norm):
    f = pl.program_id(1)
    nf = pl.num_programs(1)

    def ffn(first):
        for r in range(bm // _MLP_RC):
            rows = slice(r * _MLP_RC, (r + 1) * _MLP_RC)
            a = jnp.dot(h_ref[rows, :], w1_ref[...], preferred_element_type=F32)
            a = jnp.maximum(a, 0.0)
            a = (a * a).astype(BF16)
            part = jnp.dot(a, w2_ref[...], preferred_element_type=F32)
            if first:
                o_ref[rows, :] = part
            else:
                o_ref[rows, :] += part

    @pl.when(f == 0)
    def _():
        for r in range(bm // _MLP_NORM_ROWS):
            rows = slice(r * _MLP_NORM_ROWS, (r + 1) * _MLP_NORM_ROWS)
            x = x1_ref[rows, :]
            ms = jnp.mean(x * x, axis=-1, keepdims=True)
            y = x * lax.rsqrt(ms + NORM_EPS) * nw_ref[...]
            h_ref[rows, :] = (y * (1.0 + sc_ref[0]) + sh_ref[0]).astype(BF16)
        ffn(True)

    @pl.when(f > 0)
    def _():
        ffn(False)

    @pl.when(f == nf - 1)
    def _():
        for r in range(bm // _MLP_FINAL_ROWS):
            rows = slice(r * _MLP_FINAL_ROWS, (r + 1) * _MLP_FINAL_ROWS)
            x2 = x1_ref[rows, :] + g_ref[0] * o_ref[rows, :]
            if final_norm:
                ms = jnp.mean(x2 * x2, axis=-1, keepdims=True)
                x2 = x2 * lax.rsqrt(ms + NORM_EPS) * fw_ref[...]
            o_ref[rows, :] = x2


def _mlp(x1, w1_bf16, w2_bf16, mod3, norm_w, final_w, *, seq, final_norm, bm=1024, bf=512):
    bt, d = x1.shape
    dff = w1_bf16.shape[1]
    tiles_per_seq = seq // bm
    mod_spec = lambda idx: pl.BlockSpec(
        (1, 1, d), lambda m, f: ((m // tiles_per_seq) * 6 + idx, 0, 0))
    return pl.pallas_call(
        functools.partial(_mlp_kernel, bm=bm, final_norm=final_norm),
        out_shape=jax.ShapeDtypeStruct((bt, d), F32),
        grid=(bt // bm, dff // bf),
        in_specs=[
            pl.BlockSpec((bm, d), lambda m, f: (m, 0)),
            mod_spec(3), mod_spec(4),
            pl.BlockSpec((1, d), lambda m, f: (0, 0)),
            pl.BlockSpec((d, bf), lambda m, f: (0, f)),
            pl.BlockSpec((bf, d), lambda m, f: (f, 0)),
            mod_spec(5),
            pl.BlockSpec((1, d), lambda m, f: (0, 0)),
        ],
        out_specs=pl.BlockSpec((bm, d), lambda m, f: (m, 0)),
        scratch_shapes=[pltpu.VMEM((bm, d), BF16)],
        compiler_params=pltpu.CompilerParams(
            dimension_semantics=("arbitrary", "arbitrary"), vmem_limit_bytes=V7X_VMEM_LIMIT_BYTES),
        name="mlp",
    )(x1, mod3, mod3, norm_w, w1_bf16, w2_bf16, mod3, final_w)


def kernel(x, c, w_ada, b_ada, norm1_w, norm2_w, w_in, ret_gn_w, diff_lq1, diff_lk1, diff_lq2,
           diff_lk2, diff_subln_w, w_out, w_mlp1, w_mlp2, final_norm_w):
    batch, seq, d = x.shape
    depth = w_ada.shape[0]
    c_pad = jnp.pad(c, ((0, F32_SUBLANES - batch), (0, 0)))
    rot = _rotary_lane_constants()
    x2d = x.reshape(batch * seq, d)
    final_w = final_norm_w.reshape(1, d)

    for l in range(depth):
        lam_init = 0.8 - 0.6 * math.exp(-0.3 * l)
        b_ada_l = b_ada[l].reshape(1, -1)
        mod_a, tabs = _ada(c_pad, w_ada[l], b_ada_l, rot, seq=seq, n=2 * d)
        mod3_a = mod_a[:batch].reshape(batch * 2, 1, d)
        proj = _proj(x2d, mod3_a, norm1_w[l].reshape(1, d), w_in[l], tabs, seq=seq)
        y_ret = _retention(proj, ret_gn_w[l].reshape(1, -1), batch=batch, seq=seq)
        y_dif, mod_b, (w_out_b, w1_b, w2_b) = _diff_attention(
            proj, diff_lq1[l].reshape(1, -1), diff_lk1[l].reshape(1, -1),
            diff_lq2[l].reshape(1, -1), diff_lk2[l].reshape(1, -1),
            diff_subln_w[l].reshape(1, -1), c_pad, w_ada[l], b_ada_l, 2 * d,
            (w_out[l], w_mlp1[l], w_mlp2[l]), batch=batch, seq=seq, lam_init=lam_init)
        mod3 = jnp.concatenate([mod_a[:batch], mod_b[:batch]], axis=1).reshape(batch * 6, 1, d)
        x1 = _outproj(y_ret, y_dif, w_out_b, x2d, mod3, seq=seq)
        x2d = _mlp(x1, w1_b, w2_b, mod3, norm2_w[l].reshape(1, d), final_w, seq=seq,
                   final_norm=(l == depth - 1))
    if depth == 0:
        raise ValueError("depth must be positive")
    return x2d.reshape(batch, seq, d)
```

```python
import functools
import math

import jax
import jax.numpy as jnp
from jax import lax
from jax.experimental import pallas as pl
from jax.experimental.pallas import tpu as pltpu

F32 = jnp.float32
BF16 = jnp.bfloat16

HEAD_DIM = 128
N_RET_HEADS = 8
N_DIFF_HEADS = 8
DIFF_DQK = HEAD_DIM // 2
DIFF_ROT_DIM = DIFF_DQK // 4
RET_CHUNK = 128
ROPE_THETA = 500000.0
RET_THETA = 10000.0
NORM_EPS = 1e-6
GN_EPS = 1e-5

V7X_VMEM_LIMIT_BYTES = 58 * 1024 * 1024
F32_SUBLANES = 8


def _silu(v):
    return v * jax.nn.sigmoid(v)


def _ada_kernel(c_ref, w_ref, b_ref, rot_ref, o_ref, ret_c_ref, ret_s_ref, dif_c_ref,
                dif_a_ref, dif_b_ref, base_ref, *, tab_rows):
    s = _silu(c_ref[...]).astype(BF16)
    o_ref[...] = jnp.dot(s, w_ref[...].astype(BF16), preferred_element_type=F32) + b_ref[...]

    freq = (rot_ref[0:1, :], rot_ref[2:3, :])

    @pl.when(pl.program_id(0) == 0)
    def _():
        r = lax.broadcasted_iota(jnp.int32, (tab_rows, HEAD_DIM), 0).astype(F32)
        for k, w in enumerate(freq):
            base_ref[2 * k] = jnp.cos(r * w)
            base_ref[2 * k + 1] = jnp.sin(r * w)

    p0 = jnp.full((F32_SUBLANES, HEAD_DIM), pl.program_id(0) * tab_rows, jnp.int32).astype(F32)
    cos_sin = []
    for k, w in enumerate(freq):
        off_c = jnp.cos(p0 * w)[0:1, :]
        off_s = jnp.sin(p0 * w)[0:1, :]
        base_c, base_s = base_ref[2 * k], base_ref[2 * k + 1]
        cos_sin.append((off_c * base_c - off_s * base_s, off_s * base_c + off_c * base_s))
    (cos_ret, sin_ret), (cos_dif, sin_dif) = cos_sin
    ret_c_ref[...] = cos_ret
    ret_s_ref[...] = sin_ret * rot_ref[1:2, :]
    dif_c_ref[...] = cos_dif
    dif_a_ref[...] = sin_dif * rot_ref[3:4, :]
    dif_b_ref[...] = sin_dif * rot_ref[4:5, :]


def _ada(c_pad, w_ada, b_ada, rot, *, seq, bn=768):
    rows, d = c_pad.shape
    n = w_ada.shape[1]
    steps = n // bn
    tab_rows = seq // steps
    assert steps * bn == n and steps * tab_rows == seq and tab_rows % F32_SUBLANES == 0
    tab_spec = pl.BlockSpec((tab_rows, HEAD_DIM), lambda j: (j, 0))
    outs = pl.pallas_call(
        functools.partial(_ada_kernel, tab_rows=tab_rows),
        out_shape=[jax.ShapeDtypeStruct((rows, n), F32)]
        + [jax.ShapeDtypeStruct((seq, HEAD_DIM), F32)] * 5,
        grid=(steps,),
        in_specs=[
            pl.BlockSpec((rows, d), lambda j: (0, 0)),
            pl.BlockSpec((d, bn), lambda j: (0, j)),
            pl.BlockSpec((1, bn), lambda j: (0, j)),
            pl.BlockSpec(rot.shape, lambda j: (0, 0)),
        ],
        out_specs=[pl.BlockSpec((rows, bn), lambda j: (0, j))] + [tab_spec] * 5,
        scratch_shapes=[pltpu.VMEM((4, tab_rows, HEAD_DIM), F32)],
        compiler_params=pltpu.CompilerParams(
            dimension_semantics=("arbitrary",), vmem_limit_bytes=V7X_VMEM_LIMIT_BYTES),
        name="ada",
    )(c_pad, w_ada, b_ada, rot)
    return outs[0], tuple(outs[1:])


_PROJ_RC = 512
_PROJ_CC = 256
_PROJ_NORM_ROWS = 256


def _proj_kernel(x_ref, sh_ref, sc_ref, nw_ref, w_ref, rc_ref, rs_ref, dc_ref, da_ref, db_ref,
                 o_ref, h_ref, wb_ref, *, bm, bn):
    j = pl.program_id(1)
    n_rc = bm // _PROJ_RC

    def norm_modulate():
        for r in range(bm // _PROJ_NORM_ROWS):
            rows = slice(r * _PROJ_NORM_ROWS, (r + 1) * _PROJ_NORM_ROWS)
            x = x_ref[rows, :]
            ms = jnp.mean(x * x, axis=-1, keepdims=True)
            y = x * lax.rsqrt(ms + NORM_EPS) * nw_ref[...]
            h_ref[rows, :] = (y * (1.0 + sc_ref[0]) + sh_ref[0]).astype(BF16)

    def run(epilogue):
        for cc in range(bn // _PROJ_CC):
            cols = slice(cc * _PROJ_CC, (cc + 1) * _PROJ_CC)
            wb_ref[:, cols] = w_ref[:, cols].astype(BF16)
            for r in range(n_rc):
                rows = slice(r * _PROJ_RC, (r + 1) * _PROJ_RC)
                acc = jnp.dot(h_ref[rows, :], wb_ref[:, cols], preferred_element_type=F32)
                for hb in range(_PROJ_CC // HEAD_DIM):
                    a = acc[:, hb * HEAD_DIM:(hb + 1) * HEAD_DIM]
                    c0 = cc * _PROJ_CC + hb * HEAD_DIM
                    o_ref[rows, c0:c0 + HEAD_DIM] = epilogue(a, rows).astype(o_ref.dtype)

    def ret_rotary(a, rows):
        return a * rc_ref[rows, :] + pltpu.roll(a, HEAD_DIM // 2, 1) * rs_ref[rows, :]

    @pl.when(j == 0)
    def _():
        norm_modulate()
        run(ret_rotary)

    @pl.when(j == 1)
    def _():
        run(lambda a, rows: ret_rotary(a, rows) * (HEAD_DIM ** -0.5))

    @pl.when(jnp.logical_or(j == 4, j == 5))
    def _():
        scale = jnp.where(j == 4, DIFF_DQK ** -0.5 * math.log2(math.e), 1.0).astype(F32)
        half = DIFF_ROT_DIM // 2

        def ep(a, rows):
            rot = (a * dc_ref[rows, :]
                   + pltpu.roll(a, HEAD_DIM - half, 1) * da_ref[rows, :]
                   + pltpu.roll(a, half, 1) * db_ref[rows, :])
            return rot * scale
        run(ep)

    @pl.when(jnp.logical_or(jnp.logical_or(j == 2, j == 3), j == 6))
    def _():
        run(lambda a, rows: a)


def _proj(x2d, mod3, norm_w, w_in, tabs, *, seq, bm=1024, bn=1024):
    bt, d = x2d.shape
    n = w_in.shape[1]
    tiles_per_seq = seq // bm
    tab_spec = pl.BlockSpec((bm, HEAD_DIM), lambda m, j: (m % tiles_per_seq, 0))
    return pl.pallas_call(
        functools.partial(_proj_kernel, bm=bm, bn=bn),
        out_shape=jax.ShapeDtypeStruct((bt, n), BF16),
        grid=(bt // bm, n // bn),
        in_specs=[
            pl.BlockSpec((bm, d), lambda m, j: (m, 0)),
            pl.BlockSpec((1, 1, d), lambda m, j: ((m // tiles_per_seq) * 6 + 0, 0, 0)),
            pl.BlockSpec((1, 1, d), lambda m, j: ((m // tiles_per_seq) * 6 + 1, 0, 0)),
            pl.BlockSpec((1, d), lambda m, j: (0, 0)),
            pl.BlockSpec((d, bn), lambda m, j: (0, j)),
            tab_spec, tab_spec, tab_spec, tab_spec, tab_spec,
        ],
        out_specs=pl.BlockSpec((bm, bn), lambda m, j: (m, j)),
        scratch_shapes=[pltpu.VMEM((bm, d), BF16), pltpu.VMEM((d, bn), BF16)],
        compiler_params=pltpu.CompilerParams(
            dimension_semantics=("arbitrary", "arbitrary"), vmem_limit_bytes=V7X_VMEM_LIMIT_BYTES),
        name="proj",
    )(x2d, mod3, mod3, norm_w, w_in, *tabs)


def _rotary_lane_constants():
    lane = jnp.arange(HEAD_DIM)
    half = HEAD_DIM // 2
    inv_ret = RET_THETA ** (-(2 * (lane % half)).astype(F32) / HEAD_DIM)
    sign_ret = jnp.where(lane < half, -1.0, 1.0)
    half = DIFF_ROT_DIM // 2
    within = lane % DIFF_DQK
    inv_dif = ROPE_THETA ** (-(2 * (within % half)).astype(F32) / DIFF_ROT_DIM)
    inv_dif = jnp.where(within < DIFF_ROT_DIM, inv_dif, 0.0)
    sel_a = jnp.where(within < half, -1.0, 0.0)
    sel_b = jnp.where((within >= half) & (within < DIFF_ROT_DIM), 1.0, 0.0)
    rows = [inv_ret, sign_ret, inv_dif, sel_a, sel_b]
    rows += [jnp.zeros((HEAD_DIM,), F32)] * (F32_SUBLANES - len(rows))
    return jnp.stack(rows).astype(F32)


def _ret_kernel(q_ref, k_ref, v_ref, g_ref, gw_ref, o_ref, s_ref, *, seq):
    C = RET_CHUNK
    nc = seq // C
    head = (pl.program_id(0) % N_RET_HEADS).astype(F32)
    ii = lax.broadcasted_iota(jnp.int32, (C, C), 0).astype(F32)
    jj = lax.broadcasted_iota(jnp.int32, (C, C), 1).astype(F32)
    lg = jnp.log(1.0 - jnp.exp2(jnp.zeros((C, C), F32) - 5.0 - head))
    rel = ii - jj
    decay = jnp.where(rel >= 0, jnp.exp(jnp.maximum(rel, 0.0) * lg), 0.0)
    xi = jnp.exp((ii + 1.0) * lg)
    zeta = jnp.exp((C - 1.0 - ii) * lg)
    chunk_decay = jnp.exp(C * lg)

    def scan_body(c, state):
        rows = pl.ds(pl.multiple_of(c * C, C), C)
        s_ref[c] = state.astype(BF16)
        kz = (k_ref[rows, :].astype(F32) * zeta).astype(BF16)
        kv = lax.dot_general(kz, v_ref[rows, :], (((0,), (0,)), ((), ())),
                             preferred_element_type=F32)
        return state * chunk_decay + kv
    lax.fori_loop(0, nc, scan_body, jnp.zeros((C, C), F32), unroll=8)

    gw = gw_ref[...]

    def out_body(c, carry):
        rows = pl.ds(pl.multiple_of(c * C, C), C)
        q = q_ref[rows, :]
        inner = lax.dot_general(q, k_ref[rows, :], (((1,), (1,)), ((), ())),
                                preferred_element_type=F32) * decay
        o = jnp.dot(inner.astype(BF16), v_ref[rows, :], preferred_element_type=F32)
        o = o + jnp.dot(q, s_ref[c], preferred_element_type=F32) * xi
        mu = jnp.mean(o, axis=-1, keepdims=True)
        d = o - mu
        var = jnp.mean(d * d, axis=-1, keepdims=True)
        y = d * lax.rsqrt(var + GN_EPS) * gw
        o_ref[rows, :] = (y * _silu(g_ref[rows, :].astype(F32))).astype(o_ref.dtype)
        return carry
    lax.fori_loop(0, nc, out_body, 0, unroll=16)


def _retention(proj, gn_w, *, batch, seq):
    nh = N_RET_HEADS
    blk = lambda off: pl.BlockSpec((seq, HEAD_DIM), lambda s: (s // nh, off + s % nh))
    return pl.pallas_call(
        functools.partial(_ret_kernel, seq=seq),
        out_shape=jax.ShapeDtypeStruct((batch * seq, nh * HEAD_DIM), BF16),
        grid=(batch * nh,),
        in_specs=[blk(0), blk(nh), blk(2 * nh), blk(3 * nh),
                  pl.BlockSpec((1, HEAD_DIM), lambda s: (0, s % nh))],
        out_specs=pl.BlockSpec((seq, HEAD_DIM), lambda s: (s // nh, s % nh)),
        scratch_shapes=[pltpu.VMEM((seq // RET_CHUNK, RET_CHUNK, HEAD_DIM), BF16)],
        compiler_params=pltpu.CompilerParams(
            dimension_semantics=("arbitrary",), vmem_limit_bytes=V7X_VMEM_LIMIT_BYTES),
        name="retention",
    )(proj, proj, proj, proj, gn_w)


_ATTN_STRIP = 256


_ATTN_N_IN = 8


def _attn_kernel(*refs, blk, lam_init, n_cast):
    q_ref, k_ref, v_ref, lq1_ref, lk1_ref, lq2_ref, lk2_ref, sw_ref = refs[:_ATTN_N_IN]
    cast_src = refs[_ATTN_N_IN:_ATTN_N_IN + n_cast]
    o_ref = refs[_ATTN_N_IN + n_cast]
    cast_dst = refs[_ATTN_N_IN + n_cast + 1:_ATTN_N_IN + 2 * n_cast + 1]
    vt_ref, qt_ref, m_ref, l_ref, acc_ref, sa_ref, sb_ref = refs[_ATTN_N_IN + 2 * n_cast + 1:]

    def side_work(part, n_parts):
        for src, dst in zip(cast_src, cast_dst):
            rows = src.shape[0] // n_parts
            dst[part * rows:(part + 1) * rows, :] = src[part * rows:(part + 1) * rows, :].astype(
                dst.dtype)

    _attn_body(q_ref, k_ref, v_ref, lq1_ref, lk1_ref, lq2_ref, lk2_ref, sw_ref, o_ref,
               vt_ref, qt_ref, m_ref, l_ref, acc_ref, sa_ref, sb_ref, side_work,
               blk=blk, lam_init=lam_init)


def _attn_body(q_ref, k_ref, v_ref, lq1_ref, lk1_ref, lq2_ref, lk2_ref, sw_ref, o_ref,
               vt_ref, qt_ref, m_ref, l_ref, acc_ref, sa_ref, sb_ref, side_work, *, blk, lam_init):
    nblk = vt_ref.shape[0]
    nstrip = blk // _ATTN_STRIP
    assert nblk % 2 == 0

    for jb in range(nblk):
        vt_ref[jb] = v_ref[jb * blk:(jb + 1) * blk, :].T

    feat = lax.broadcasted_iota(jnp.int32, (HEAD_DIM, _ATTN_STRIP), 0)
    for idx in range(nblk * nstrip):
        qt = q_ref[idx * _ATTN_STRIP:(idx + 1) * _ATTN_STRIP, :].T
        zero = jnp.zeros_like(qt)
        qt_ref[0, idx] = jnp.where(feat < DIFF_DQK, qt, zero)
        qt_ref[1, idx] = jnp.where(feat >= DIFF_DQK, qt, zero)

    chains = [(comp, s) for comp in range(2) for s in range(nstrip)]

    def keys_needed(s, diagonal):
        return (s + 1) * _ATTN_STRIP if diagonal else blk

    def scores(item, s_ref, diagonal=False):
        i, j = item
        jrow = j * blk
        if not isinstance(jrow, int):
            jrow = pl.multiple_of(jrow, blk)
        for c, (comp, s) in enumerate(chains):
            nk = keys_needed(s, diagonal)
            s_ref[c, :nk, :] = jnp.dot(k_ref[pl.ds(jrow, nk), :], qt_ref[comp, i * nstrip + s],
                                       preferred_element_type=F32)

    def softmax_av(item, s_ref, diagonal):
        i, j = item
        vtb = vt_ref[j]
        for c, (comp, s) in enumerate(chains):
            nk = keys_needed(s, diagonal)
            st = s_ref[c, :nk, :]
            idx = i * nstrip + s
            if diagonal:
                kpos = lax.broadcasted_iota(jnp.int32, st.shape, 0)
                qpos = lax.broadcasted_iota(jnp.int32, st.shape, 1) + s * _ATTN_STRIP
                st = jnp.where(kpos <= qpos, st, -jnp.inf)
                m_new = jnp.max(st, axis=0, keepdims=True)
                p = jnp.exp2(st - m_new)
                l_ref[comp, idx] = jnp.sum(p, axis=0, keepdims=True)
                acc_ref[comp, idx] = jnp.dot(vtb[:, :nk], p.astype(BF16),
                                             preferred_element_type=F32)
            else:
                m_old = m_ref[comp, idx]
                m_new = jnp.maximum(m_old, jnp.max(st, axis=0, keepdims=True))
                alpha = jnp.exp2(m_old - m_new)
                p = jnp.exp2(st - m_new)
                l_ref[comp, idx] = alpha * l_ref[comp, idx] + jnp.sum(p, axis=0, keepdims=True)
                acc_ref[comp, idx] = alpha * acc_ref[comp, idx] + jnp.dot(
                    vtb, p.astype(BF16), preferred_element_type=F32)
            m_ref[comp, idx] = m_new

    def next_item(item):
        i, j = item
        wrap = j + 1 >= i
        return jnp.where(wrap, i + 1, i), jnp.where(wrap, 0, j + 1)

    n_parts = nblk // 2
    scores((0, 0), sa_ref, True)
    for t in range(n_parts):
        side_work(t, n_parts)
        scores((2 * t + 1, 2 * t + 1), sb_ref, True)
        softmax_av((2 * t, 2 * t), sa_ref, True)
        if 2 * t + 2 < nblk:
            scores((2 * t + 2, 2 * t + 2), sa_ref, True)
        else:
            scores((1, 0), sa_ref)
        softmax_av((2 * t + 1, 2 * t + 1), sb_ref, True)

    n_off = nblk * (nblk - 1) // 2
    assert n_off % 2 == 0

    def pair(t, item):
        item_b = next_item(item)
        item_c = next_item(item_b)
        scores(item_b, sb_ref)
        softmax_av(item, sa_ref, False)
        scores(item_c, sa_ref)
        softmax_av(item_b, sb_ref, False)
        return item_c
    item = lax.fori_loop(0, n_off // 2 - 1, pair, (jnp.int32(1), jnp.int32(0)))
    item_b = next_item(item)
    scores(item_b, sb_ref)
    softmax_av(item, sa_ref, False)
    softmax_av(item_b, sb_ref, False)

    lam = (jnp.exp(jnp.sum(lq1_ref[...] * lk1_ref[...], axis=-1, keepdims=True))
           - jnp.exp(jnp.sum(lq2_ref[...] * lk2_ref[...], axis=-1, keepdims=True))
           + lam_init)
    sw = sw_ref[...] * (1.0 - lam_init)

    for idx in range(nblk * nstrip):
        yt = acc_ref[0, idx] / l_ref[0, idx] - lam * (acc_ref[1, idx] / l_ref[1, idx])
        yt = yt * lax.rsqrt(jnp.mean(yt * yt, axis=0, keepdims=True) + GN_EPS)
        rows = slice(idx * _ATTN_STRIP, (idx + 1) * _ATTN_STRIP)
        o_ref[rows, :] = (yt.T * sw).astype(o_ref.dtype)


def _diff_attention(proj, lq1, lk1, lq2, lk2, subln_w, cast_weights, *, batch, seq, lam_init,
                    blk=512):
    nh = N_DIFF_HEADS
    nblk = seq // blk
    nchain = 2 * blk // _ATTN_STRIP
    nstrips = seq // _ATTN_STRIP
    steps = batch * nh
    q_off, k_off, v_off = 4 * nh, 5 * nh, 6 * nh
    head_blk = lambda off: pl.BlockSpec((seq, HEAD_DIM), lambda s: (s // nh, off + s % nh))
    lam_spec = pl.BlockSpec((1, DIFF_DQK), lambda s: (0, 0))
    cast_specs = [pl.BlockSpec((w.shape[0] // steps, w.shape[1]), lambda s: (s, 0))
                  for w in cast_weights]
    outs = pl.pallas_call(
        functools.partial(_attn_kernel, blk=blk, lam_init=lam_init, n_cast=len(cast_weights)),
        out_shape=[jax.ShapeDtypeStruct((batch * seq, nh * HEAD_DIM), BF16)]
        + [jax.ShapeDtypeStruct(w.shape, BF16) for w in cast_weights],
        grid=(steps,),
        in_specs=[
            head_blk(q_off), head_blk(k_off), head_blk(v_off),
            lam_spec, lam_spec, lam_spec, lam_spec,
            pl.BlockSpec((1, HEAD_DIM), lambda s: (0, s % nh)),
        ] + cast_specs,
        out_specs=[pl.BlockSpec((seq, HEAD_DIM), lambda s: (s // nh, s % nh))] + cast_specs,
        scratch_shapes=[
            pltpu.VMEM((nblk, HEAD_DIM, blk), BF16),
            pltpu.VMEM((2, nstrips, HEAD_DIM, _ATTN_STRIP), BF16),
            pltpu.VMEM((2, nstrips, 1, _ATTN_STRIP), F32),
            pltpu.VMEM((2, nstrips, 1, _ATTN_STRIP), F32),
            pltpu.VMEM((2, nstrips, HEAD_DIM, _ATTN_STRIP), F32),
            pltpu.VMEM((nchain, blk, _ATTN_STRIP), F32),
            pltpu.VMEM((nchain, blk, _ATTN_STRIP), F32),
        ],
        compiler_params=pltpu.CompilerParams(
            dimension_semantics=("arbitrary",), vmem_limit_bytes=V7X_VMEM_LIMIT_BYTES),
        name="diff_attn",
    )(proj, proj, proj, lq1, lk1, lq2, lk2, subln_w, *cast_weights)
    return outs[0], outs[1:]


_OUT_RC = 512


def _outproj_kernel(yr_ref, yd_ref, w_ref, x_ref, g_ref, x1_ref, *, bm, wr):
    for r in range(bm // _OUT_RC):
        rows = slice(r * _OUT_RC, (r + 1) * _OUT_RC)
        mixed = (jnp.dot(yr_ref[rows, :], w_ref[:wr, :], preferred_element_type=F32)
                 + jnp.dot(yd_ref[rows, :], w_ref[wr:, :], preferred_element_type=F32))
        x1_ref[rows, :] = x_ref[rows, :] + g_ref[0] * mixed


def _outproj(y_ret, y_dif, w_out_bf16, x2d, mod3, *, seq, bm=512, bn=2048):
    bt, d = x2d.shape
    wr = y_ret.shape[1]
    wd = y_dif.shape[1]
    tiles_per_seq = seq // bm
    return pl.pallas_call(
        functools.partial(_outproj_kernel, bm=bm, wr=wr),
        out_shape=jax.ShapeDtypeStruct((bt, d), F32),
        grid=(bt // bm, d // bn),
        in_specs=[
            pl.BlockSpec((bm, wr), lambda m, n: (m, 0)),
            pl.BlockSpec((bm, wd), lambda m, n: (m, 0)),
            pl.BlockSpec((wr + wd, bn), lambda m, n: (0, n)),
            pl.BlockSpec((bm, bn), lambda m, n: (m, n)),
            pl.BlockSpec((1, 1, bn), lambda m, n: ((m // tiles_per_seq) * 6 + 2, 0, n)),
        ],
        out_specs=pl.BlockSpec((bm, bn), lambda m, n: (m, n)),
        compiler_params=pltpu.CompilerParams(
            dimension_semantics=("arbitrary", "arbitrary"), vmem_limit_bytes=V7X_VMEM_LIMIT_BYTES),
        name="outproj",
    )(y_ret, y_dif, w_out_bf16, x2d, mod3)


_MLP_RC = 512
_MLP_NORM_ROWS = 256
_MLP_FINAL_ROWS = 16


def _mlp_kernel(x1_ref, sh_ref, sc_ref, nw_ref, w1_ref, w2_ref, g_ref, fw_ref, o_ref, h_ref,
                *, bm, final_norm):
    f = pl.program_id(1)
    nf = pl.num_programs(1)

    def ffn(first):
        for r in range(bm // _MLP_RC):
            rows = slice(r * _MLP_RC, (r + 1) * _MLP_RC)
            a = jnp.dot(h_ref[rows, :], w1_ref[...], preferred_element_type=F32)
            a = jnp.maximum(a, 0.0)
            a = (a * a).astype(BF16)
            part = jnp.dot(a, w2_ref[...], preferred_element_type=F32)
            if first:
                o_ref[rows, :] = part
            else:
                o_ref[rows, :] += part

    @pl.when(f == 0)
    def _():
        for r in range(bm // _MLP_NORM_ROWS):
            rows = slice(r * _MLP_NORM_ROWS, (r + 1) * _MLP_NORM_ROWS)
            x = x1_ref[rows, :]
            ms = jnp.mean(x * x, axis=-1, keepdims=True)
            y = x * lax.rsqrt(ms + NORM_EPS) * nw_ref[...]
            h_ref[rows, :] = (y * (1.0 + sc_ref[0]) + sh_ref[0]).astype(BF16)
        ffn(True)

    @pl.when(f > 0)
    def _():
        ffn(False)

    @pl.when(f == nf - 1)
    def _():
        for r in range(bm // _MLP_FINAL_ROWS):
            rows = slice(r * _MLP_FINAL_ROWS, (r + 1) * _MLP_FINAL_ROWS)
            x2 = x1_ref[rows, :] + g_ref[0] * o_ref[rows, :]
            if final_norm:
                ms = jnp.mean(x2 * x2, axis=-1, keepdims=True)
                x2 = x2 * lax.rsqrt(ms + NORM_EPS) * fw_ref[...]
            o_ref[rows, :] = x2


def _mlp(x1, w1_bf16, w2_bf16, mod3, norm_w, final_w, *, seq, final_norm, bm=1024, bf=512):
    bt, d = x1.shape
    dff = w1_bf16.shape[1]
    tiles_per_seq = seq // bm
    mod_spec = lambda idx: pl.BlockSpec(
        (1, 1, d), lambda m, f: ((m // tiles_per_seq) * 6 + idx, 0, 0))
    return pl.pallas_call(
        functools.partial(_mlp_kernel, bm=bm, final_norm=final_norm),
        out_shape=jax.ShapeDtypeStruct((bt, d), F32),
        grid=(bt // bm, dff // bf),
        in_specs=[
            pl.BlockSpec((bm, d), lambda m, f: (m, 0)),
            mod_spec(3), mod_spec(4),
            pl.BlockSpec((1, d), lambda m, f: (0, 0)),
            pl.BlockSpec((d, bf), lambda m, f: (0, f)),
            pl.BlockSpec((bf, d), lambda m, f: (f, 0)),
            mod_spec(5),
            pl.BlockSpec((1, d), lambda m, f: (0, 0)),
        ],
        out_specs=pl.BlockSpec((bm, d), lambda m, f: (m, 0)),
        scratch_shapes=[pltpu.VMEM((bm, d), BF16)],
        compiler_params=pltpu.CompilerParams(
            dimension_semantics=("arbitrary", "arbitrary"), vmem_limit_bytes=V7X_VMEM_LIMIT_BYTES),
        name="mlp",
    )(x1, mod3, mod3, norm_w, w1_bf16, w2_bf16, mod3, final_w)


def kernel(x, c, w_ada, b_ada, norm1_w, norm2_w, w_in, ret_gn_w, diff_lq1, diff_lk1, diff_lq2,
           diff_lk2, diff_subln_w, w_out, w_mlp1, w_mlp2, final_norm_w):
    batch, seq, d = x.shape
    depth = w_ada.shape[0]
    c_pad = jnp.pad(c, ((0, F32_SUBLANES - batch), (0, 0)))
    rot = _rotary_lane_constants()
    x2d = x.reshape(batch * seq, d)
    final_w = final_norm_w.reshape(1, d)

    for l in range(depth):
        lam_init = 0.8 - 0.6 * math.exp(-0.3 * l)
        mod, tabs = _ada(c_pad, w_ada[l], b_ada[l].reshape(1, -1), rot, seq=seq)
        mod3 = mod[:batch].reshape(batch * 6, 1, d)
        proj = _proj(x2d, mod3, norm1_w[l].reshape(1, d), w_in[l], tabs, seq=seq)
        y_ret = _retention(proj, ret_gn_w[l].reshape(1, -1), batch=batch, seq=seq)
        y_dif, (w_out_b, w1_b, w2_b) = _diff_attention(
            proj, diff_lq1[l].reshape(1, -1), diff_lk1[l].reshape(1, -1),
            diff_lq2[l].reshape(1, -1), diff_lk2[l].reshape(1, -1),
            diff_subln_w[l].reshape(1, -1), (w_out[l], w_mlp1[l], w_mlp2[l]),
            batch=batch, seq=seq, lam_init=lam_init)
        x1 = _outproj(y_ret, y_dif, w_out_b, x2d, mod3, seq=seq)
        x2d = _mlp(x1, w1_b, w2_b, mod3, norm2_w[l].reshape(1, d), final_w, seq=seq,
                   final_norm=(l == depth - 1))
    if depth == 0:
        raise ValueError("depth must be positive")
    return x2d.reshape(batch, seq, d)
```

```python
import functools
import math

import jax
import jax.numpy as jnp
from jax import lax
from jax.experimental import pallas as pl
from jax.experimental.pallas import tpu as pltpu

F32 = jnp.float32
BF16 = jnp.bfloat16

HEAD_DIM = 128
N_RET_HEADS = 8
N_DIFF_HEADS = 8
DIFF_DQK = HEAD_DIM // 2
DIFF_ROT_DIM = DIFF_DQK // 4
RET_CHUNK = 128
ROPE_THETA = 500000.0
RET_THETA = 10000.0
NORM_EPS = 1e-6
GN_EPS = 1e-5

V7X_VMEM_LIMIT_BYTES = 58 * 1024 * 1024
F32_SUBLANES = 8


def _silu(v):
    return v * jax.nn.sigmoid(v)


def _ada_kernel(c_ref, w_ref, b_ref, rot_ref, o_ref, ret_c_ref, ret_s_ref, dif_c_ref,
                dif_a_ref, dif_b_ref, base_ref, *, tab_rows):
    s = _silu(c_ref[...]).astype(BF16)
    part = jnp.dot(s, w_ref[...].astype(BF16), preferred_element_type=F32)

    @pl.when(pl.program_id(0) == 0)
    def _():
        o_ref[...] = part + b_ref[...]

    @pl.when(pl.program_id(0) > 0)
    def _():
        o_ref[...] += part

    freq = (rot_ref[0:1, :], rot_ref[2:3, :])

    @pl.when(pl.program_id(0) == 0)
    def _():
        r = lax.broadcasted_iota(jnp.int32, (tab_rows, HEAD_DIM), 0).astype(F32)
        for k, w in enumerate(freq):
            base_ref[2 * k] = jnp.cos(r * w)
            base_ref[2 * k + 1] = jnp.sin(r * w)

    p0 = jnp.full((F32_SUBLANES, HEAD_DIM), pl.program_id(0) * tab_rows, jnp.int32).astype(F32)
    cos_sin = []
    for k, w in enumerate(freq):
        off_c = jnp.cos(p0 * w)[0:1, :]
        off_s = jnp.sin(p0 * w)[0:1, :]
        base_c, base_s = base_ref[2 * k], base_ref[2 * k + 1]
        cos_sin.append((off_c * base_c - off_s * base_s, off_s * base_c + off_c * base_s))
    (cos_ret, sin_ret), (cos_dif, sin_dif) = cos_sin
    ret_c_ref[...] = cos_ret
    ret_s_ref[...] = sin_ret * rot_ref[1:2, :]
    dif_c_ref[...] = cos_dif
    dif_a_ref[...] = sin_dif * rot_ref[3:4, :]
    dif_b_ref[...] = sin_dif * rot_ref[4:5, :]


def _ada(c_pad, w_ada, b_ada, rot, *, seq, bk=128):
    rows, d = c_pad.shape
    n = w_ada.shape[1]
    steps = d // bk
    tab_rows = seq // steps
    assert steps * bk == d and steps * tab_rows == seq and tab_rows % F32_SUBLANES == 0
    tab_spec = pl.BlockSpec((tab_rows, HEAD_DIM), lambda j: (j, 0))
    outs = pl.pallas_call(
        functools.partial(_ada_kernel, tab_rows=tab_rows),
        out_shape=[jax.ShapeDtypeStruct((rows, n), F32)]
        + [jax.ShapeDtypeStruct((seq, HEAD_DIM), F32)] * 5,
        grid=(steps,),
        in_specs=[
            pl.BlockSpec((rows, bk), lambda j: (0, j)),
            pl.BlockSpec((bk, n), lambda j: (j, 0)),
            pl.BlockSpec((1, n), lambda j: (0, 0)),
            pl.BlockSpec(rot.shape, lambda j: (0, 0)),
        ],
        out_specs=[pl.BlockSpec((rows, n), lambda j: (0, 0))] + [tab_spec] * 5,
        scratch_shapes=[pltpu.VMEM((4, tab_rows, HEAD_DIM), F32)],
        compiler_params=pltpu.CompilerParams(
            dimension_semantics=("arbitrary",), vmem_limit_bytes=V7X_VMEM_LIMIT_BYTES),
        name="ada",
    )(c_pad, w_ada, b_ada, rot)
    return outs[0], tuple(outs[1:])


_PROJ_RC = 512
_PROJ_CC = 256
_PROJ_NORM_ROWS = 256


def _proj_kernel(x_ref, sh_ref, sc_ref, nw_ref, w_ref, rc_ref, rs_ref, dc_ref, da_ref, db_ref,
                 o_ref, h_ref, wb_ref, *, bm, bn):
    j = pl.program_id(1)
    n_rc = bm // _PROJ_RC

    def norm_modulate():
        for r in range(bm // _PROJ_NORM_ROWS):
            rows = slice(r * _PROJ_NORM_ROWS, (r + 1) * _PROJ_NORM_ROWS)
            x = x_ref[rows, :]
            ms = jnp.mean(x * x, axis=-1, keepdims=True)
            y = x * lax.rsqrt(ms + NORM_EPS) * nw_ref[...]
            h_ref[rows, :] = (y * (1.0 + sc_ref[0]) + sh_ref[0]).astype(BF16)

    def run(epilogue):
        for cc in range(bn // _PROJ_CC):
            cols = slice(cc * _PROJ_CC, (cc + 1) * _PROJ_CC)
            wb_ref[:, cols] = w_ref[:, cols].astype(BF16)
            for r in range(n_rc):
                rows = slice(r * _PROJ_RC, (r + 1) * _PROJ_RC)
                acc = jnp.dot(h_ref[rows, :], wb_ref[:, cols], preferred_element_type=F32)
                for hb in range(_PROJ_CC // HEAD_DIM):
                    a = acc[:, hb * HEAD_DIM:(hb + 1) * HEAD_DIM]
                    c0 = cc * _PROJ_CC + hb * HEAD_DIM
                    o_ref[rows, c0:c0 + HEAD_DIM] = epilogue(a, rows).astype(o_ref.dtype)

    def ret_rotary(a, rows):
        return a * rc_ref[rows, :] + pltpu.roll(a, HEAD_DIM // 2, 1) * rs_ref[rows, :]

    @pl.when(j == 0)
    def _():
        norm_modulate()
        run(ret_rotary)

    @pl.when(j == 1)
    def _():
        run(lambda a, rows: ret_rotary(a, rows) * (HEAD_DIM ** -0.5))

    @pl.when(jnp.logical_or(j == 4, j == 5))
    def _():
        scale = jnp.where(j == 4, DIFF_DQK ** -0.5 * math.log2(math.e), 1.0).astype(F32)
        half = DIFF_ROT_DIM // 2

        def ep(a, rows):
            rot = (a * dc_ref[rows, :]
                   + pltpu.roll(a, HEAD_DIM - half, 1) * da_ref[rows, :]
                   + pltpu.roll(a, half, 1) * db_ref[rows, :])
            return rot * scale
        run(ep)

    @pl.when(jnp.logical_or(jnp.logical_or(j == 2, j == 3), j == 6))
    def _():
        run(lambda a, rows: a)


def _proj(x2d, mod3, norm_w, w_in, tabs, *, seq, bm=1024, bn=1024):
    bt, d = x2d.shape
    n = w_in.shape[1]
    tiles_per_seq = seq // bm
    n_m = bt // bm
    def tab_spec(last_use):
        return pl.BlockSpec((bm, HEAD_DIM), lambda m, j: (
            jnp.minimum(jnp.where(j > last_use, m + 1, m), n_m - 1) % tiles_per_seq, 0))
    return pl.pallas_call(
        functools.partial(_proj_kernel, bm=bm, bn=bn),
        out_shape=jax.ShapeDtypeStruct((bt, n), BF16),
        grid=(bt // bm, n // bn),
        in_specs=[
            pl.BlockSpec((bm, d), lambda m, j: (jnp.minimum(jnp.where(j > 0, m + 1, m), n_m - 1), 0)),
            pl.BlockSpec((1, 1, d), lambda m, j: ((m // tiles_per_seq) * 6 + 0, 0, 0)),
            pl.BlockSpec((1, 1, d), lambda m, j: ((m // tiles_per_seq) * 6 + 1, 0, 0)),
            pl.BlockSpec((1, d), lambda m, j: (0, 0)),
            pl.BlockSpec((d, bn), lambda m, j: (0, j)),
            tab_spec(1), tab_spec(1), tab_spec(5), tab_spec(5), tab_spec(5),
        ],
        out_specs=pl.BlockSpec((bm, bn), lambda m, j: (m, j)),
        scratch_shapes=[pltpu.VMEM((bm, d), BF16), pltpu.VMEM((d, bn), BF16)],
        compiler_params=pltpu.CompilerParams(
            dimension_semantics=("arbitrary", "arbitrary"), vmem_limit_bytes=V7X_VMEM_LIMIT_BYTES),
        name="proj",
    )(x2d, mod3, mod3, norm_w, w_in, *tabs)


def _rotary_lane_constants():
    lane = jnp.arange(HEAD_DIM)
    half = HEAD_DIM // 2
    inv_ret = RET_THETA ** (-(2 * (lane % half)).astype(F32) / HEAD_DIM)
    sign_ret = jnp.where(lane < half, -1.0, 1.0)
    half = DIFF_ROT_DIM // 2
    within = lane % DIFF_DQK
    inv_dif = ROPE_THETA ** (-(2 * (within % half)).astype(F32) / DIFF_ROT_DIM)
    inv_dif = jnp.where(within < DIFF_ROT_DIM, inv_dif, 0.0)
    sel_a = jnp.where(within < half, -1.0, 0.0)
    sel_b = jnp.where((within >= half) & (within < DIFF_ROT_DIM), 1.0, 0.0)
    rows = [inv_ret, sign_ret, inv_dif, sel_a, sel_b]
    rows += [jnp.zeros((HEAD_DIM,), F32)] * (F32_SUBLANES - len(rows))
    return jnp.stack(rows).astype(F32)


def _ret_kernel(q_ref, k_ref, v_ref, g_ref, gw_ref, o_ref, s_ref, *, seq):
    C = RET_CHUNK
    nc = seq // C
    head = (pl.program_id(0) % N_RET_HEADS).astype(F32)
    ii = lax.broadcasted_iota(jnp.int32, (C, C), 0).astype(F32)
    jj = lax.broadcasted_iota(jnp.int32, (C, C), 1).astype(F32)
    lg = jnp.log(1.0 - jnp.exp2(jnp.zeros((C, C), F32) - 5.0 - head))
    rel = ii - jj
    decay = jnp.where(rel >= 0, jnp.exp(jnp.maximum(rel, 0.0) * lg), 0.0)
    xi = jnp.exp((ii + 1.0) * lg)
    zeta = jnp.exp((C - 1.0 - ii) * lg)
    chunk_decay = jnp.exp(C * lg)

    def scan_body(c, state):
        rows = pl.ds(pl.multiple_of(c * C, C), C)
        s_ref[c] = state.astype(BF16)
        kz = (k_ref[rows, :].astype(F32) * zeta).astype(BF16)
        kv = lax.dot_general(kz, v_ref[rows, :], (((0,), (0,)), ((), ())),
                             preferred_element_type=F32)
        return state * chunk_decay + kv
    lax.fori_loop(0, nc, scan_body, jnp.zeros((C, C), F32), unroll=8)

    gw = gw_ref[...]

    def out_body(c, carry):
        rows = pl.ds(pl.multiple_of(c * C, C), C)
        q = q_ref[rows, :]
        inner = lax.dot_general(q, k_ref[rows, :], (((1,), (1,)), ((), ())),
                                preferred_element_type=F32) * decay
        o = jnp.dot(inner.astype(BF16), v_ref[rows, :], preferred_element_type=F32)
        o = o + jnp.dot(q, s_ref[c], preferred_element_type=F32) * xi
        mu = jnp.mean(o, axis=-1, keepdims=True)
        d = o - mu
        var = jnp.mean(d * d, axis=-1, keepdims=True)
        y = d * lax.rsqrt(var + GN_EPS) * gw
        o_ref[rows, :] = (y * _silu(g_ref[rows, :].astype(F32))).astype(o_ref.dtype)
        return carry
    lax.fori_loop(0, nc, out_body, 0, unroll=16)


def _retention(proj, gn_w, *, batch, seq):
    nh = N_RET_HEADS
    blk = lambda off: pl.BlockSpec((seq, HEAD_DIM), lambda s: (s // nh, off + s % nh))
    return pl.pallas_call(
        functools.partial(_ret_kernel, seq=seq),
        out_shape=jax.ShapeDtypeStruct((batch * seq, nh * HEAD_DIM), BF16),
        grid=(batch * nh,),
        in_specs=[blk(0), blk(nh), blk(2 * nh), blk(3 * nh),
                  pl.BlockSpec((1, HEAD_DIM), lambda s: (0, s % nh))],
        out_specs=pl.BlockSpec((seq, HEAD_DIM), lambda s: (s // nh, s % nh)),
        scratch_shapes=[pltpu.VMEM((seq // RET_CHUNK, RET_CHUNK, HEAD_DIM), BF16)],
        compiler_params=pltpu.CompilerParams(
            dimension_semantics=("arbitrary",), vmem_limit_bytes=V7X_VMEM_LIMIT_BYTES),
        name="retention",
    )(proj, proj, proj, proj, gn_w)


_ATTN_STRIP = 256


_ATTN_N_IN = 8


def _attn_kernel(*refs, blk, lam_init, n_cast):
    q_ref, k_ref, v_ref, lq1_ref, lk1_ref, lq2_ref, lk2_ref, sw_ref = refs[:_ATTN_N_IN]
    cast_src = refs[_ATTN_N_IN:_ATTN_N_IN + n_cast]
    o_ref = refs[_ATTN_N_IN + n_cast]
    cast_dst = refs[_ATTN_N_IN + n_cast + 1:_ATTN_N_IN + 2 * n_cast + 1]
    vt_ref, qt_ref, m_ref, l_ref, acc_ref, sa_ref, sb_ref = refs[_ATTN_N_IN + 2 * n_cast + 1:]

    def side_work(part, n_parts):
        for src, dst in zip(cast_src, cast_dst):
            rows = src.shape[0] // n_parts
            dst[part * rows:(part + 1) * rows, :] = src[part * rows:(part + 1) * rows, :].astype(
                dst.dtype)

    _attn_body(q_ref, k_ref, v_ref, lq1_ref, lk1_ref, lq2_ref, lk2_ref, sw_ref, o_ref,
               vt_ref, qt_ref, m_ref, l_ref, acc_ref, sa_ref, sb_ref, side_work,
               blk=blk, lam_init=lam_init)


def _attn_body(q_ref, k_ref, v_ref, lq1_ref, lk1_ref, lq2_ref, lk2_ref, sw_ref, o_ref,
               vt_ref, qt_ref, m_ref, l_ref, acc_ref, sa_ref, sb_ref, side_work, *, blk, lam_init):
    nblk = vt_ref.shape[0]
    nstrip = blk // _ATTN_STRIP
    assert nblk % 2 == 0

    for jb in range(nblk):
        vt_ref[jb] = v_ref[jb * blk:(jb + 1) * blk, :].T

    feat = lax.broadcasted_iota(jnp.int32, (HEAD_DIM, _ATTN_STRIP), 0)
    for idx in range(nblk * nstrip):
        qt = q_ref[idx * _ATTN_STRIP:(idx + 1) * _ATTN_STRIP, :].T
        zero = jnp.zeros_like(qt)
        qt_ref[0, idx] = jnp.where(feat < DIFF_DQK, qt, zero)
        qt_ref[1, idx] = jnp.where(feat >= DIFF_DQK, qt, zero)

    chains = [(comp, s) for comp in range(2) for s in range(nstrip)]

    def keys_needed(s, diagonal):
        return (s + 1) * _ATTN_STRIP if diagonal else blk

    def scores(item, s_ref, diagonal=False):
        i, j = item
        jrow = j * blk
        if not isinstance(jrow, int):
            jrow = pl.multiple_of(jrow, blk)
        for c, (comp, s) in enumerate(chains):
            nk = keys_needed(s, diagonal)
            s_ref[c, :nk, :] = jnp.dot(k_ref[pl.ds(jrow, nk), :], qt_ref[comp, i * nstrip + s],
                                       preferred_element_type=F32)

    def softmax_av(item, s_ref, diagonal):
        i, j = item
        vtb = vt_ref[j]
        for c, (comp, s) in enumerate(chains):
            nk = keys_needed(s, diagonal)
            st = s_ref[c, :nk, :]
            idx = i * nstrip + s
            if diagonal:
                kpos = lax.broadcasted_iota(jnp.int32, st.shape, 0)
                qpos = lax.broadcasted_iota(jnp.int32, st.shape, 1) + s * _ATTN_STRIP
                st = jnp.where(kpos <= qpos, st, -jnp.inf)
                m_new = jnp.max(st, axis=0, keepdims=True)
                p = jnp.exp2(st - m_new)
                l_ref[comp, idx] = jnp.sum(p, axis=0, keepdims=True)
                acc_ref[comp, idx] = jnp.dot(vtb[:, :nk], p.astype(BF16),
                                             preferred_element_type=F32)
            else:
                m_old = m_ref[comp, idx]
                m_new = jnp.maximum(m_old, jnp.max(st, axis=0, keepdims=True))
                alpha = jnp.exp2(m_old - m_new)
                p = jnp.exp2(st - m_new)
                l_ref[comp, idx] = alpha * l_ref[comp, idx] + jnp.sum(p, axis=0, keepdims=True)
                acc_ref[comp, idx] = alpha * acc_ref[comp, idx] + jnp.dot(
                    vtb, p.astype(BF16), preferred_element_type=F32)
            m_ref[comp, idx] = m_new

    def next_item(item):
        i, j = item
        wrap = j + 1 >= i
        return jnp.where(wrap, i + 1, i), jnp.where(wrap, 0, j + 1)

    n_parts = nblk // 2
    scores((0, 0), sa_ref, True)
    for t in range(n_parts):
        side_work(t, n_parts)
        scores((2 * t + 1, 2 * t + 1), sb_ref, True)
        softmax_av((2 * t, 2 * t), sa_ref, True)
        if 2 * t + 2 < nblk:
            scores((2 * t + 2, 2 * t + 2), sa_ref, True)
        else:
            scores((1, 0), sa_ref)
        softmax_av((2 * t + 1, 2 * t + 1), sb_ref, True)

    n_off = nblk * (nblk - 1) // 2
    assert n_off % 2 == 0

    def pair(t, item):
        item_b = next_item(item)
        item_c = next_item(item_b)
        scores(item_b, sb_ref)
        softmax_av(item, sa_ref, False)
        scores(item_c, sa_ref)
        softmax_av(item_b, sb_ref, False)
        return item_c
    item = lax.fori_loop(0, n_off // 2 - 1, pair, (jnp.int32(1), jnp.int32(0)))
    item_b = next_item(item)
    scores(item_b, sb_ref)
    softmax_av(item, sa_ref, False)
    softmax_av(item_b, sb_ref, False)

    lam = (jnp.exp(jnp.sum(lq1_ref[...] * lk1_ref[...], axis=-1, keepdims=True))
           - jnp.exp(jnp.sum(lq2_ref[...] * lk2_ref[...], axis=-1, keepdims=True))
           + lam_init)
    sw = sw_ref[...] * (1.0 - lam_init)

    for idx in range(nblk * nstrip):
        yt = acc_ref[0, idx] / l_ref[0, idx] - lam * (acc_ref[1, idx] / l_ref[1, idx])
        yt = yt * lax.rsqrt(jnp.mean(yt * yt, axis=0, keepdims=True) + GN_EPS)
        rows = slice(idx * _ATTN_STRIP, (idx + 1) * _ATTN_STRIP)
        o_ref[rows, :] = (yt.T * sw).astype(o_ref.dtype)


def _diff_attention(proj, lq1, lk1, lq2, lk2, subln_w, cast_weights, *, batch, seq, lam_init,
                    blk=512):
    nh = N_DIFF_HEADS
    nblk = seq // blk
    nchain = 2 * blk // _ATTN_STRIP
    nstrips = seq // _ATTN_STRIP
    steps = batch * nh
    q_off, k_off, v_off = 4 * nh, 5 * nh, 6 * nh
    head_blk = lambda off: pl.BlockSpec((seq, HEAD_DIM), lambda s: (s // nh, off + s % nh))
    lam_spec = pl.BlockSpec((1, DIFF_DQK), lambda s: (0, 0))
    cast_specs = [pl.BlockSpec((w.shape[0] // steps, w.shape[1]), lambda s: (s, 0))
                  for w in cast_weights]
    outs = pl.pallas_call(
        functools.partial(_attn_kernel, blk=blk, lam_init=lam_init, n_cast=len(cast_weights)),
        out_shape=[jax.ShapeDtypeStruct((batch * seq, nh * HEAD_DIM), BF16)]
        + [jax.ShapeDtypeStruct(w.shape, BF16) for w in cast_weights],
        grid=(steps,),
        in_specs=[
            head_blk(q_off), head_blk(k_off), head_blk(v_off),
            lam_spec, lam_spec, lam_spec, lam_spec,
            pl.BlockSpec((1, HEAD_DIM), lambda s: (0, s % nh)),
        ] + cast_specs,
        out_specs=[pl.BlockSpec((seq, HEAD_DIM), lambda s: (s // nh, s % nh))] + cast_specs,
        scratch_shapes=[
            pltpu.VMEM((nblk, HEAD_DIM, blk), BF16),
            pltpu.VMEM((2, nstrips, HEAD_DIM, _ATTN_STRIP), BF16),
            pltpu.VMEM((2, nstrips, 1, _ATTN_STRIP), F32),
            pltpu.VMEM((2, nstrips, 1, _ATTN_STRIP), F32),
            pltpu.VMEM((2, nstrips, HEAD_DIM, _ATTN_STRIP), F32),
            pltpu.VMEM((nchain, blk, _ATTN_STRIP), F32),
            pltpu.VMEM((nchain, blk, _ATTN_STRIP), F32),
        ],
        compiler_params=pltpu.CompilerParams(
            dimension_semantics=("arbitrary",), vmem_limit_bytes=V7X_VMEM_LIMIT_BYTES),
        name="diff_attn",
    )(proj, proj, proj, lq1, lk1, lq2, lk2, subln_w, *cast_weights)
    return outs[0], outs[1:]


_OUT_RC = 512


def _outproj_kernel(yr_ref, yd_ref, w_ref, x_ref, g_ref, x1_ref, *, bm, wr):
    for r in range(bm // _OUT_RC):
        rows = slice(r * _OUT_RC, (r + 1) * _OUT_RC)
        mixed = (jnp.dot(yr_ref[rows, :], w_ref[:wr, :], preferred_element_type=F32)
                 + jnp.dot(yd_ref[rows, :], w_ref[wr:, :], preferred_element_type=F32))
        x1_ref[rows, :] = x_ref[rows, :] + g_ref[0] * mixed


def _outproj(y_ret, y_dif, w_out_bf16, x2d, mod3, *, seq, bm=512, bn=2048):
    bt, d = x2d.shape
    wr = y_ret.shape[1]
    wd = y_dif.shape[1]
    tiles_per_seq = seq // bm
    return pl.pallas_call(
        functools.partial(_outproj_kernel, bm=bm, wr=wr),
        out_shape=jax.ShapeDtypeStruct((bt, d), F32),
        grid=(bt // bm, d // bn),
        in_specs=[
            pl.BlockSpec((bm, wr), lambda m, n: (m, 0)),
            pl.BlockSpec((bm, wd), lambda m, n: (m, 0)),
            pl.BlockSpec((wr + wd, bn), lambda m, n: (0, n)),
            pl.BlockSpec((bm, bn), lambda m, n: (m, n)),
            pl.BlockSpec((1, 1, bn), lambda m, n: ((m // tiles_per_seq) * 6 + 2, 0, n)),
        ],
        out_specs=pl.BlockSpec((bm, bn), lambda m, n: (m, n)),
        compiler_params=pltpu.CompilerParams(
            dimension_semantics=("arbitrary", "arbitrary"), vmem_limit_bytes=V7X_VMEM_LIMIT_BYTES),
        name="outproj",
    )(y_ret, y_dif, w_out_bf16, x2d, mod3)


_MLP_RC = 512
_MLP_NORM_ROWS = 256
_MLP_FINAL_ROWS = 16


def _mlp_kernel(x1_ref, sh_ref, sc_ref, nw_ref, w1_ref, w2_ref, g_ref, fw_ref, o_ref, h_ref,
                *, bm, final_norm):
    f = pl.program_id(1)
    nf = pl.num_programs(1)

    def ffn(first):
        for r in range(bm // _MLP_RC):
            rows = slice(r * _MLP_RC, (r + 1) * _MLP_RC)
            a = jnp.dot(h_ref[rows, :], w1_ref[...], preferred_element_type=F32)
            a = jnp.maximum(a, 0.0)
            a = (a * a).astype(BF16)
            part = jnp.dot(a, w2_ref[...], preferred_element_type=F32)
            if first:
                o_ref[rows, :] = part
            else:
                o_ref[rows, :] += part

    @pl.when(f == 0)
    def _():
        for r in range(bm // _MLP_NORM_ROWS):
            rows = slice(r * _MLP_NORM_ROWS, (r + 1) * _MLP_NORM_ROWS)
            x = x1_ref[rows, :]
            ms = jnp.mean(x * x, axis=-1, keepdims=True)
            y = x * lax.rsqrt(ms + NORM_EPS) * nw_ref[...]
            h_ref[rows, :] = (y * (1.0 + sc_ref[0]) + sh_ref[0]).astype(BF16)
        ffn(True)

    @pl.when(f > 0)
    def _():
        ffn(False)

    @pl.when(f == nf - 1)
    def _():
        for r in range(bm // _MLP_FINAL_ROWS):
            rows = slice(r * _MLP_FINAL_ROWS, (r + 1) * _MLP_FINAL_ROWS)
            x2 = x1_ref[rows, :] + g_ref[0] * o_ref[rows, :]
            if final_norm:
                ms = jnp.mean(x2 * x2, axis=-1, keepdims=True)
                x2 = x2 * lax.rsqrt(ms + NORM_EPS) * fw_ref[...]
            o_ref[rows, :] = x2


def _mlp(x1, w1_bf16, w2_bf16, mod3, norm_w, final_w, *, seq, final_norm, bm=1024, bf=512):
    bt, d = x1.shape
    dff = w1_bf16.shape[1]
    tiles_per_seq = seq // bm
    mod_spec = lambda idx: pl.BlockSpec(
        (1, 1, d), lambda m, f: ((m // tiles_per_seq) * 6 + idx, 0, 0))
    return pl.pallas_call(
        functools.partial(_mlp_kernel, bm=bm, final_norm=final_norm),
        out_shape=jax.ShapeDtypeStruct((bt, d), F32),
        grid=(bt // bm, dff // bf),
        in_specs=[
            pl.BlockSpec((bm, d), lambda m, f: (m, 0)),
            mod_spec(3), mod_spec(4),
            pl.BlockSpec((1, d), lambda m, f: (0, 0)),
            pl.BlockSpec((d, bf), lambda m, f: (0, f)),
            pl.BlockSpec((bf, d), lambda m, f: (f, 0)),
            mod_spec(5),
            pl.BlockSpec((1, d), lambda m, f: (0, 0)),
        ],
        out_specs=pl.BlockSpec((bm, d), lambda m, f: (m, 0)),
        scratch_shapes=[pltpu.VMEM((bm, d), BF16)],
        compiler_params=pltpu.CompilerParams(
            dimension_semantics=("arbitrary", "arbitrary"), vmem_limit_bytes=V7X_VMEM_LIMIT_BYTES),
        name="mlp",
    )(x1, mod3, mod3, norm_w, w1_bf16, w2_bf16, mod3, final_w)


def kernel(x, c, w_ada, b_ada, norm1_w, norm2_w, w_in, ret_gn_w, diff_lq1, diff_lk1, diff_lq2,
           diff_lk2, diff_subln_w, w_out, w_mlp1, w_mlp2, final_norm_w):
    batch, seq, d = x.shape
    depth = w_ada.shape[0]
    c_pad = jnp.pad(c, ((0, F32_SUBLANES - batch), (0, 0)))
    rot = _rotary_lane_constants()
    x2d = x.reshape(batch * seq, d)
    final_w = final_norm_w.reshape(1, d)

    for l in range(depth):
        lam_init = 0.8 - 0.6 * math.exp(-0.3 * l)
        mod, tabs = _ada(c_pad, w_ada[l], b_ada[l].reshape(1, -1), rot, seq=seq)
        mod3 = mod[:batch].reshape(batch * 6, 1, d)
        proj = _proj(x2d, mod3, norm1_w[l].reshape(1, d), w_in[l], tabs, seq=seq)
        y_ret = _retention(proj, ret_gn_w[l].reshape(1, -1), batch=batch, seq=seq)
        y_dif, (w_out_b, w1_b, w2_b) = _diff_attention(
            proj, diff_lq1[l].reshape(1, -1), diff_lk1[l].reshape(1, -1),
            diff_lq2[l].reshape(1, -1), diff_lk2[l].reshape(1, -1),
            diff_subln_w[l].reshape(1, -1), (w_out[l], w_mlp1[l], w_mlp2[l]),
            batch=batch, seq=seq, lam_init=lam_init)
        x1 = _outproj(y_ret, y_dif, w_out_b, x2d, mod3, seq=seq)
        x2d = _mlp(x1, w1_b, w2_b, mod3, norm2_w[l].reshape(1, d), final_w, seq=seq,
                   final_norm=(l == depth - 1))
    if depth == 0:
        raise ValueError("depth must be positive")
    return x2d.reshape(batch, seq, d)
```

```python
import functools
import math

import jax
import jax.numpy as jnp
from jax import lax
from jax.experimental import pallas as pl
from jax.experimental.pallas import tpu as pltpu

F32 = jnp.float32
BF16 = jnp.bfloat16

HEAD_DIM = 128
N_RET_HEADS = 8
N_DIFF_HEADS = 8
DIFF_DQK = HEAD_DIM // 2
DIFF_ROT_DIM = DIFF_DQK // 4
RET_CHUNK = 128
ROPE_THETA = 500000.0
RET_THETA = 10000.0
NORM_EPS = 1e-6
GN_EPS = 1e-5

V7X_VMEM_LIMIT_BYTES = 58 * 1024 * 1024
F32_SUBLANES = 8


def _silu(v):
    return v * jax.nn.sigmoid(v)


def _ada_kernel(c_ref, w_ref, b_ref, rot_ref, o_ref, ret_c_ref, ret_s_ref, dif_c_ref,
                dif_a_ref, dif_b_ref, base_ref, *, tab_rows):
    s = _silu(c_ref[...]).astype(BF16)
    part = jnp.dot(s, w_ref[...].astype(BF16), preferred_element_type=F32)

    @pl.when(pl.program_id(0) == 0)
    def _():
        o_ref[...] = part + b_ref[...]

    @pl.when(pl.program_id(0) > 0)
    def _():
        o_ref[...] += part

    freq = (rot_ref[0:1, :], rot_ref[2:3, :])

    @pl.when(pl.program_id(0) == 0)
    def _():
        r = lax.broadcasted_iota(jnp.int32, (tab_rows, HEAD_DIM), 0).astype(F32)
        for k, w in enumerate(freq):
            base_ref[2 * k] = jnp.cos(r * w)
            base_ref[2 * k + 1] = jnp.sin(r * w)

    p0 = jnp.full((F32_SUBLANES, HEAD_DIM), pl.program_id(0) * tab_rows, jnp.int32).astype(F32)
    cos_sin = []
    for k, w in enumerate(freq):
        off_c = jnp.cos(p0 * w)[0:1, :]
        off_s = jnp.sin(p0 * w)[0:1, :]
        base_c, base_s = base_ref[2 * k], base_ref[2 * k + 1]
        cos_sin.append((off_c * base_c - off_s * base_s, off_s * base_c + off_c * base_s))
    (cos_ret, sin_ret), (cos_dif, sin_dif) = cos_sin
    ret_c_ref[...] = cos_ret
    ret_s_ref[...] = sin_ret * rot_ref[1:2, :]
    dif_c_ref[...] = cos_dif
    dif_a_ref[...] = sin_dif * rot_ref[3:4, :]
    dif_b_ref[...] = sin_dif * rot_ref[4:5, :]


def _ada(c_pad, w_ada, b_ada, rot, *, seq, bk=128):
    rows, d = c_pad.shape
    n = w_ada.shape[1]
    steps = d // bk
    tab_rows = seq // steps
    assert steps * bk == d and steps * tab_rows == seq and tab_rows % F32_SUBLANES == 0
    tab_spec = pl.BlockSpec((tab_rows, HEAD_DIM), lambda j: (j, 0))
    outs = pl.pallas_call(
        functools.partial(_ada_kernel, tab_rows=tab_rows),
        out_shape=[jax.ShapeDtypeStruct((rows, n), F32)]
        + [jax.ShapeDtypeStruct((seq, HEAD_DIM), F32)] * 5,
        grid=(steps,),
        in_specs=[
            pl.BlockSpec((rows, bk), lambda j: (0, j)),
            pl.BlockSpec((bk, n), lambda j: (j, 0)),
            pl.BlockSpec((1, n), lambda j: (0, 0)),
            pl.BlockSpec(rot.shape, lambda j: (0, 0)),
        ],
        out_specs=[pl.BlockSpec((rows, n), lambda j: (0, 0))] + [tab_spec] * 5,
        scratch_shapes=[pltpu.VMEM((4, tab_rows, HEAD_DIM), F32)],
        compiler_params=pltpu.CompilerParams(
            dimension_semantics=("arbitrary",), vmem_limit_bytes=V7X_VMEM_LIMIT_BYTES),
        name="ada",
    )(c_pad, w_ada, b_ada, rot)
    return outs[0], tuple(outs[1:])


_PROJ_RC = 512
_PROJ_CC = 256
_PROJ_NORM_ROWS = 256


def _proj_kernel(x_ref, sh_ref, sc_ref, nw_ref, w_ref, rc_ref, rs_ref, dc_ref, da_ref, db_ref,
                 o_ref, h_ref, wb_ref, *, bm, bn):
    j = pl.program_id(1)
    n_rc = bm // _PROJ_RC

    def norm_modulate():
        for r in range(bm // _PROJ_NORM_ROWS):
            rows = slice(r * _PROJ_NORM_ROWS, (r + 1) * _PROJ_NORM_ROWS)
            x = x_ref[rows, :]
            ms = jnp.mean(x * x, axis=-1, keepdims=True)
            y = x * lax.rsqrt(ms + NORM_EPS) * nw_ref[...]
            h_ref[rows, :] = (y * (1.0 + sc_ref[0]) + sh_ref[0]).astype(BF16)

    def run(epilogue):
        for cc in range(bn // _PROJ_CC):
            cols = slice(cc * _PROJ_CC, (cc + 1) * _PROJ_CC)
            wb_ref[:, cols] = w_ref[:, cols].astype(BF16)
            for r in range(n_rc):
                rows = slice(r * _PROJ_RC, (r + 1) * _PROJ_RC)
                acc = jnp.dot(h_ref[rows, :], wb_ref[:, cols], preferred_element_type=F32)
                for hb in range(_PROJ_CC // HEAD_DIM):
                    a = acc[:, hb * HEAD_DIM:(hb + 1) * HEAD_DIM]
                    c0 = cc * _PROJ_CC + hb * HEAD_DIM
                    o_ref[rows, c0:c0 + HEAD_DIM] = epilogue(a, rows).astype(o_ref.dtype)

    def ret_rotary(a, rows):
        return a * rc_ref[rows, :] + pltpu.roll(a, HEAD_DIM // 2, 1) * rs_ref[rows, :]

    @pl.when(j == 0)
    def _():
        norm_modulate()
        run(ret_rotary)

    @pl.when(j == 1)
    def _():
        run(lambda a, rows: ret_rotary(a, rows) * (HEAD_DIM ** -0.5))

    @pl.when(jnp.logical_or(j == 4, j == 5))
    def _():
        scale = jnp.where(j == 4, DIFF_DQK ** -0.5 * math.log2(math.e), 1.0).astype(F32)
        half = DIFF_ROT_DIM // 2

        def ep(a, rows):
            rot = (a * dc_ref[rows, :]
                   + pltpu.roll(a, HEAD_DIM - half, 1) * da_ref[rows, :]
                   + pltpu.roll(a, half, 1) * db_ref[rows, :])
            return rot * scale
        run(ep)

    @pl.when(jnp.logical_or(jnp.logical_or(j == 2, j == 3), j == 6))
    def _():
        run(lambda a, rows: a)


def _proj(x2d, mod3, norm_w, w_in, tabs, *, seq, bm=1024, bn=1024):
    bt, d = x2d.shape
    n = w_in.shape[1]
    tiles_per_seq = seq // bm
    n_m = bt // bm
    def tab_spec(last_use):
        return pl.BlockSpec((bm, HEAD_DIM), lambda m, j: (
            jnp.minimum(jnp.where(j > last_use, m + 1, m), n_m - 1) % tiles_per_seq, 0))
    return pl.pallas_call(
        functools.partial(_proj_kernel, bm=bm, bn=bn),
        out_shape=jax.ShapeDtypeStruct((bt, n), BF16),
        grid=(bt // bm, n // bn),
        in_specs=[
            pl.BlockSpec((bm, d), lambda m, j: (jnp.minimum(jnp.where(j > 0, m + 1, m), n_m - 1), 0)),
            pl.BlockSpec((1, 1, d), lambda m, j: ((m // tiles_per_seq) * 6 + 0, 0, 0)),
            pl.BlockSpec((1, 1, d), lambda m, j: ((m // tiles_per_seq) * 6 + 1, 0, 0)),
            pl.BlockSpec((1, d), lambda m, j: (0, 0)),
            pl.BlockSpec((d, bn), lambda m, j: (0, j)),
            tab_spec(1), tab_spec(1), tab_spec(5), tab_spec(5), tab_spec(5),
        ],
        out_specs=pl.BlockSpec((bm, bn), lambda m, j: (m, j)),
        scratch_shapes=[pltpu.VMEM((bm, d), BF16), pltpu.VMEM((d, bn), BF16)],
        compiler_params=pltpu.CompilerParams(
            dimension_semantics=("arbitrary", "arbitrary"), vmem_limit_bytes=V7X_VMEM_LIMIT_BYTES),
        name="proj",
    )(x2d, mod3, mod3, norm_w, w_in, *tabs)


def _rotary_lane_constants():
    lane = jnp.arange(HEAD_DIM)
    half = HEAD_DIM // 2
    inv_ret = RET_THETA ** (-(2 * (lane % half)).astype(F32) / HEAD_DIM)
    sign_ret = jnp.where(lane < half, -1.0, 1.0)
    half = DIFF_ROT_DIM // 2
    within = lane % DIFF_DQK
    inv_dif = ROPE_THETA ** (-(2 * (within % half)).astype(F32) / DIFF_ROT_DIM)
    inv_dif = jnp.where(within < DIFF_ROT_DIM, inv_dif, 0.0)
    sel_a = jnp.where(within < half, -1.0, 0.0)
    sel_b = jnp.where((within >= half) & (within < DIFF_ROT_DIM), 1.0, 0.0)
    rows = [inv_ret, sign_ret, inv_dif, sel_a, sel_b]
    rows += [jnp.zeros((HEAD_DIM,), F32)] * (F32_SUBLANES - len(rows))
    return jnp.stack(rows).astype(F32)


def _ret_kernel(q_ref, k_ref, v_ref, g_ref, gw_ref, o_ref, s_ref, *, seq):
    C = RET_CHUNK
    nc = seq // C
    head = (pl.program_id(0) % N_RET_HEADS).astype(F32)
    ii = lax.broadcasted_iota(jnp.int32, (C, C), 0).astype(F32)
    jj = lax.broadcasted_iota(jnp.int32, (C, C), 1).astype(F32)
    lg = jnp.log(1.0 - jnp.exp2(jnp.zeros((C, C), F32) - 5.0 - head))
    rel = ii - jj
    decay = jnp.where(rel >= 0, jnp.exp(jnp.maximum(rel, 0.0) * lg), 0.0)
    xi = jnp.exp((ii + 1.0) * lg)
    zeta = jnp.exp((C - 1.0 - ii) * lg)
    chunk_decay = jnp.exp(C * lg)

    def scan_body(c, state):
        rows = pl.ds(pl.multiple_of(c * C, C), C)
        s_ref[c] = state.astype(BF16)
        kz = (k_ref[rows, :].astype(F32) * zeta).astype(BF16)
        kv = lax.dot_general(kz, v_ref[rows, :], (((0,), (0,)), ((), ())),
                             preferred_element_type=F32)
        return state * chunk_decay + kv
    lax.fori_loop(0, nc, scan_body, jnp.zeros((C, C), F32), unroll=8)

    gw = gw_ref[...]

    def out_body(c, carry):
        rows = pl.ds(pl.multiple_of(c * C, C), C)
        q = q_ref[rows, :]
        inner = lax.dot_general(q, k_ref[rows, :], (((1,), (1,)), ((), ())),
                                preferred_element_type=F32) * decay
        o = jnp.dot(inner.astype(BF16), v_ref[rows, :], preferred_element_type=F32)
        o = o + jnp.dot(q, s_ref[c], preferred_element_type=F32) * xi
        mu = jnp.mean(o, axis=-1, keepdims=True)
        d = o - mu
        var = jnp.mean(d * d, axis=-1, keepdims=True)
        y = d * lax.rsqrt(var + GN_EPS) * gw
        o_ref[rows, :] = (y * _silu(g_ref[rows, :].astype(F32))).astype(o_ref.dtype)
        return carry
    lax.fori_loop(0, nc, out_body, 0, unroll=16)


def _retention(proj, gn_w, *, batch, seq):
    nh = N_RET_HEADS
    blk = lambda off: pl.BlockSpec((seq, HEAD_DIM), lambda s: (s // nh, off + s % nh))
    return pl.pallas_call(
        functools.partial(_ret_kernel, seq=seq),
        out_shape=jax.ShapeDtypeStruct((batch * seq, nh * HEAD_DIM), BF16),
        grid=(batch * nh,),
        in_specs=[blk(0), blk(nh), blk(2 * nh), blk(3 * nh),
                  pl.BlockSpec((1, HEAD_DIM), lambda s: (0, s % nh))],
        out_specs=pl.BlockSpec((seq, HEAD_DIM), lambda s: (s // nh, s % nh)),
        scratch_shapes=[pltpu.VMEM((seq // RET_CHUNK, RET_CHUNK, HEAD_DIM), BF16)],
        compiler_params=pltpu.CompilerParams(
            dimension_semantics=("arbitrary",), vmem_limit_bytes=V7X_VMEM_LIMIT_BYTES),
        name="retention",
    )(proj, proj, proj, proj, gn_w)


_ATTN_STRIP = 256
_ATTN_SCORE_BUFS = 4


_ATTN_N_IN = 8


def _attn_kernel(*refs, blk, lam_init, n_cast):
    q_ref, k_ref, v_ref, lq1_ref, lk1_ref, lq2_ref, lk2_ref, sw_ref = refs[:_ATTN_N_IN]
    cast_src = refs[_ATTN_N_IN:_ATTN_N_IN + n_cast]
    o_ref = refs[_ATTN_N_IN + n_cast]
    cast_dst = refs[_ATTN_N_IN + n_cast + 1:_ATTN_N_IN + 2 * n_cast + 1]
    vt_ref, qt_ref, m_ref, l_ref, acc_ref, *score_bufs = refs[_ATTN_N_IN + 2 * n_cast + 1:]

    def side_work(part, n_parts):
        for src, dst in zip(cast_src, cast_dst):
            rows = src.shape[0] // n_parts
            dst[part * rows:(part + 1) * rows, :] = src[part * rows:(part + 1) * rows, :].astype(
                dst.dtype)

    _attn_body(q_ref, k_ref, v_ref, lq1_ref, lk1_ref, lq2_ref, lk2_ref, sw_ref, o_ref,
               vt_ref, qt_ref, m_ref, l_ref, acc_ref, tuple(score_bufs), side_work,
               blk=blk, lam_init=lam_init)


def _attn_body(q_ref, k_ref, v_ref, lq1_ref, lk1_ref, lq2_ref, lk2_ref, sw_ref, o_ref,
               vt_ref, qt_ref, m_ref, l_ref, acc_ref, score_bufs, side_work, *, blk, lam_init):
    nblk = vt_ref.shape[0]
    nstrip = blk // _ATTN_STRIP
    assert nblk % 2 == 0

    for jb in range(nblk):
        vt_ref[jb] = v_ref[jb * blk:(jb + 1) * blk, :].T

    feat = lax.broadcasted_iota(jnp.int32, (HEAD_DIM, _ATTN_STRIP), 0)
    for idx in range(nblk * nstrip):
        qt = q_ref[idx * _ATTN_STRIP:(idx + 1) * _ATTN_STRIP, :].T
        zero = jnp.zeros_like(qt)
        qt_ref[0, idx] = jnp.where(feat < DIFF_DQK, qt, zero)
        qt_ref[1, idx] = jnp.where(feat >= DIFF_DQK, qt, zero)

    chains = [(comp, s) for comp in range(2) for s in range(nstrip)]

    def keys_needed(s, diagonal):
        return (s + 1) * _ATTN_STRIP if diagonal else blk

    def scores(item, s_ref, diagonal=False):
        i, j = item
        jrow = j * blk
        if not isinstance(jrow, int):
            jrow = pl.multiple_of(jrow, blk)
        for c, (comp, s) in enumerate(chains):
            nk = keys_needed(s, diagonal)
            s_ref[c, :nk, :] = jnp.dot(k_ref[pl.ds(jrow, nk), :], qt_ref[comp, i * nstrip + s],
                                       preferred_element_type=F32)

    def softmax_av(item, s_ref, diagonal):
        i, j = item
        vtb = vt_ref[j]
        for c, (comp, s) in enumerate(chains):
            nk = keys_needed(s, diagonal)
            st = s_ref[c, :nk, :]
            idx = i * nstrip + s
            if diagonal:
                kpos = lax.broadcasted_iota(jnp.int32, st.shape, 0)
                qpos = lax.broadcasted_iota(jnp.int32, st.shape, 1) + s * _ATTN_STRIP
                st = jnp.where(kpos <= qpos, st, -jnp.inf)
                m_new = jnp.max(st, axis=0, keepdims=True)
                p = jnp.exp2(st - m_new)
                l_ref[comp, idx] = jnp.sum(p, axis=0, keepdims=True)
                acc_ref[comp, idx] = jnp.dot(vtb[:, :nk], p.astype(BF16),
                                             preferred_element_type=F32)
            else:
                m_old = m_ref[comp, idx]
                m_new = jnp.maximum(m_old, jnp.max(st, axis=0, keepdims=True))
                alpha = jnp.exp2(m_old - m_new)
                p = jnp.exp2(st - m_new)
                l_ref[comp, idx] = alpha * l_ref[comp, idx] + jnp.sum(p, axis=0, keepdims=True)
                acc_ref[comp, idx] = alpha * acc_ref[comp, idx] + jnp.dot(
                    vtb, p.astype(BF16), preferred_element_type=F32)
            m_ref[comp, idx] = m_new

    def next_item(item):
        i, j = item
        wrap = j + 1 >= i
        return jnp.where(wrap, i + 1, i), jnp.where(wrap, 0, j + 1)

    nbuf = len(score_bufs)
    n_off = nblk * (nblk - 1) // 2
    assert n_off % nbuf == 0
    buf = lambda n: score_bufs[n % nbuf]

    n_parts = nblk // 2
    scores((0, 0), buf(0), True)
    for n in range(nblk):
        if n % 2 == 0:
            side_work(n // 2, n_parts)
        if n + 1 < nblk:
            scores((n + 1, n + 1), buf(n + 1), True)
        else:
            scores((1, 0), buf(n + 1))
        softmax_av((n, n), buf(n), True)

    def rotation(t, item):
        items = [item]
        for _ in range(nbuf):
            items.append(next_item(items[-1]))
        for n in range(nbuf):
            scores(items[n + 1], buf(nblk + n + 1))
            softmax_av(items[n], buf(nblk + n), False)
        return items[-1]
    item = lax.fori_loop(0, n_off // nbuf - 1, rotation, (jnp.int32(1), jnp.int32(0)))
    items = [item]
    for _ in range(nbuf - 1):
        items.append(next_item(items[-1]))
    for n in range(nbuf):
        if n + 1 < nbuf:
            scores(items[n + 1], buf(nblk + n + 1))
        softmax_av(items[n], buf(nblk + n), False)

    lam = (jnp.exp(jnp.sum(lq1_ref[...] * lk1_ref[...], axis=-1, keepdims=True))
           - jnp.exp(jnp.sum(lq2_ref[...] * lk2_ref[...], axis=-1, keepdims=True))
           + lam_init)
    sw = sw_ref[...] * (1.0 - lam_init)

    for idx in range(nblk * nstrip):
        yt = acc_ref[0, idx] / l_ref[0, idx] - lam * (acc_ref[1, idx] / l_ref[1, idx])
        yt = yt * lax.rsqrt(jnp.mean(yt * yt, axis=0, keepdims=True) + GN_EPS)
        rows = slice(idx * _ATTN_STRIP, (idx + 1) * _ATTN_STRIP)
        o_ref[rows, :] = (yt.T * sw).astype(o_ref.dtype)


def _diff_attention(proj, lq1, lk1, lq2, lk2, subln_w, cast_weights, *, batch, seq, lam_init,
                    blk=512):
    nh = N_DIFF_HEADS
    nblk = seq // blk
    nchain = 2 * blk // _ATTN_STRIP
    nstrips = seq // _ATTN_STRIP
    steps = batch * nh
    q_off, k_off, v_off = 4 * nh, 5 * nh, 6 * nh
    head_blk = lambda off: pl.BlockSpec((seq, HEAD_DIM), lambda s: (s // nh, off + s % nh))
    lam_spec = pl.BlockSpec((1, DIFF_DQK), lambda s: (0, 0))
    cast_specs = [pl.BlockSpec((w.shape[0] // steps, w.shape[1]), lambda s: (s, 0))
                  for w in cast_weights]
    outs = pl.pallas_call(
        functools.partial(_attn_kernel, blk=blk, lam_init=lam_init, n_cast=len(cast_weights)),
        out_shape=[jax.ShapeDtypeStruct((batch * seq, nh * HEAD_DIM), BF16)]
        + [jax.ShapeDtypeStruct(w.shape, BF16) for w in cast_weights],
        grid=(steps,),
        in_specs=[
            head_blk(q_off), head_blk(k_off), head_blk(v_off),
            lam_spec, lam_spec, lam_spec, lam_spec,
            pl.BlockSpec((1, HEAD_DIM), lambda s: (0, s % nh)),
        ] + cast_specs,
        out_specs=[pl.BlockSpec((seq, HEAD_DIM), lambda s: (s // nh, s % nh))] + cast_specs,
        scratch_shapes=[
            pltpu.VMEM((nblk, HEAD_DIM, blk), BF16),
            pltpu.VMEM((2, nstrips, HEAD_DIM, _ATTN_STRIP), BF16),
            pltpu.VMEM((2, nstrips, 1, _ATTN_STRIP), F32),
            pltpu.VMEM((2, nstrips, 1, _ATTN_STRIP), F32),
            pltpu.VMEM((2, nstrips, HEAD_DIM, _ATTN_STRIP), F32),
        ] + [pltpu.VMEM((nchain, blk, _ATTN_STRIP), F32)] * _ATTN_SCORE_BUFS,
        compiler_params=pltpu.CompilerParams(
            dimension_semantics=("arbitrary",), vmem_limit_bytes=V7X_VMEM_LIMIT_BYTES),
        name="diff_attn",
    )(proj, proj, proj, lq1, lk1, lq2, lk2, subln_w, *cast_weights)
    return outs[0], outs[1:]


_OUT_RC = 512


def _outproj_kernel(yr_ref, yd_ref, w_ref, x_ref, g_ref, x1_ref, *, bm, wr):
    for r in range(bm // _OUT_RC):
        rows = slice(r * _OUT_RC, (r + 1) * _OUT_RC)
        mixed = (jnp.dot(yr_ref[rows, :], w_ref[:wr, :], preferred_element_type=F32)
                 + jnp.dot(yd_ref[rows, :], w_ref[wr:, :], preferred_element_type=F32))
        x1_ref[rows, :] = x_ref[rows, :] + g_ref[0] * mixed


def _outproj(y_ret, y_dif, w_out_bf16, x2d, mod3, *, seq, bm=512, bn=2048):
    bt, d = x2d.shape
    wr = y_ret.shape[1]
    wd = y_dif.shape[1]
    tiles_per_seq = seq // bm
    return pl.pallas_call(
        functools.partial(_outproj_kernel, bm=bm, wr=wr),
        out_shape=jax.ShapeDtypeStruct((bt, d), F32),
        grid=(bt // bm, d // bn),
        in_specs=[
            pl.BlockSpec((bm, wr), lambda m, n: (m, 0)),
            pl.BlockSpec((bm, wd), lambda m, n: (m, 0)),
            pl.BlockSpec((wr + wd, bn), lambda m, n: (0, n)),
            pl.BlockSpec((bm, bn), lambda m, n: (m, n)),
            pl.BlockSpec((1, 1, bn), lambda m, n: ((m // tiles_per_seq) * 6 + 2, 0, n)),
        ],
        out_specs=pl.BlockSpec((bm, bn), lambda m, n: (m, n)),
        compiler_params=pltpu.CompilerParams(
            dimension_semantics=("arbitrary", "arbitrary"), vmem_limit_bytes=V7X_VMEM_LIMIT_BYTES),
        name="outproj",
    )(y_ret, y_dif, w_out_bf16, x2d, mod3)


_MLP_RC = 512
_MLP_NORM_ROWS = 256
_MLP_FINAL_ROWS = 16


def _mlp_kernel(x1_ref, sh_ref, sc_ref, nw_ref, w1_ref, w2_ref, g_ref, fw_ref, o_ref, h_ref,
                *, bm, final_norm):
    f = pl.program_id(1)
    nf = pl.num_programs(1)

    def ffn(first):
        for r in range(bm // _MLP_RC):
            rows = slice(r * _MLP_RC, (r + 1) * _MLP_RC)
            a = jnp.dot(h_ref[rows, :], w1_ref[...], preferred_element_type=F32)
            a = jnp.maximum(a, 0.0)
            a = (a * a).astype(BF16)
            part = jnp.dot(a, w2_ref[...], preferred_element_type=F32)
            if first:
                o_ref[rows, :] = part
            else:
                o_ref[rows, :] += part

    @pl.when(f == 0)
    def _():
        for r in range(bm // _MLP_NORM_ROWS):
            rows = slice(r * _MLP_NORM_ROWS, (r + 1) * _MLP_NORM_ROWS)
            x = x1_ref[rows, :]
            ms = jnp.mean(x * x, axis=-1, keepdims=True)
            y = x * lax.rsqrt(ms + NORM_EPS) * nw_ref[...]
            h_ref[rows, :] = (y * (1.0 + sc_ref[0]) + sh_ref[0]).astype(BF16)
        ffn(True)

    @pl.when(f > 0)
    def _():
        ffn(False)

    @pl.when(f == nf - 1)
    def _():
        for r in range(bm // _MLP_FINAL_ROWS):
            rows = slice(r * _MLP_FINAL_ROWS, (r + 1) * _MLP_FINAL_ROWS)
            x2 = x1_ref[rows, :] + g_ref[0] * o_ref[rows, :]
            if final_norm:
                ms = jnp.mean(x2 * x2, axis=-1, keepdims=True)
                x2 = x2 * lax.rsqrt(ms + NORM_EPS) * fw_ref[...]
            o_ref[rows, :] = x2


def _mlp(x1, w1_bf16, w2_bf16, mod3, norm_w, final_w, *, seq, final_norm, bm=1024, bf=512):
    bt, d = x1.shape
    dff = w1_bf16.shape[1]
    tiles_per_seq = seq // bm
    mod_spec = lambda idx: pl.BlockSpec(
        (1, 1, d), lambda m, f: ((m // tiles_per_seq) * 6 + idx, 0, 0))
    return pl.pallas_call(
        functools.partial(_mlp_kernel, bm=bm, final_norm=final_norm),
        out_shape=jax.ShapeDtypeStruct((bt, d), F32),
        grid=(bt // bm, dff // bf),
        in_specs=[
            pl.BlockSpec((bm, d), lambda m, f: (m, 0)),
            mod_spec(3), mod_spec(4),
            pl.BlockSpec((1, d), lambda m, f: (0, 0)),
            pl.BlockSpec((d, bf), lambda m, f: (0, f)),
            pl.BlockSpec((bf, d), lambda m, f: (f, 0)),
            mod_spec(5),
            pl.BlockSpec((1, d), lambda m, f: (0, 0)),
        ],
        out_specs=pl.BlockSpec((bm, d), lambda m, f: (m, 0)),
        scratch_shapes=[pltpu.VMEM((bm, d), BF16)],
        compiler_params=pltpu.CompilerParams(
            dimension_semantics=("arbitrary", "arbitrary"), vmem_limit_bytes=V7X_VMEM_LIMIT_BYTES),
        name="mlp",
    )(x1, mod3, mod3, norm_w, w1_bf16, w2_bf16, mod3, final_w)


def kernel(x, c, w_ada, b_ada, norm1_w, norm2_w, w_in, ret_gn_w, diff_lq1, diff_lk1, diff_lq2,
           diff_lk2, diff_subln_w, w_out, w_mlp1, w_mlp2, final_norm_w):
    batch, seq, d = x.shape
    depth = w_ada.shape[0]
    c_pad = jnp.pad(c, ((0, F32_SUBLANES - batch), (0, 0)))
    rot = _rotary_lane_constants()
    x2d = x.reshape(batch * seq, d)
    final_w = final_norm_w.reshape(1, d)

    for l in range(depth):
        lam_init = 0.8 - 0.6 * math.exp(-0.3 * l)
        mod, tabs = _ada(c_pad, w_ada[l], b_ada[l].reshape(1, -1), rot, seq=seq)
        mod3 = mod[:batch].reshape(batch * 6, 1, d)
        proj = _proj(x2d, mod3, norm1_w[l].reshape(1, d), w_in[l], tabs, seq=seq)
        y_ret = _retention(proj, ret_gn_w[l].reshape(1, -1), batch=batch, seq=seq)
        y_dif, (w_out_b, w1_b, w2_b) = _diff_attention(
            proj, diff_lq1[l].reshape(1, -1), diff_lk1[l].reshape(1, -1),
            diff_lq2[l].reshape(1, -1), diff_lk2[l].reshape(1, -1),
            diff_subln_w[l].reshape(1, -1), (w_out[l], w_mlp1[l], w_mlp2[l]),
            batch=batch, seq=seq, lam_init=lam_init)
        x1 = _outproj(y_ret, y_dif, w_out_b, x2d, mod3, seq=seq)
        x2d = _mlp(x1, w1_b, w2_b, mod3, norm2_w[l].reshape(1, d), final_w, seq=seq,
                   final_norm=(l == depth - 1))
    if depth == 0:
        raise ValueError("depth must be positive")
    return x2d.reshape(batch, seq, d)
```

```python
import functools
import math

import jax
import jax.numpy as jnp
from jax import lax
from jax.experimental import pallas as pl
from jax.experimental.pallas import tpu as pltpu

F32 = jnp.float32
BF16 = jnp.bfloat16

HEAD_DIM = 128
N_RET_HEADS = 8
N_DIFF_HEADS = 8
DIFF_DQK = HEAD_DIM // 2
DIFF_ROT_DIM = DIFF_DQK // 4
RET_CHUNK = 128
ROPE_THETA = 500000.0
RET_THETA = 10000.0
NORM_EPS = 1e-6
GN_EPS = 1e-5

V7X_VMEM_LIMIT_BYTES = 58 * 1024 * 1024
F32_SUBLANES = 8


def _silu(v):
    return v * jax.nn.sigmoid(v)


def _ada_kernel(c_ref, w_ref, b_ref, rot_ref, o_ref, ret_c_ref, ret_s_ref, dif_c_ref,
                dif_a_ref, dif_b_ref, base_ref, *, tab_rows):
    s = _silu(c_ref[...]).astype(BF16)
    part = jnp.dot(s, w_ref[...].astype(BF16), preferred_element_type=F32)

    @pl.when(pl.program_id(0) == 0)
    def _():
        o_ref[...] = part + b_ref[...]

    @pl.when(pl.program_id(0) > 0)
    def _():
        o_ref[...] += part

    freq = (rot_ref[0:1, :], rot_ref[2:3, :])

    @pl.when(pl.program_id(0) == 0)
    def _():
        r = lax.broadcasted_iota(jnp.int32, (tab_rows, HEAD_DIM), 0).astype(F32)
        for k, w in enumerate(freq):
            base_ref[2 * k] = jnp.cos(r * w)
            base_ref[2 * k + 1] = jnp.sin(r * w)

    p0 = jnp.full((F32_SUBLANES, HEAD_DIM), pl.program_id(0) * tab_rows, jnp.int32).astype(F32)
    cos_sin = []
    for k, w in enumerate(freq):
        off_c = jnp.cos(p0 * w)[0:1, :]
        off_s = jnp.sin(p0 * w)[0:1, :]
        base_c, base_s = base_ref[2 * k], base_ref[2 * k + 1]
        cos_sin.append((off_c * base_c - off_s * base_s, off_s * base_c + off_c * base_s))
    (cos_ret, sin_ret), (cos_dif, sin_dif) = cos_sin
    ret_c_ref[...] = cos_ret
    ret_s_ref[...] = sin_ret * rot_ref[1:2, :]
    dif_c_ref[...] = cos_dif
    dif_a_ref[...] = sin_dif * rot_ref[3:4, :]
    dif_b_ref[...] = sin_dif * rot_ref[4:5, :]


def _ada(c_pad, w_ada, b_ada, rot, *, seq, bk=128):
    rows, d = c_pad.shape
    n = w_ada.shape[1]
    steps = d // bk
    tab_rows = seq // steps
    assert steps * bk == d and steps * tab_rows == seq and tab_rows % F32_SUBLANES == 0
    tab_spec = pl.BlockSpec((tab_rows, HEAD_DIM), lambda j: (j, 0))
    outs = pl.pallas_call(
        functools.partial(_ada_kernel, tab_rows=tab_rows),
        out_shape=[jax.ShapeDtypeStruct((rows, n), F32)]
        + [jax.ShapeDtypeStruct((seq, HEAD_DIM), F32)] * 5,
        grid=(steps,),
        in_specs=[
            pl.BlockSpec((rows, bk), lambda j: (0, j)),
            pl.BlockSpec((bk, n), lambda j: (j, 0)),
            pl.BlockSpec((1, n), lambda j: (0, 0)),
            pl.BlockSpec(rot.shape, lambda j: (0, 0)),
        ],
        out_specs=[pl.BlockSpec((rows, n), lambda j: (0, 0))] + [tab_spec] * 5,
        scratch_shapes=[pltpu.VMEM((4, tab_rows, HEAD_DIM), F32)],
        compiler_params=pltpu.CompilerParams(
            dimension_semantics=("arbitrary",), vmem_limit_bytes=V7X_VMEM_LIMIT_BYTES),
        name="ada",
    )(c_pad, w_ada, b_ada, rot)
    return outs[0], tuple(outs[1:])


_PROJ_RC = 512
_PROJ_CC = 256
_PROJ_NORM_ROWS = 256


def _proj_kernel(x_ref, sh_ref, sc_ref, nw_ref, w_ref, rc_ref, rs_ref, dc_ref, da_ref, db_ref,
                 o_ref, h_ref, wb_ref, *, bm, bn):
    j = pl.program_id(1)
    n_rc = bm // _PROJ_RC

    def norm_modulate():
        for r in range(bm // _PROJ_NORM_ROWS):
            rows = slice(r * _PROJ_NORM_ROWS, (r + 1) * _PROJ_NORM_ROWS)
            x = x_ref[rows, :]
            ms = jnp.mean(x * x, axis=-1, keepdims=True)
            y = x * lax.rsqrt(ms + NORM_EPS) * nw_ref[...]
            h_ref[rows, :] = (y * (1.0 + sc_ref[0]) + sh_ref[0]).astype(BF16)

    def run(epilogue):
        for cc in range(bn // _PROJ_CC):
            cols = slice(cc * _PROJ_CC, (cc + 1) * _PROJ_CC)
            wb_ref[:, cols] = w_ref[:, cols].astype(BF16)
            for r in range(n_rc):
                rows = slice(r * _PROJ_RC, (r + 1) * _PROJ_RC)
                acc = jnp.dot(h_ref[rows, :], wb_ref[:, cols], preferred_element_type=F32)
                for hb in range(_PROJ_CC // HEAD_DIM):
                    a = acc[:, hb * HEAD_DIM:(hb + 1) * HEAD_DIM]
                    c0 = cc * _PROJ_CC + hb * HEAD_DIM
                    o_ref[rows, c0:c0 + HEAD_DIM] = epilogue(a, rows).astype(o_ref.dtype)

    def ret_rotary(a, rows):
        return a * rc_ref[rows, :] + pltpu.roll(a, HEAD_DIM // 2, 1) * rs_ref[rows, :]

    @pl.when(j == 0)
    def _():
        norm_modulate()
        run(ret_rotary)

    @pl.when(j == 1)
    def _():
        run(lambda a, rows: ret_rotary(a, rows) * (HEAD_DIM ** -0.5))

    @pl.when(jnp.logical_or(j == 4, j == 5))
    def _():
        scale = jnp.where(j == 4, DIFF_DQK ** -0.5 * math.log2(math.e), 1.0).astype(F32)
        half = DIFF_ROT_DIM // 2

        def ep(a, rows):
            rot = (a * dc_ref[rows, :]
                   + pltpu.roll(a, HEAD_DIM - half, 1) * da_ref[rows, :]
                   + pltpu.roll(a, half, 1) * db_ref[rows, :])
            return rot * scale
        run(ep)

    @pl.when(jnp.logical_or(jnp.logical_or(j == 2, j == 3), j == 6))
    def _():
        run(lambda a, rows: a)


def _proj(x2d, mod3, norm_w, w_in, tabs, *, seq, bm=1024, bn=1024):
    bt, d = x2d.shape
    n = w_in.shape[1]
    tiles_per_seq = seq // bm
    n_m = bt // bm
    def tab_spec(last_use):
        return pl.BlockSpec((bm, HEAD_DIM), lambda m, j: (
            jnp.minimum(jnp.where(j > last_use, m + 1, m), n_m - 1) % tiles_per_seq, 0))
    return pl.pallas_call(
        functools.partial(_proj_kernel, bm=bm, bn=bn),
        out_shape=jax.ShapeDtypeStruct((bt, n), BF16),
        grid=(bt // bm, n // bn),
        in_specs=[
            pl.BlockSpec((bm, d), lambda m, j: (jnp.minimum(jnp.where(j > 0, m + 1, m), n_m - 1), 0)),
            pl.BlockSpec((1, 1, d), lambda m, j: ((m // tiles_per_seq) * 6 + 0, 0, 0)),
            pl.BlockSpec((1, 1, d), lambda m, j: ((m // tiles_per_seq) * 6 + 1, 0, 0)),
            pl.BlockSpec((1, d), lambda m, j: (0, 0)),
            pl.BlockSpec((d, bn), lambda m, j: (0, j)),
            tab_spec(1), tab_spec(1), tab_spec(5), tab_spec(5), tab_spec(5),
        ],
        out_specs=pl.BlockSpec((bm, bn), lambda m, j: (m, j)),
        scratch_shapes=[pltpu.VMEM((bm, d), BF16), pltpu.VMEM((d, bn), BF16)],
        compiler_params=pltpu.CompilerParams(
            dimension_semantics=("arbitrary", "arbitrary"), vmem_limit_bytes=V7X_VMEM_LIMIT_BYTES),
        name="proj",
    )(x2d, mod3, mod3, norm_w, w_in, *tabs)


def _rotary_lane_constants():
    lane = jnp.arange(HEAD_DIM)
    half = HEAD_DIM // 2
    inv_ret = RET_THETA ** (-(2 * (lane % half)).astype(F32) / HEAD_DIM)
    sign_ret = jnp.where(lane < half, -1.0, 1.0)
    half = DIFF_ROT_DIM // 2
    within = lane % DIFF_DQK
    inv_dif = ROPE_THETA ** (-(2 * (within % half)).astype(F32) / DIFF_ROT_DIM)
    inv_dif = jnp.where(within < DIFF_ROT_DIM, inv_dif, 0.0)
    sel_a = jnp.where(within < half, -1.0, 0.0)
    sel_b = jnp.where((within >= half) & (within < DIFF_ROT_DIM), 1.0, 0.0)
    rows = [inv_ret, sign_ret, inv_dif, sel_a, sel_b]
    rows += [jnp.zeros((HEAD_DIM,), F32)] * (F32_SUBLANES - len(rows))
    return jnp.stack(rows).astype(F32)


def _ret_kernel(q_ref, k_ref, v_ref, g_ref, gw_ref, o_ref, s_ref, *, seq):
    C = RET_CHUNK
    nc = seq // C
    head = (pl.program_id(0) % N_RET_HEADS).astype(F32)
    ii = lax.broadcasted_iota(jnp.int32, (C, C), 0).astype(F32)
    jj = lax.broadcasted_iota(jnp.int32, (C, C), 1).astype(F32)
    lg = jnp.log(1.0 - jnp.exp2(jnp.zeros((C, C), F32) - 5.0 - head))
    rel = ii - jj
    decay = jnp.where(rel >= 0, jnp.exp(jnp.maximum(rel, 0.0) * lg), 0.0)
    xi = jnp.exp((ii + 1.0) * lg)
    zeta = jnp.exp((C - 1.0 - ii) * lg)
    chunk_decay = jnp.exp(C * lg)

    def scan_body(c, state):
        rows = pl.ds(pl.multiple_of(c * C, C), C)
        s_ref[c] = state.astype(BF16)
        kz = (k_ref[rows, :].astype(F32) * zeta).astype(BF16)
        kv = lax.dot_general(kz, v_ref[rows, :], (((0,), (0,)), ((), ())),
                             preferred_element_type=F32)
        return state * chunk_decay + kv
    lax.fori_loop(0, nc, scan_body, jnp.zeros((C, C), F32), unroll=8)

    gw = gw_ref[...]

    def out_body(c, carry):
        rows = pl.ds(pl.multiple_of(c * C, C), C)
        q = q_ref[rows, :]
        inner = lax.dot_general(q, k_ref[rows, :], (((1,), (1,)), ((), ())),
                                preferred_element_type=F32) * decay
        o = jnp.dot(inner.astype(BF16), v_ref[rows, :], preferred_element_type=F32)
        o = o + jnp.dot(q, s_ref[c], preferred_element_type=F32) * xi
        mu = jnp.mean(o, axis=-1, keepdims=True)
        d = o - mu
        var = jnp.mean(d * d, axis=-1, keepdims=True)
        y = d * lax.rsqrt(var + GN_EPS) * gw
        o_ref[rows, :] = (y * _silu(g_ref[rows, :].astype(F32))).astype(o_ref.dtype)
        return carry
    lax.fori_loop(0, nc, out_body, 0, unroll=16)


def _retention(proj, gn_w, *, batch, seq):
    nh = N_RET_HEADS
    blk = lambda off: pl.BlockSpec((seq, HEAD_DIM), lambda s: (s // nh, off + s % nh))
    return pl.pallas_call(
        functools.partial(_ret_kernel, seq=seq),
        out_shape=jax.ShapeDtypeStruct((batch * seq, nh * HEAD_DIM), BF16),
        grid=(batch * nh,),
        in_specs=[blk(0), blk(nh), blk(2 * nh), blk(3 * nh),
                  pl.BlockSpec((1, HEAD_DIM), lambda s: (0, s % nh))],
        out_specs=pl.BlockSpec((seq, HEAD_DIM), lambda s: (s // nh, s % nh)),
        scratch_shapes=[pltpu.VMEM((seq // RET_CHUNK, RET_CHUNK, HEAD_DIM), BF16)],
        compiler_params=pltpu.CompilerParams(
            dimension_semantics=("arbitrary",), vmem_limit_bytes=V7X_VMEM_LIMIT_BYTES),
        name="retention",
    )(proj, proj, proj, proj, gn_w)


_ATTN_STRIP = 256
_ATTN_SCORE_BUFS = 4


_ATTN_N_IN = 8


def _attn_kernel(*refs, blk, lam_init, n_cast):
    q_ref, k_ref, v_ref, lq1_ref, lk1_ref, lq2_ref, lk2_ref, sw_ref = refs[:_ATTN_N_IN]
    cast_src = refs[_ATTN_N_IN:_ATTN_N_IN + n_cast]
    o_ref = refs[_ATTN_N_IN + n_cast]
    cast_dst = refs[_ATTN_N_IN + n_cast + 1:_ATTN_N_IN + 2 * n_cast + 1]
    vt_ref, qt_ref, m_ref, l_ref, acc_ref, *score_bufs = refs[_ATTN_N_IN + 2 * n_cast + 1:]

    def side_work(part, n_parts):
        for src, dst in zip(cast_src, cast_dst):
            rows = src.shape[0] // n_parts
            dst[part * rows:(part + 1) * rows, :] = src[part * rows:(part + 1) * rows, :].astype(
                dst.dtype)

    _attn_body(q_ref, k_ref, v_ref, lq1_ref, lk1_ref, lq2_ref, lk2_ref, sw_ref, o_ref,
               vt_ref, qt_ref, m_ref, l_ref, acc_ref, tuple(score_bufs), side_work,
               blk=blk, lam_init=lam_init)


def _attn_body(q_ref, k_ref, v_ref, lq1_ref, lk1_ref, lq2_ref, lk2_ref, sw_ref, o_ref,
               vt_ref, qt_ref, m_ref, l_ref, acc_ref, score_bufs, side_work, *, blk, lam_init):
    nblk = vt_ref.shape[0]
    nstrip = blk // _ATTN_STRIP
    assert nblk % 2 == 0

    for jb in range(nblk):
        vt_ref[jb] = v_ref[jb * blk:(jb + 1) * blk, :].T

    feat = lax.broadcasted_iota(jnp.int32, (HEAD_DIM, _ATTN_STRIP), 0)
    for idx in range(nblk * nstrip):
        qt = q_ref[idx * _ATTN_STRIP:(idx + 1) * _ATTN_STRIP, :].T
        zero = jnp.zeros_like(qt)
        qt_ref[0, idx] = jnp.where(feat < DIFF_DQK, qt, zero)
        qt_ref[1, idx] = jnp.where(feat >= DIFF_DQK, qt, zero)

    chains = [(comp, s) for comp in range(2) for s in range(nstrip)]

    def keys_needed(s, diagonal):
        return (s + 1) * _ATTN_STRIP if diagonal else blk

    def scores(item, s_ref, diagonal=False):
        i, j = item
        jrow = j * blk
        if not isinstance(jrow, int):
            jrow = pl.multiple_of(jrow, blk)
        for c, (comp, s) in enumerate(chains):
            nk = keys_needed(s, diagonal)
            s_ref[c, :nk, :] = jnp.dot(k_ref[pl.ds(jrow, nk), :], qt_ref[comp, i * nstrip + s],
                                       preferred_element_type=F32)

    def softmax_av(item, s_ref, diagonal):
        i, j = item
        vtb = vt_ref[j]
        for c, (comp, s) in enumerate(chains):
            nk = keys_needed(s, diagonal)
            st = s_ref[c, :nk, :]
            idx = i * nstrip + s
            if diagonal:
                kpos = lax.broadcasted_iota(jnp.int32, st.shape, 0)
                qpos = lax.broadcasted_iota(jnp.int32, st.shape, 1) + s * _ATTN_STRIP
                st = jnp.where(kpos <= qpos, st, -jnp.inf)
                m_new = jnp.max(st, axis=0, keepdims=True)
                p = jnp.exp2(st - m_new)
                l_ref[comp, idx] = jnp.sum(p, axis=0, keepdims=True)
                acc_ref[comp, idx] = jnp.dot(vtb[:, :nk], p.astype(BF16),
                                             preferred_element_type=F32)
            else:
                m_old = m_ref[comp, idx]
                m_new = jnp.maximum(m_old, jnp.max(st, axis=0, keepdims=True))
                alpha = jnp.exp2(m_old - m_new)
                p = jnp.exp2(st - m_new)
                l_ref[comp, idx] = alpha * l_ref[comp, idx] + jnp.sum(p, axis=0, keepdims=True)
                acc_ref[comp, idx] = alpha * acc_ref[comp, idx] + jnp.dot(
                    vtb, p.astype(BF16), preferred_element_type=F32)
            m_ref[comp, idx] = m_new

    def next_item(item):
        i, j = item
        wrap = j + 1 >= i
        return jnp.where(wrap, i + 1, i), jnp.where(wrap, 0, j + 1)

    nbuf = len(score_bufs)
    n_off = nblk * (nblk - 1) // 2
    assert n_off % nbuf == 0
    buf = lambda n: score_bufs[n % nbuf]

    n_parts = nblk // 2
    scores((0, 0), buf(0), True)
    for n in range(nblk):
        if n % 2 == 0:
            side_work(n // 2, n_parts)
        if n + 1 < nblk:
            scores((n + 1, n + 1), buf(n + 1), True)
        else:
            scores((1, 0), buf(n + 1))
        softmax_av((n, n), buf(n), True)

    def rotation(t, item):
        items = [item]
        for _ in range(nbuf):
            items.append(next_item(items[-1]))
        for n in range(nbuf):
            scores(items[n + 1], buf(nblk + n + 1))
            softmax_av(items[n], buf(nblk + n), False)
        return items[-1]
    item = lax.fori_loop(0, n_off // nbuf - 1, rotation, (jnp.int32(1), jnp.int32(0)))
    items = [item]
    for _ in range(nbuf - 1):
        items.append(next_item(items[-1]))
    for n in range(nbuf):
        if n + 1 < nbuf:
            scores(items[n + 1], buf(nblk + n + 1))
        softmax_av(items[n], buf(nblk + n), False)

    lam = (jnp.exp(jnp.sum(lq1_ref[...] * lk1_ref[...], axis=-1, keepdims=True))
           - jnp.exp(jnp.sum(lq2_ref[...] * lk2_ref[...], axis=-1, keepdims=True))
           + lam_init)
    sw = sw_ref[...] * (1.0 - lam_init)

    for idx in range(nblk * nstrip):
        yt = acc_ref[0, idx] / l_ref[0, idx] - lam * (acc_ref[1, idx] / l_ref[1, idx])
        yt = yt * lax.rsqrt(jnp.mean(yt * yt, axis=0, keepdims=True) + GN_EPS)
        rows = slice(idx * _ATTN_STRIP, (idx + 1) * _ATTN_STRIP)
        o_ref[rows, :] = (yt.T * sw).astype(o_ref.dtype)


def _diff_attention(proj, lq1, lk1, lq2, lk2, subln_w, cast_weights, *, batch, seq, lam_init,
                    blk=512):
    nh = N_DIFF_HEADS
    nblk = seq // blk
    nchain = 2 * blk // _ATTN_STRIP
    nstrips = seq // _ATTN_STRIP
    steps = batch * nh
    q_off, k_off, v_off = 4 * nh, 5 * nh, 6 * nh
    head_blk = lambda off: pl.BlockSpec((seq, HEAD_DIM), lambda s: (s // nh, off + s % nh))
    lam_spec = pl.BlockSpec((1, DIFF_DQK), lambda s: (0, 0))
    cast_specs = [pl.BlockSpec((w.shape[0] // steps, w.shape[1]), lambda s: (s, 0))
                  for w in cast_weights]
    outs = pl.pallas_call(
        functools.partial(_attn_kernel, blk=blk, lam_init=lam_init, n_cast=len(cast_weights)),
        out_shape=[jax.ShapeDtypeStruct((batch * seq, nh * HEAD_DIM), BF16)]
        + [jax.ShapeDtypeStruct(w.shape, BF16) for w in cast_weights],
        grid=(steps,),
        in_specs=[
            head_blk(q_off), head_blk(k_off), head_blk(v_off),
            lam_spec, lam_spec, lam_spec, lam_spec,
            pl.BlockSpec((1, HEAD_DIM), lambda s: (0, s % nh)),
        ] + cast_specs,
        out_specs=[pl.BlockSpec((seq, HEAD_DIM), lambda s: (s // nh, s % nh))] + cast_specs,
        scratch_shapes=[
            pltpu.VMEM((nblk, HEAD_DIM, blk), BF16),
            pltpu.VMEM((2, nstrips, HEAD_DIM, _ATTN_STRIP), BF16),
            pltpu.VMEM((2, nstrips, 1, _ATTN_STRIP), F32),
            pltpu.VMEM((2, nstrips, 1, _ATTN_STRIP), F32),
            pltpu.VMEM((2, nstrips, HEAD_DIM, _ATTN_STRIP), F32),
        ] + [pltpu.VMEM((nchain, blk, _ATTN_STRIP), F32)] * _ATTN_SCORE_BUFS,
        compiler_params=pltpu.CompilerParams(
            dimension_semantics=("arbitrary",), vmem_limit_bytes=V7X_VMEM_LIMIT_BYTES),
        name="diff_attn",
    )(proj, proj, proj, lq1, lk1, lq2, lk2, subln_w, *cast_weights)
    return outs[0], outs[1:]


_OUT_RC = 512


def _outproj_kernel(yr_ref, yd_ref, w_ref, x_ref, g_ref, x1_ref, *, bm, wr):
    for r in range(bm // _OUT_RC):
        rows = slice(r * _OUT_RC, (r + 1) * _OUT_RC)
        mixed = (jnp.dot(yr_ref[rows, :], w_ref[:wr, :], preferred_element_type=F32)
                 + jnp.dot(yd_ref[rows, :], w_ref[wr:, :], preferred_element_type=F32))
        x1_ref[rows, :] = x_ref[rows, :] + g_ref[0] * mixed


def _outproj(y_ret, y_dif, w_out_bf16, x2d, mod3, *, seq, bm=512, bn=2048):
    bt, d = x2d.shape
    wr = y_ret.shape[1]
    wd = y_dif.shape[1]
    tiles_per_seq = seq // bm
    return pl.pallas_call(
        functools.partial(_outproj_kernel, bm=bm, wr=wr),
        out_shape=jax.ShapeDtypeStruct((bt, d), F32),
        grid=(bt // bm, d // bn),
        in_specs=[
            pl.BlockSpec((bm, wr), lambda m, n: (m, 0)),
            pl.BlockSpec((bm, wd), lambda m, n: (m, 0)),
            pl.BlockSpec((wr + wd, bn), lambda m, n: (0, n)),
            pl.BlockSpec((bm, bn), lambda m, n: (m, n)),
            pl.BlockSpec((1, 1, bn), lambda m, n: ((m // tiles_per_seq) * 6 + 2, 0, n)),
        ],
        out_specs=pl.BlockSpec((bm, bn), lambda m, n: (m, n)),
        compiler_params=pltpu.CompilerParams(
            dimension_semantics=("arbitrary", "arbitrary"), vmem_limit_bytes=V7X_VMEM_LIMIT_BYTES),
        name="outproj",
    )(y_ret, y_dif, w_out_bf16, x2d, mod3)


_MLP_RC = 512
_MLP_NORM_ROWS = 256
_MLP_FINAL_ROWS = 16


def _mlp_kernel(x1_ref, sh_ref, sc_ref, nw_ref, w1_ref, w2_ref, g_ref, fw_ref, o_ref, h_ref,
                *, bm, final_norm):
    f = pl.program_id(1)
    nf = pl.num_programs(1)

    def finish_rows(row0, nrows):
        for r in range(row0 // _MLP_FINAL_ROWS, (row0 + nrows) // _MLP_FINAL_ROWS):
            rows = slice(r * _MLP_FINAL_ROWS, (r + 1) * _MLP_FINAL_ROWS)
            x2 = x1_ref[rows, :] + g_ref[0] * o_ref[rows, :]
            if final_norm:
                ms = jnp.mean(x2 * x2, axis=-1, keepdims=True)
                x2 = x2 * lax.rsqrt(ms + NORM_EPS) * fw_ref[...]
            o_ref[rows, :] = x2

    def ffn(first, last):
        for r in range(bm // _MLP_RC):
            rows = slice(r * _MLP_RC, (r + 1) * _MLP_RC)
            a = jnp.dot(h_ref[rows, :], w1_ref[...], preferred_element_type=F32)
            a = jnp.maximum(a, 0.0)
            a = (a * a).astype(BF16)
            part = jnp.dot(a, w2_ref[...], preferred_element_type=F32)
            if first:
                o_ref[rows, :] = part
            else:
                o_ref[rows, :] += part
            if last:
                finish_rows(r * _MLP_RC, _MLP_RC)

    @pl.when(f == 0)
    def _():
        for r in range(bm // _MLP_NORM_ROWS):
            rows = slice(r * _MLP_NORM_ROWS, (r + 1) * _MLP_NORM_ROWS)
            x = x1_ref[rows, :]
            ms = jnp.mean(x * x, axis=-1, keepdims=True)
            y = x * lax.rsqrt(ms + NORM_EPS) * nw_ref[...]
            h_ref[rows, :] = (y * (1.0 + sc_ref[0]) + sh_ref[0]).astype(BF16)
        ffn(True, False)

    @pl.when(jnp.logical_and(f > 0, f < nf - 1))
    def _():
        ffn(False, False)

    @pl.when(f == nf - 1)
    def _():
        ffn(False, True)


def _mlp(x1, w1_bf16, w2_bf16, mod3, norm_w, final_w, *, seq, final_norm, bm=1024, bf=512):
    bt, d = x1.shape
    dff = w1_bf16.shape[1]
    assert dff // bf >= 2
    tiles_per_seq = seq // bm
    mod_spec = lambda idx: pl.BlockSpec(
        (1, 1, d), lambda m, f: ((m // tiles_per_seq) * 6 + idx, 0, 0))
    return pl.pallas_call(
        functools.partial(_mlp_kernel, bm=bm, final_norm=final_norm),
        out_shape=jax.ShapeDtypeStruct((bt, d), F32),
        grid=(bt // bm, dff // bf),
        in_specs=[
            pl.BlockSpec((bm, d), lambda m, f: (m, 0)),
            mod_spec(3), mod_spec(4),
            pl.BlockSpec((1, d), lambda m, f: (0, 0)),
            pl.BlockSpec((d, bf), lambda m, f: (0, f)),
            pl.BlockSpec((bf, d), lambda m, f: (f, 0)),
            mod_spec(5),
            pl.BlockSpec((1, d), lambda m, f: (0, 0)),
        ],
        out_specs=pl.BlockSpec((bm, d), lambda m, f: (m, 0)),
        scratch_shapes=[pltpu.VMEM((bm, d), BF16)],
        compiler_params=pltpu.CompilerParams(
            dimension_semantics=("arbitrary", "arbitrary"), vmem_limit_bytes=V7X_VMEM_LIMIT_BYTES),
        name="mlp",
    )(x1, mod3, mod3, norm_w, w1_bf16, w2_bf16, mod3, final_w)


def kernel(x, c, w_ada, b_ada, norm1_w, norm2_w, w_in, ret_gn_w, diff_lq1, diff_lk1, diff_lq2,
           diff_lk2, diff_subln_w, w_out, w_mlp1, w_mlp2, final_norm_w):
    batch, seq, d = x.shape
    depth = w_ada.shape[0]
    c_pad = jnp.pad(c, ((0, F32_SUBLANES - batch), (0, 0)))
    rot = _rotary_lane_constants()
    x2d = x.reshape(batch * seq, d)
    final_w = final_norm_w.reshape(1, d)

    for l in range(depth):
        lam_init = 0.8 - 0.6 * math.exp(-0.3 * l)
        mod, tabs = _ada(c_pad, w_ada[l], b_ada[l].reshape(1, -1), rot, seq=seq)
        mod3 = mod[:batch].reshape(batch * 6, 1, d)
        proj = _proj(x2d, mod3, norm1_w[l].reshape(1, d), w_in[l], tabs, seq=seq)
        y_ret = _retention(proj, ret_gn_w[l].reshape(1, -1), batch=batch, seq=seq)
        y_dif, (w_out_b, w1_b, w2_b) = _diff_attention(
            proj, diff_lq1[l].reshape(1, -1), diff_lk1[l].reshape(1, -1),
            diff_lq2[l].reshape(1, -1), diff_lk2[l].reshape(1, -1),
            diff_subln_w[l].reshape(1, -1), (w_out[l], w_mlp1[l], w_mlp2[l]),
            batch=batch, seq=seq, lam_init=lam_init)
        x1 = _outproj(y_ret, y_dif, w_out_b, x2d, mod3, seq=seq)
        x2d = _mlp(x1, w1_b, w2_b, mod3, norm2_w[l].reshape(1, d), final_w, seq=seq,
                   final_norm=(l == depth - 1))
    if depth == 0:
        raise ValueError("depth must be positive")
    return x2d.reshape(batch, seq, d)
```

```python
import functools
import math

import jax
import jax.numpy as jnp
from jax import lax
from jax.experimental import pallas as pl
from jax.experimental.pallas import tpu as pltpu

F32 = jnp.float32
BF16 = jnp.bfloat16

HEAD_DIM = 128
N_RET_HEADS = 8
N_DIFF_HEADS = 8
DIFF_DQK = HEAD_DIM // 2
DIFF_ROT_DIM = DIFF_DQK // 4
RET_CHUNK = 128
ROPE_THETA = 500000.0
RET_THETA = 10000.0
NORM_EPS = 1e-6
GN_EPS = 1e-5

V7X_VMEM_LIMIT_BYTES = 58 * 1024 * 1024
F32_SUBLANES = 8


def _silu(v):
    return v * jax.nn.sigmoid(v)


def _ada_kernel(c_ref, w_ref, b_ref, rot_ref, o_ref, ret_c_ref, ret_s_ref, dif_c_ref,
                dif_a_ref, dif_b_ref, base_ref, *, tab_rows):
    s = _silu(c_ref[...]).astype(BF16)
    part = jnp.dot(s, w_ref[...].astype(BF16), preferred_element_type=F32)

    @pl.when(pl.program_id(0) == 0)
    def _():
        o_ref[...] = part + b_ref[...]

    @pl.when(pl.program_id(0) > 0)
    def _():
        o_ref[...] += part

    freq = (rot_ref[0:1, :], rot_ref[2:3, :])

    @pl.when(pl.program_id(0) == 0)
    def _():
        r = lax.broadcasted_iota(jnp.int32, (tab_rows, HEAD_DIM), 0).astype(F32)
        for k, w in enumerate(freq):
            base_ref[2 * k] = jnp.cos(r * w)
            base_ref[2 * k + 1] = jnp.sin(r * w)

    p0 = jnp.full((F32_SUBLANES, HEAD_DIM), pl.program_id(0) * tab_rows, jnp.int32).astype(F32)
    cos_sin = []
    for k, w in enumerate(freq):
        off_c = jnp.cos(p0 * w)[0:1, :]
        off_s = jnp.sin(p0 * w)[0:1, :]
        base_c, base_s = base_ref[2 * k], base_ref[2 * k + 1]
        cos_sin.append((off_c * base_c - off_s * base_s, off_s * base_c + off_c * base_s))
    (cos_ret, sin_ret), (cos_dif, sin_dif) = cos_sin
    ret_c_ref[...] = cos_ret
    ret_s_ref[...] = sin_ret * rot_ref[1:2, :]
    dif_c_ref[...] = cos_dif
    dif_a_ref[...] = sin_dif * rot_ref[3:4, :]
    dif_b_ref[...] = sin_dif * rot_ref[4:5, :]


def _ada(c_pad, w_ada, b_ada, rot, *, seq, bk=128):
    rows, d = c_pad.shape
    n = w_ada.shape[1]
    steps = d // bk
    tab_rows = seq // steps
    assert steps * bk == d and steps * tab_rows == seq and tab_rows % F32_SUBLANES == 0
    tab_spec = pl.BlockSpec((tab_rows, HEAD_DIM), lambda j: (j, 0))
    outs = pl.pallas_call(
        functools.partial(_ada_kernel, tab_rows=tab_rows),
        out_shape=[jax.ShapeDtypeStruct((rows, n), F32)]
        + [jax.ShapeDtypeStruct((seq, HEAD_DIM), F32)] * 5,
        grid=(steps,),
        in_specs=[
            pl.BlockSpec((rows, bk), lambda j: (0, j)),
            pl.BlockSpec((bk, n), lambda j: (j, 0)),
            pl.BlockSpec((1, n), lambda j: (0, 0)),
            pl.BlockSpec(rot.shape, lambda j: (0, 0)),
        ],
        out_specs=[pl.BlockSpec((rows, n), lambda j: (0, 0))] + [tab_spec] * 5,
        scratch_shapes=[pltpu.VMEM((4, tab_rows, HEAD_DIM), F32)],
        compiler_params=pltpu.CompilerParams(
            dimension_semantics=("arbitrary",), vmem_limit_bytes=V7X_VMEM_LIMIT_BYTES),
        name="ada",
    )(c_pad, w_ada, b_ada, rot)
    return outs[0], tuple(outs[1:])


_PROJ_RC = 512
_PROJ_CC = 256
_PROJ_NORM_ROWS = 256


def _proj_kernel(x_ref, sh_ref, sc_ref, nw_ref, w_ref, rc_ref, rs_ref, dc_ref, da_ref, db_ref,
                 o_ref, h_ref, wb_ref, *, bm, bn):
    j = pl.program_id(1)
    n_rc = bm // _PROJ_RC

    def norm_modulate():
        for r in range(bm // _PROJ_NORM_ROWS):
            rows = slice(r * _PROJ_NORM_ROWS, (r + 1) * _PROJ_NORM_ROWS)
            x = x_ref[rows, :]
            ms = jnp.mean(x * x, axis=-1, keepdims=True)
            y = x * lax.rsqrt(ms + NORM_EPS) * nw_ref[...]
            h_ref[rows, :] = (y * (1.0 + sc_ref[0]) + sh_ref[0]).astype(BF16)

    def run(epilogue):
        for cc in range(bn // _PROJ_CC):
            cols = slice(cc * _PROJ_CC, (cc + 1) * _PROJ_CC)
            wb_ref[:, cols] = w_ref[:, cols].astype(BF16)
            for r in range(n_rc):
                rows = slice(r * _PROJ_RC, (r + 1) * _PROJ_RC)
                acc = jnp.dot(h_ref[rows, :], wb_ref[:, cols], preferred_element_type=F32)
                for hb in range(_PROJ_CC // HEAD_DIM):
                    a = acc[:, hb * HEAD_DIM:(hb + 1) * HEAD_DIM]
                    c0 = cc * _PROJ_CC + hb * HEAD_DIM
                    o_ref[rows, c0:c0 + HEAD_DIM] = epilogue(a, rows).astype(o_ref.dtype)

    def ret_rotary(a, rows):
        return a * rc_ref[rows, :] + pltpu.roll(a, HEAD_DIM // 2, 1) * rs_ref[rows, :]

    @pl.when(j == 0)
    def _():
        norm_modulate()
        run(ret_rotary)

    @pl.when(j == 1)
    def _():
        run(lambda a, rows: ret_rotary(a, rows) * (HEAD_DIM ** -0.5))

    @pl.when(jnp.logical_or(j == 4, j == 5))
    def _():
        scale = jnp.where(j == 4, DIFF_DQK ** -0.5 * math.log2(math.e), 1.0).astype(F32)
        half = DIFF_ROT_DIM // 2

        def ep(a, rows):
            rot = (a * dc_ref[rows, :]
                   + pltpu.roll(a, HEAD_DIM - half, 1) * da_ref[rows, :]
                   + pltpu.roll(a, half, 1) * db_ref[rows, :])
            return rot * scale
        run(ep)

    @pl.when(jnp.logical_or(jnp.logical_or(j == 2, j == 3), j == 6))
    def _():
        run(lambda a, rows: a)


def _proj(x2d, mod3, norm_w, w_in, tabs, *, seq, bm=1024, bn=1024):
    bt, d = x2d.shape
    n = w_in.shape[1]
    tiles_per_seq = seq // bm
    n_m = bt // bm
    def tab_spec(last_use):
        return pl.BlockSpec((bm, HEAD_DIM), lambda m, j: (
            jnp.minimum(jnp.where(j > last_use, m + 1, m), n_m - 1) % tiles_per_seq, 0))
    return pl.pallas_call(
        functools.partial(_proj_kernel, bm=bm, bn=bn),
        out_shape=jax.ShapeDtypeStruct((bt, n), BF16),
        grid=(bt // bm, n // bn),
        in_specs=[
            pl.BlockSpec((bm, d), lambda m, j: (jnp.minimum(jnp.where(j > 0, m + 1, m), n_m - 1), 0)),
            pl.BlockSpec((1, 1, d), lambda m, j: ((m // tiles_per_seq) * 6 + 0, 0, 0)),
            pl.BlockSpec((1, 1, d), lambda m, j: ((m // tiles_per_seq) * 6 + 1, 0, 0)),
            pl.BlockSpec((1, d), lambda m, j: (0, 0)),
            pl.BlockSpec((d, bn), lambda m, j: (0, j)),
            tab_spec(1), tab_spec(1), tab_spec(5), tab_spec(5), tab_spec(5),
        ],
        out_specs=pl.BlockSpec((bm, bn), lambda m, j: (m, j)),
        scratch_shapes=[pltpu.VMEM((bm, d), BF16), pltpu.VMEM((d, bn), BF16)],
        compiler_params=pltpu.CompilerParams(
            dimension_semantics=("arbitrary", "arbitrary"), vmem_limit_bytes=V7X_VMEM_LIMIT_BYTES),
        name="proj",
    )(x2d, mod3, mod3, norm_w, w_in, *tabs)


def _rotary_lane_constants():
    lane = jnp.arange(HEAD_DIM)
    half = HEAD_DIM // 2
    inv_ret = RET_THETA ** (-(2 * (lane % half)).astype(F32) / HEAD_DIM)
    sign_ret = jnp.where(lane < half, -1.0, 1.0)
    half = DIFF_ROT_DIM // 2
    within = lane % DIFF_DQK
    inv_dif = ROPE_THETA ** (-(2 * (within % half)).astype(F32) / DIFF_ROT_DIM)
    inv_dif = jnp.where(within < DIFF_ROT_DIM, inv_dif, 0.0)
    sel_a = jnp.where(within < half, -1.0, 0.0)
    sel_b = jnp.where((within >= half) & (within < DIFF_ROT_DIM), 1.0, 0.0)
    rows = [inv_ret, sign_ret, inv_dif, sel_a, sel_b]
    rows += [jnp.zeros((HEAD_DIM,), F32)] * (F32_SUBLANES - len(rows))
    return jnp.stack(rows).astype(F32)


def _ret_kernel(q_ref, k_ref, v_ref, g_ref, gw_ref, o_ref, s_ref, const_ref, *, seq):
    C = RET_CHUNK
    nc = seq // C
    head = (pl.program_id(0) % N_RET_HEADS).astype(F32)
    ii = lax.broadcasted_iota(jnp.int32, (C, C), 0).astype(F32)
    jj = lax.broadcasted_iota(jnp.int32, (C, C), 1).astype(F32)
    lg = jnp.log(1.0 - jnp.exp2(jnp.zeros((C, C), F32) - 5.0 - head))
    rel = ii - jj
    DECAY, XI, ZETA, CHUNK_DECAY = range(4)
    const_ref[DECAY] = jnp.where(rel >= 0, jnp.exp(jnp.maximum(rel, 0.0) * lg), 0.0)
    const_ref[XI] = jnp.exp((ii + 1.0) * lg)
    const_ref[ZETA] = jnp.exp((C - 1.0 - ii) * lg)
    const_ref[CHUNK_DECAY] = jnp.exp(C * lg)

    def scan_body(c, state):
        rows = pl.ds(pl.multiple_of(c * C, C), C)
        s_ref[c] = state.astype(BF16)
        kz = (k_ref[rows, :].astype(F32) * const_ref[ZETA]).astype(BF16)
        kv = lax.dot_general(kz, v_ref[rows, :], (((0,), (0,)), ((), ())),
                             preferred_element_type=F32)
        return state * const_ref[CHUNK_DECAY] + kv
    lax.fori_loop(0, nc, scan_body, jnp.zeros((C, C), F32), unroll=32)

    gw = gw_ref[...]

    def out_body(c, carry):
        rows = pl.ds(pl.multiple_of(c * C, C), C)
        q = q_ref[rows, :]
        inner = lax.dot_general(q, k_ref[rows, :], (((1,), (1,)), ((), ())),
                                preferred_element_type=F32) * const_ref[DECAY]
        o = jnp.dot(inner.astype(BF16), v_ref[rows, :], preferred_element_type=F32)
        o = o + jnp.dot(q, s_ref[c], preferred_element_type=F32) * const_ref[XI]
        mu = jnp.mean(o, axis=-1, keepdims=True)
        d = o - mu
        var = jnp.mean(d * d, axis=-1, keepdims=True)
        y = d * lax.rsqrt(var + GN_EPS) * gw
        o_ref[rows, :] = (y * _silu(g_ref[rows, :].astype(F32))).astype(o_ref.dtype)
        return carry
    lax.fori_loop(0, nc, out_body, 0, unroll=32)


def _retention(proj, gn_w, *, batch, seq):
    nh = N_RET_HEADS
    blk = lambda off: pl.BlockSpec((seq, HEAD_DIM), lambda s: (s // nh, off + s % nh))
    return pl.pallas_call(
        functools.partial(_ret_kernel, seq=seq),
        out_shape=jax.ShapeDtypeStruct((batch * seq, nh * HEAD_DIM), BF16),
        grid=(batch * nh,),
        in_specs=[blk(0), blk(nh), blk(2 * nh), blk(3 * nh),
                  pl.BlockSpec((1, HEAD_DIM), lambda s: (0, s % nh))],
        out_specs=pl.BlockSpec((seq, HEAD_DIM), lambda s: (s // nh, s % nh)),
        scratch_shapes=[pltpu.VMEM((seq // RET_CHUNK, RET_CHUNK, HEAD_DIM), BF16),
                        pltpu.VMEM((4, RET_CHUNK, RET_CHUNK), F32)],
        compiler_params=pltpu.CompilerParams(
            dimension_semantics=("arbitrary",), vmem_limit_bytes=V7X_VMEM_LIMIT_BYTES),
        name="retention",
    )(proj, proj, proj, proj, gn_w)


_ATTN_STRIP = 256
_ATTN_SCORE_BUFS = 4


_ATTN_N_IN = 8


def _attn_kernel(*refs, blk, lam_init, n_cast):
    q_ref, k_ref, v_ref, lq1_ref, lk1_ref, lq2_ref, lk2_ref, sw_ref = refs[:_ATTN_N_IN]
    cast_src = refs[_ATTN_N_IN:_ATTN_N_IN + n_cast]
    o_ref = refs[_ATTN_N_IN + n_cast]
    cast_dst = refs[_ATTN_N_IN + n_cast + 1:_ATTN_N_IN + 2 * n_cast + 1]
    vt_ref, qt_ref, m_ref, l_ref, acc_ref, *score_bufs = refs[_ATTN_N_IN + 2 * n_cast + 1:]

    def side_work(part, n_parts):
        for src, dst in zip(cast_src, cast_dst):
            rows = src.shape[0] // n_parts
            dst[part * rows:(part + 1) * rows, :] = src[part * rows:(part + 1) * rows, :].astype(
                dst.dtype)

    _attn_body(q_ref, k_ref, v_ref, lq1_ref, lk1_ref, lq2_ref, lk2_ref, sw_ref, o_ref,
               vt_ref, qt_ref, m_ref, l_ref, acc_ref, tuple(score_bufs), side_work,
               blk=blk, lam_init=lam_init)


def _attn_body(q_ref, k_ref, v_ref, lq1_ref, lk1_ref, lq2_ref, lk2_ref, sw_ref, o_ref,
               vt_ref, qt_ref, m_ref, l_ref, acc_ref, score_bufs, side_work, *, blk, lam_init):
    nblk = vt_ref.shape[0]
    nstrip = blk // _ATTN_STRIP
    assert nblk % 2 == 0

    for jb in range(nblk):
        vt_ref[jb] = v_ref[jb * blk:(jb + 1) * blk, :].T

    feat = lax.broadcasted_iota(jnp.int32, (HEAD_DIM, _ATTN_STRIP), 0)
    for idx in range(nblk * nstrip):
        qt = q_ref[idx * _ATTN_STRIP:(idx + 1) * _ATTN_STRIP, :].T
        zero = jnp.zeros_like(qt)
        qt_ref[0, idx] = jnp.where(feat < DIFF_DQK, qt, zero)
        qt_ref[1, idx] = jnp.where(feat >= DIFF_DQK, qt, zero)

    chains = [(comp, s) for comp in range(2) for s in range(nstrip)]

    def keys_needed(s, diagonal):
        return (s + 1) * _ATTN_STRIP if diagonal else blk

    def scores(item, s_ref, diagonal=False):
        i, j = item
        jrow = j * blk
        if not isinstance(jrow, int):
            jrow = pl.multiple_of(jrow, blk)
        for c, (comp, s) in enumerate(chains):
            nk = keys_needed(s, diagonal)
            s_ref[c, :nk, :] = jnp.dot(k_ref[pl.ds(jrow, nk), :], qt_ref[comp, i * nstrip + s],
                                       preferred_element_type=F32)

    def softmax_av(item, s_ref, diagonal):
        i, j = item
        vtb = vt_ref[j]
        for c, (comp, s) in enumerate(chains):
            nk = keys_needed(s, diagonal)
            st = s_ref[c, :nk, :]
            idx = i * nstrip + s
            if diagonal:
                kpos = lax.broadcasted_iota(jnp.int32, st.shape, 0)
                qpos = lax.broadcasted_iota(jnp.int32, st.shape, 1) + s * _ATTN_STRIP
                st = jnp.where(kpos <= qpos, st, -jnp.inf)
                m_new = jnp.max(st, axis=0, keepdims=True)
                p = jnp.exp2(st - m_new)
                l_ref[comp, idx] = jnp.sum(p, axis=0, keepdims=True)
                acc_ref[comp, idx] = jnp.dot(vtb[:, :nk], p.astype(BF16),
                                             preferred_element_type=F32)
            else:
                m_old = m_ref[comp, idx]
                m_new = jnp.maximum(m_old, jnp.max(st, axis=0, keepdims=True))
                alpha = jnp.exp2(m_old - m_new)
                p = jnp.exp2(st - m_new)
                l_ref[comp, idx] = alpha * l_ref[comp, idx] + jnp.sum(p, axis=0, keepdims=True)
                acc_ref[comp, idx] = alpha * acc_ref[comp, idx] + jnp.dot(
                    vtb, p.astype(BF16), preferred_element_type=F32)
            m_ref[comp, idx] = m_new

    def next_item(item):
        i, j = item
        wrap = j + 1 >= i
        return jnp.where(wrap, i + 1, i), jnp.where(wrap, 0, j + 1)

    nbuf = len(score_bufs)
    n_off = nblk * (nblk - 1) // 2
    assert n_off % nbuf == 0
    buf = lambda n: score_bufs[n % nbuf]

    n_parts = nblk // 2
    scores((0, 0), buf(0), True)
    for n in range(nblk):
        if n % 2 == 0:
            side_work(n // 2, n_parts)
        if n + 1 < nblk:
            scores((n + 1, n + 1), buf(n + 1), True)
        else:
            scores((1, 0), buf(n + 1))
        softmax_av((n, n), buf(n), True)

    def rotation(t, item):
        items = [item]
        for _ in range(nbuf):
            items.append(next_item(items[-1]))
        for n in range(nbuf):
            scores(items[n + 1], buf(nblk + n + 1))
            softmax_av(items[n], buf(nblk + n), False)
        return items[-1]
    item = lax.fori_loop(0, n_off // nbuf - 1, rotation, (jnp.int32(1), jnp.int32(0)))
    items = [item]
    for _ in range(nbuf - 1):
        items.append(next_item(items[-1]))
    for n in range(nbuf):
        if n + 1 < nbuf:
            scores(items[n + 1], buf(nblk + n + 1))
        softmax_av(items[n], buf(nblk + n), False)

    lam = (jnp.exp(jnp.sum(lq1_ref[...] * lk1_ref[...], axis=-1, keepdims=True))
           - jnp.exp(jnp.sum(lq2_ref[...] * lk2_ref[...], axis=-1, keepdims=True))
           + lam_init)
    sw = sw_ref[...] * (1.0 - lam_init)

    for idx in range(nblk * nstrip):
        yt = acc_ref[0, idx] / l_ref[0, idx] - lam * (acc_ref[1, idx] / l_ref[1, idx])
        yt = yt * lax.rsqrt(jnp.mean(yt * yt, axis=0, keepdims=True) + GN_EPS)
        rows = slice(idx * _ATTN_STRIP, (idx + 1) * _ATTN_STRIP)
        o_ref[rows, :] = (yt.T * sw).astype(o_ref.dtype)


def _diff_attention(proj, lq1, lk1, lq2, lk2, subln_w, cast_weights, *, batch, seq, lam_init,
                    blk=512):
    nh = N_DIFF_HEADS
    nblk = seq // blk
    nchain = 2 * blk // _ATTN_STRIP
    nstrips = seq // _ATTN_STRIP
    steps = batch * nh
    q_off, k_off, v_off = 4 * nh, 5 * nh, 6 * nh
    head_blk = lambda off: pl.BlockSpec((seq, HEAD_DIM), lambda s: (s // nh, off + s % nh))
    lam_spec = pl.BlockSpec((1, DIFF_DQK), lambda s: (0, 0))
    cast_specs = [pl.BlockSpec((w.shape[0] // steps, w.shape[1]), lambda s: (s, 0))
                  for w in cast_weights]
    outs = pl.pallas_call(
        functools.partial(_attn_kernel, blk=blk, lam_init=lam_init, n_cast=len(cast_weights)),
        out_shape=[jax.ShapeDtypeStruct((batch * seq, nh * HEAD_DIM), BF16)]
        + [jax.ShapeDtypeStruct(w.shape, BF16) for w in cast_weights],
        grid=(steps,),
        in_specs=[
            head_blk(q_off), head_blk(k_off), head_blk(v_off),
            lam_spec, lam_spec, lam_spec, lam_spec,
            pl.BlockSpec((1, HEAD_DIM), lambda s: (0, s % nh)),
        ] + cast_specs,
        out_specs=[pl.BlockSpec((seq, HEAD_DIM), lambda s: (s // nh, s % nh))] + cast_specs,
        scratch_shapes=[
            pltpu.VMEM((nblk, HEAD_DIM, blk), BF16),
            pltpu.VMEM((2, nstrips, HEAD_DIM, _ATTN_STRIP), BF16),
            pltpu.VMEM((2, nstrips, 1, _ATTN_STRIP), F32),
            pltpu.VMEM((2, nstrips, 1, _ATTN_STRIP), F32),
            pltpu.VMEM((2, nstrips, HEAD_DIM, _ATTN_STRIP), F32),
        ] + [pltpu.VMEM((nchain, blk, _ATTN_STRIP), F32)] * _ATTN_SCORE_BUFS,
        compiler_params=pltpu.CompilerParams(
            dimension_semantics=("arbitrary",), vmem_limit_bytes=V7X_VMEM_LIMIT_BYTES),
        name="diff_attn",
    )(proj, proj, proj, lq1, lk1, lq2, lk2, subln_w, *cast_weights)
    return outs[0], outs[1:]


_OUT_RC = 512


def _outproj_kernel(yr_ref, yd_ref, w_ref, x_ref, g_ref, x1_ref, *, bm, wr):
    for r in range(bm // _OUT_RC):
        rows = slice(r * _OUT_RC, (r + 1) * _OUT_RC)
        mixed = (jnp.dot(yr_ref[rows, :], w_ref[:wr, :], preferred_element_type=F32)
                 + jnp.dot(yd_ref[rows, :], w_ref[wr:, :], preferred_element_type=F32))
        x1_ref[rows, :] = x_ref[rows, :] + g_ref[0] * mixed


def _outproj(y_ret, y_dif, w_out_bf16, x2d, mod3, *, seq, bm=512, bn=2048):
    bt, d = x2d.shape
    wr = y_ret.shape[1]
    wd = y_dif.shape[1]
    tiles_per_seq = seq // bm
    return pl.pallas_call(
        functools.partial(_outproj_kernel, bm=bm, wr=wr),
        out_shape=jax.ShapeDtypeStruct((bt, d), F32),
        grid=(bt // bm, d // bn),
        in_specs=[
            pl.BlockSpec((bm, wr), lambda m, n: (m, 0)),
            pl.BlockSpec((bm, wd), lambda m, n: (m, 0)),
            pl.BlockSpec((wr + wd, bn), lambda m, n: (0, n)),
            pl.BlockSpec((bm, bn), lambda m, n: (m, n)),
            pl.BlockSpec((1, 1, bn), lambda m, n: ((m // tiles_per_seq) * 6 + 2, 0, n)),
        ],
        out_specs=pl.BlockSpec((bm, bn), lambda m, n: (m, n)),
        compiler_params=pltpu.CompilerParams(
            dimension_semantics=("arbitrary", "arbitrary"), vmem_limit_bytes=V7X_VMEM_LIMIT_BYTES),
        name="outproj",
    )(y_ret, y_dif, w_out_bf16, x2d, mod3)


_MLP_RC = 512
_MLP_NORM_ROWS = 256
_MLP_FINAL_ROWS = 16


def _mlp_kernel(x1_ref, sh_ref, sc_ref, nw_ref, w1_ref, w2_ref, g_ref, fw_ref, o_ref, h_ref,
                *, bm, final_norm):
    f = pl.program_id(1)
    nf = pl.num_programs(1)

    def ffn(first):
        for r in range(bm // _MLP_RC):
            rows = slice(r * _MLP_RC, (r + 1) * _MLP_RC)
            a = jnp.dot(h_ref[rows, :], w1_ref[...], preferred_element_type=F32)
            a = jnp.maximum(a, 0.0)
            a = (a * a).astype(BF16)
            part = jnp.dot(a, w2_ref[...], preferred_element_type=F32)
            if first:
                o_ref[rows, :] = part
            else:
                o_ref[rows, :] += part

    @pl.when(f == 0)
    def _():
        for r in range(bm // _MLP_NORM_ROWS):
            rows = slice(r * _MLP_NORM_ROWS, (r + 1) * _MLP_NORM_ROWS)
            x = x1_ref[rows, :]
            ms = jnp.mean(x * x, axis=-1, keepdims=True)
            y = x * lax.rsqrt(ms + NORM_EPS) * nw_ref[...]
            h_ref[rows, :] = (y * (1.0 + sc_ref[0]) + sh_ref[0]).astype(BF16)
        ffn(True)

    @pl.when(f > 0)
    def _():
        ffn(False)

    @pl.when(f == nf - 1)
    def _():
        for r in range(bm // _MLP_FINAL_ROWS):
            rows = slice(r * _MLP_FINAL_ROWS, (r + 1) * _MLP_FINAL_ROWS)
            x2 = x1_ref[rows, :] + g_ref[0] * o_ref[rows, :]
            if final_norm:
                ms = jnp.mean(x2 * x2, axis=-1, keepdims=True)
                x2 = x2 * lax.rsqrt(ms + NORM_EPS) * fw_ref[...]
            o_ref[rows, :] = x2


def _mlp(x1, w1_bf16, w2_bf16, mod3, norm_w, final_w, *, seq, final_norm, bm=1024, bf=512):
    bt, d = x1.shape
    dff = w1_bf16.shape[1]
    tiles_per_seq = seq // bm
    mod_spec = lambda idx: pl.BlockSpec(
        (1, 1, d), lambda m, f: ((m // tiles_per_seq) * 6 + idx, 0, 0))
    return pl.pallas_call(
        functools.partial(_mlp_kernel, bm=bm, final_norm=final_norm),
        out_shape=jax.ShapeDtypeStruct((bt, d), F32),
        grid=(bt // bm, dff // bf),
        in_specs=[
            pl.BlockSpec((bm, d), lambda m, f: (m, 0)),
            mod_spec(3), mod_spec(4),
            pl.BlockSpec((1, d), lambda m, f: (0, 0)),
            pl.BlockSpec((d, bf), lambda m, f: (0, f)),
            pl.BlockSpec((bf, d), lambda m, f: (f, 0)),
            mod_spec(5),
            pl.BlockSpec((1, d), lambda m, f: (0, 0)),
        ],
        out_specs=pl.BlockSpec((bm, d), lambda m, f: (m, 0)),
        scratch_shapes=[pltpu.VMEM((bm, d), BF16)],
        compiler_params=pltpu.CompilerParams(
            dimension_semantics=("arbitrary", "arbitrary"), vmem_limit_bytes=V7X_VMEM_LIMIT_BYTES),
        name="mlp",
    )(x1, mod3, mod3, norm_w, w1_bf16, w2_bf16, mod3, final_w)


def kernel(x, c, w_ada, b_ada, norm1_w, norm2_w, w_in, ret_gn_w, diff_lq1, diff_lk1, diff_lq2,
           diff_lk2, diff_subln_w, w_out, w_mlp1, w_mlp2, final_norm_w):
    batch, seq, d = x.shape
    depth = w_ada.shape[0]
    c_pad = jnp.pad(c, ((0, F32_SUBLANES - batch), (0, 0)))
    rot = _rotary_lane_constants()
    x2d = x.reshape(batch * seq, d)
    final_w = final_norm_w.reshape(1, d)

    for l in range(depth):
        lam_init = 0.8 - 0.6 * math.exp(-0.3 * l)
        mod, tabs = _ada(c_pad, w_ada[l], b_ada[l].reshape(1, -1), rot, seq=seq)
        mod3 = mod[:batch].reshape(batch * 6, 1, d)
        proj = _proj(x2d, mod3, norm1_w[l].reshape(1, d), w_in[l], tabs, seq=seq)
        y_ret = _retention(proj, ret_gn_w[l].reshape(1, -1), batch=batch, seq=seq)
        y_dif, (w_out_b, w1_b, w2_b) = _diff_attention(
            proj, diff_lq1[l].reshape(1, -1), diff_lk1[l].reshape(1, -1),
            diff_lq2[l].reshape(1, -1), diff_lk2[l].reshape(1, -1),
            diff_subln_w[l].reshape(1, -1), (w_out[l], w_mlp1[l], w_mlp2[l]),
            batch=batch, seq=seq, lam_init=lam_init)
        x1 = _outproj(y_ret, y_dif, w_out_b, x2d, mod3, seq=seq)
        x2d = _mlp(x1, w1_b, w2_b, mod3, norm2_w[l].reshape(1, d), final_w, seq=seq,
                   final_norm=(l == depth - 1))
    if depth == 0:
        raise ValueError("depth must be positive")
    return x2d.reshape(batch, seq, d)
```

```python
import functools
import math

import jax
import jax.numpy as jnp
from jax import lax
from jax.experimental import pallas as pl
from jax.experimental.pallas import tpu as pltpu

F32 = jnp.float32
BF16 = jnp.bfloat16

HEAD_DIM = 128
N_RET_HEADS = 8
N_DIFF_HEADS = 8
DIFF_DQK = HEAD_DIM // 2
DIFF_ROT_DIM = DIFF_DQK // 4
RET_CHUNK = 128
ROPE_THETA = 500000.0
RET_THETA = 10000.0
NORM_EPS = 1e-6
GN_EPS = 1e-5

V7X_VMEM_LIMIT_BYTES = 58 * 1024 * 1024
V7X_VMEM_LIMIT_MLP_BYTES = 61 * 1024 * 1024
F32_SUBLANES = 8


def _silu(v):
    return v * jax.nn.sigmoid(v)


def _ada_kernel(c_ref, w_ref, b_ref, rot_ref, o_ref, ret_c_ref, ret_s_ref, dif_c_ref,
                dif_a_ref, dif_b_ref, base_ref, *, tab_rows):
    s = _silu(c_ref[...]).astype(BF16)
    part = jnp.dot(s, w_ref[...].astype(BF16), preferred_element_type=F32)

    @pl.when(pl.program_id(0) == 0)
    def _():
        o_ref[...] = part + b_ref[...]

    @pl.when(pl.program_id(0) > 0)
    def _():
        o_ref[...] += part

    freq = (rot_ref[0:1, :], rot_ref[2:3, :])

    @pl.when(pl.program_id(0) == 0)
    def _():
        r = lax.broadcasted_iota(jnp.int32, (tab_rows, HEAD_DIM), 0).astype(F32)
        for k, w in enumerate(freq):
            base_ref[2 * k] = jnp.cos(r * w)
            base_ref[2 * k + 1] = jnp.sin(r * w)

    p0 = jnp.full((F32_SUBLANES, HEAD_DIM), pl.program_id(0) * tab_rows, jnp.int32).astype(F32)
    cos_sin = []
    for k, w in enumerate(freq):
        off_c = jnp.cos(p0 * w)[0:1, :]
        off_s = jnp.sin(p0 * w)[0:1, :]
        base_c, base_s = base_ref[2 * k], base_ref[2 * k + 1]
        cos_sin.append((off_c * base_c - off_s * base_s, off_s * base_c + off_c * base_s))
    (cos_ret, sin_ret), (cos_dif, sin_dif) = cos_sin
    ret_c_ref[...] = cos_ret
    ret_s_ref[...] = sin_ret * rot_ref[1:2, :]
    dif_c_ref[...] = cos_dif
    dif_a_ref[...] = sin_dif * rot_ref[3:4, :]
    dif_b_ref[...] = sin_dif * rot_ref[4:5, :]


def _ada(c_pad, w_ada, b_ada, rot, *, seq, bk=128):
    rows, d = c_pad.shape
    n = w_ada.shape[1]
    steps = d // bk
    tab_rows = seq // steps
    assert steps * bk == d and steps * tab_rows == seq and tab_rows % F32_SUBLANES == 0
    tab_spec = pl.BlockSpec((tab_rows, HEAD_DIM), lambda j: (j, 0))
    outs = pl.pallas_call(
        functools.partial(_ada_kernel, tab_rows=tab_rows),
        out_shape=[jax.ShapeDtypeStruct((rows, n), F32)]
        + [jax.ShapeDtypeStruct((seq, HEAD_DIM), F32)] * 5,
        grid=(steps,),
        in_specs=[
            pl.BlockSpec((rows, bk), lambda j: (0, j)),
            pl.BlockSpec((bk, n), lambda j: (j, 0)),
            pl.BlockSpec((1, n), lambda j: (0, 0)),
            pl.BlockSpec(rot.shape, lambda j: (0, 0)),
        ],
        out_specs=[pl.BlockSpec((rows, n), lambda j: (0, 0))] + [tab_spec] * 5,
        scratch_shapes=[pltpu.VMEM((4, tab_rows, HEAD_DIM), F32)],
        compiler_params=pltpu.CompilerParams(
            dimension_semantics=("arbitrary",), vmem_limit_bytes=V7X_VMEM_LIMIT_BYTES),
        name="ada",
    )(c_pad, w_ada, b_ada, rot)
    return outs[0], tuple(outs[1:])


_PROJ_RC = 512
_PROJ_CC = 256
_PROJ_NORM_ROWS = 256


def _proj_kernel(x_ref, sh_ref, sc_ref, nw_ref, w_ref, rc_ref, rs_ref, dc_ref, da_ref, db_ref,
                 o_ref, h_ref, wb_ref, *, bm, bn):
    j = pl.program_id(1)
    n_rc = bm // _PROJ_RC

    def norm_modulate():
        for r in range(bm // _PROJ_NORM_ROWS):
            rows = slice(r * _PROJ_NORM_ROWS, (r + 1) * _PROJ_NORM_ROWS)
            x = x_ref[rows, :]
            ms = jnp.mean(x * x, axis=-1, keepdims=True)
            y = x * lax.rsqrt(ms + NORM_EPS) * nw_ref[...]
            h_ref[rows, :] = (y * (1.0 + sc_ref[0]) + sh_ref[0]).astype(BF16)

    def run(epilogue):
        for cc in range(bn // _PROJ_CC):
            cols = slice(cc * _PROJ_CC, (cc + 1) * _PROJ_CC)
            wb_ref[:, cols] = w_ref[:, cols].astype(BF16)
            for r in range(n_rc):
                rows = slice(r * _PROJ_RC, (r + 1) * _PROJ_RC)
                acc = jnp.dot(h_ref[rows, :], wb_ref[:, cols], preferred_element_type=F32)
                for hb in range(_PROJ_CC // HEAD_DIM):
                    a = acc[:, hb * HEAD_DIM:(hb + 1) * HEAD_DIM]
                    c0 = cc * _PROJ_CC + hb * HEAD_DIM
                    o_ref[rows, c0:c0 + HEAD_DIM] = epilogue(a, rows).astype(o_ref.dtype)

    def ret_rotary(a, rows):
        return a * rc_ref[rows, :] + pltpu.roll(a, HEAD_DIM // 2, 1) * rs_ref[rows, :]

    @pl.when(j == 0)
    def _():
        norm_modulate()
        run(ret_rotary)

    @pl.when(j == 1)
    def _():
        run(lambda a, rows: ret_rotary(a, rows) * (HEAD_DIM ** -0.5))

    @pl.when(jnp.logical_or(j == 4, j == 5))
    def _():
        scale = jnp.where(j == 4, DIFF_DQK ** -0.5 * math.log2(math.e), 1.0).astype(F32)
        half = DIFF_ROT_DIM // 2

        def ep(a, rows):
            rot = (a * dc_ref[rows, :]
                   + pltpu.roll(a, HEAD_DIM - half, 1) * da_ref[rows, :]
                   + pltpu.roll(a, half, 1) * db_ref[rows, :])
            return rot * scale
        run(ep)

    @pl.when(jnp.logical_or(jnp.logical_or(j == 2, j == 3), j == 6))
    def _():
        run(lambda a, rows: a)


def _proj(x2d, mod3, norm_w, w_in, tabs, *, seq, bm=1024, bn=1024):
    bt, d = x2d.shape
    n = w_in.shape[1]
    assert bn == N_RET_HEADS * HEAD_DIM == N_DIFF_HEADS * HEAD_DIM and n == 7 * bn
    tiles_per_seq = seq // bm
    n_m = bt // bm

    def tab_spec(last_use):
        return pl.BlockSpec((bm, HEAD_DIM), lambda m, j: (
            jnp.minimum(jnp.where(j > last_use, m + 1, m), n_m - 1) % tiles_per_seq, 0))
    return pl.pallas_call(
        functools.partial(_proj_kernel, bm=bm, bn=bn),
        out_shape=jax.ShapeDtypeStruct((bt, n), BF16),
        grid=(bt // bm, n // bn),
        in_specs=[
            pl.BlockSpec((bm, d), lambda m, j: (jnp.minimum(jnp.where(j > 0, m + 1, m), n_m - 1), 0)),
            pl.BlockSpec((1, 1, d), lambda m, j: ((m // tiles_per_seq) * 6 + 0, 0, 0)),
            pl.BlockSpec((1, 1, d), lambda m, j: ((m // tiles_per_seq) * 6 + 1, 0, 0)),
            pl.BlockSpec((1, d), lambda m, j: (0, 0)),
            pl.BlockSpec((d, bn), lambda m, j: (0, j)),
            tab_spec(1), tab_spec(1), tab_spec(5), tab_spec(5), tab_spec(5),
        ],
        out_specs=pl.BlockSpec((bm, bn), lambda m, j: (m, j)),
        scratch_shapes=[pltpu.VMEM((bm, d), BF16), pltpu.VMEM((d, bn), BF16)],
        compiler_params=pltpu.CompilerParams(
            dimension_semantics=("arbitrary", "arbitrary"), vmem_limit_bytes=V7X_VMEM_LIMIT_BYTES),
        name="proj",
    )(x2d, mod3, mod3, norm_w, w_in, *tabs)


def _rotary_lane_constants():
    lane = jnp.arange(HEAD_DIM)
    half = HEAD_DIM // 2
    inv_ret = RET_THETA ** (-(2 * (lane % half)).astype(F32) / HEAD_DIM)
    sign_ret = jnp.where(lane < half, -1.0, 1.0)
    half = DIFF_ROT_DIM // 2
    within = lane % DIFF_DQK
    inv_dif = ROPE_THETA ** (-(2 * (within % half)).astype(F32) / DIFF_ROT_DIM)
    inv_dif = jnp.where(within < DIFF_ROT_DIM, inv_dif, 0.0)
    sel_a = jnp.where(within < half, -1.0, 0.0)
    sel_b = jnp.where((within >= half) & (within < DIFF_ROT_DIM), 1.0, 0.0)
    rows = [inv_ret, sign_ret, inv_dif, sel_a, sel_b]
    rows += [jnp.zeros((HEAD_DIM,), F32)] * (F32_SUBLANES - len(rows))
    return jnp.stack(rows).astype(F32)


def _ret_kernel(q_ref, k_ref, v_ref, g_ref, gw_ref, o_ref, s_ref, const_ref, *, seq):
    C = RET_CHUNK
    nc = seq // C
    head = (pl.program_id(0) % N_RET_HEADS).astype(F32)
    ii = lax.broadcasted_iota(jnp.int32, (C, C), 0).astype(F32)
    jj = lax.broadcasted_iota(jnp.int32, (C, C), 1).astype(F32)
    lg = jnp.log(1.0 - jnp.exp2(jnp.zeros((C, C), F32) - 5.0 - head))
    rel = ii - jj
    DECAY, XI, ZETA, CHUNK_DECAY = range(4)
    const_ref[DECAY] = jnp.where(rel >= 0, jnp.exp(jnp.maximum(rel, 0.0) * lg), 0.0)
    const_ref[XI] = jnp.exp((ii + 1.0) * lg)
    const_ref[ZETA] = jnp.exp((C - 1.0 - ii) * lg)
    const_ref[CHUNK_DECAY] = jnp.exp(C * lg)

    def scan_body(c, state):
        rows = pl.ds(pl.multiple_of(c * C, C), C)
        s_ref[c] = state.astype(BF16)
        kz = (k_ref[rows, :].astype(F32) * const_ref[ZETA]).astype(BF16)
        kv = lax.dot_general(kz, v_ref[rows, :], (((0,), (0,)), ((), ())),
                             preferred_element_type=F32)
        return state * const_ref[CHUNK_DECAY] + kv
    lax.fori_loop(0, nc, scan_body, jnp.zeros((C, C), F32), unroll=32)

    gw = gw_ref[...]

    def out_body(c, carry):
        rows = pl.ds(pl.multiple_of(c * C, C), C)
        q = q_ref[rows, :]
        inner = lax.dot_general(q, k_ref[rows, :], (((1,), (1,)), ((), ())),
                                preferred_element_type=F32) * const_ref[DECAY]
        o = jnp.dot(inner.astype(BF16), v_ref[rows, :], preferred_element_type=F32)
        o = o + jnp.dot(q, s_ref[c], preferred_element_type=F32) * const_ref[XI]
        mu = jnp.mean(o, axis=-1, keepdims=True)
        d = o - mu
        var = jnp.mean(d * d, axis=-1, keepdims=True)
        y = d * lax.rsqrt(var + GN_EPS) * gw
        o_ref[rows, :] = (y * _silu(g_ref[rows, :].astype(F32))).astype(o_ref.dtype)
        return carry
    lax.fori_loop(0, nc, out_body, 0, unroll=32)


def _retention(proj, gn_w, *, batch, seq):
    nh = N_RET_HEADS
    blk = lambda off: pl.BlockSpec((seq, HEAD_DIM), lambda s: (s // nh, off + s % nh))
    return pl.pallas_call(
        functools.partial(_ret_kernel, seq=seq),
        out_shape=jax.ShapeDtypeStruct((batch * seq, nh * HEAD_DIM), BF16),
        grid=(batch * nh,),
        in_specs=[blk(0), blk(nh), blk(2 * nh), blk(3 * nh),
                  pl.BlockSpec((1, HEAD_DIM), lambda s: (0, s % nh))],
        out_specs=pl.BlockSpec((seq, HEAD_DIM), lambda s: (s // nh, s % nh)),
        scratch_shapes=[pltpu.VMEM((seq // RET_CHUNK, RET_CHUNK, HEAD_DIM), BF16),
                        pltpu.VMEM((4, RET_CHUNK, RET_CHUNK), F32)],
        compiler_params=pltpu.CompilerParams(
            dimension_semantics=("arbitrary",), vmem_limit_bytes=V7X_VMEM_LIMIT_BYTES),
        name="retention",
    )(proj, proj, proj, proj, gn_w)


_ATTN_STRIP = 256
_ATTN_SCORE_BUFS = 4


_ATTN_N_IN = 8


def _attn_kernel(*refs, blk, lam_init, n_cast):
    q_ref, k_ref, v_ref, lq1_ref, lk1_ref, lq2_ref, lk2_ref, sw_ref = refs[:_ATTN_N_IN]
    cast_src = refs[_ATTN_N_IN:_ATTN_N_IN + n_cast]
    o_ref = refs[_ATTN_N_IN + n_cast]
    cast_dst = refs[_ATTN_N_IN + n_cast + 1:_ATTN_N_IN + 2 * n_cast + 1]
    vt_ref, qt_ref, m_ref, l_ref, acc_ref, *score_bufs = refs[_ATTN_N_IN + 2 * n_cast + 1:]

    def side_work(part, n_parts):
        for src, dst in zip(cast_src, cast_dst):
            rows = src.shape[0] // n_parts
            dst[part * rows:(part + 1) * rows, :] = src[part * rows:(part + 1) * rows, :].astype(
                dst.dtype)

    _attn_body(q_ref, k_ref, v_ref, lq1_ref, lk1_ref, lq2_ref, lk2_ref, sw_ref, o_ref,
               vt_ref, qt_ref, m_ref, l_ref, acc_ref, tuple(score_bufs), side_work,
               blk=blk, lam_init=lam_init)


def _attn_body(q_ref, k_ref, v_ref, lq1_ref, lk1_ref, lq2_ref, lk2_ref, sw_ref, o_ref,
               vt_ref, qt_ref, m_ref, l_ref, acc_ref, score_bufs, side_work, *, blk, lam_init):
    nblk = vt_ref.shape[0]
    nstrip = blk // _ATTN_STRIP
    assert nblk % 2 == 0

    for jb in range(nblk):
        vt_ref[jb] = v_ref[jb * blk:(jb + 1) * blk, :].T

    feat = lax.broadcasted_iota(jnp.int32, (HEAD_DIM, _ATTN_STRIP), 0)
    for idx in range(nblk * nstrip):
        qt = q_ref[idx * _ATTN_STRIP:(idx + 1) * _ATTN_STRIP, :].T
        zero = jnp.zeros_like(qt)
        qt_ref[0, idx] = jnp.where(feat < DIFF_DQK, qt, zero)
        qt_ref[1, idx] = jnp.where(feat >= DIFF_DQK, qt, zero)

    chains = [(comp, s) for comp in range(2) for s in range(nstrip)]

    def keys_needed(s, diagonal):
        return (s + 1) * _ATTN_STRIP if diagonal else blk

    def scores(item, s_ref, diagonal=False):
        i, j = item
        jrow = j * blk
        if not isinstance(jrow, int):
            jrow = pl.multiple_of(jrow, blk)
        for c, (comp, s) in enumerate(chains):
            nk = keys_needed(s, diagonal)
            s_ref[c, :nk, :] = jnp.dot(k_ref[pl.ds(jrow, nk), :], qt_ref[comp, i * nstrip + s],
                                       preferred_element_type=F32)

    def softmax_av(item, s_ref, diagonal):
        i, j = item
        vtb = vt_ref[j]
        for c, (comp, s) in enumerate(chains):
            nk = keys_needed(s, diagonal)
            st = s_ref[c, :nk, :]
            idx = i * nstrip + s
            if diagonal:
                kpos = lax.broadcasted_iota(jnp.int32, st.shape, 0)
                qpos = lax.broadcasted_iota(jnp.int32, st.shape, 1) + s * _ATTN_STRIP
                st = jnp.where(kpos <= qpos, st, -jnp.inf)
                m_new = jnp.max(st, axis=0, keepdims=True)
                p = jnp.exp2(st - m_new)
                l_ref[comp, idx] = jnp.sum(p, axis=0, keepdims=True)
                acc_ref[comp, idx] = jnp.dot(vtb[:, :nk], p.astype(BF16),
                                             preferred_element_type=F32)
            else:
                m_old = m_ref[comp, idx]
                m_new = jnp.maximum(m_old, jnp.max(st, axis=0, keepdims=True))
                alpha = jnp.exp2(m_old - m_new)
                p = jnp.exp2(st - m_new)
                l_ref[comp, idx] = alpha * l_ref[comp, idx] + jnp.sum(p, axis=0, keepdims=True)
                acc_ref[comp, idx] = alpha * acc_ref[comp, idx] + jnp.dot(
                    vtb, p.astype(BF16), preferred_element_type=F32)
            m_ref[comp, idx] = m_new

    def next_item(item):
        i, j = item
        wrap = j + 1 >= i
        return jnp.where(wrap, i + 1, i), jnp.where(wrap, 0, j + 1)

    nbuf = len(score_bufs)
    n_off = nblk * (nblk - 1) // 2
    assert n_off % nbuf == 0
    buf = lambda n: score_bufs[n % nbuf]

    n_parts = nblk // 2
    scores((0, 0), buf(0), True)
    for n in range(nblk):
        if n % 2 == 0:
            side_work(n // 2, n_parts)
        if n + 1 < nblk:
            scores((n + 1, n + 1), buf(n + 1), True)
        else:
            scores((1, 0), buf(n + 1))
        softmax_av((n, n), buf(n), True)

    def rotation(t, item):
        items = [item]
        for _ in range(nbuf):
            items.append(next_item(items[-1]))
        for n in range(nbuf):
            scores(items[n + 1], buf(nblk + n + 1))
            softmax_av(items[n], buf(nblk + n), False)
        return items[-1]
    item = lax.fori_loop(0, n_off // nbuf - 1, rotation, (jnp.int32(1), jnp.int32(0)))
    items = [item]
    for _ in range(nbuf - 1):
        items.append(next_item(items[-1]))
    for n in range(nbuf):
        if n + 1 < nbuf:
            scores(items[n + 1], buf(nblk + n + 1))
        softmax_av(items[n], buf(nblk + n), False)

    lam = (jnp.exp(jnp.sum(lq1_ref[...] * lk1_ref[...], axis=-1, keepdims=True))
           - jnp.exp(jnp.sum(lq2_ref[...] * lk2_ref[...], axis=-1, keepdims=True))
           + lam_init)
    sw = sw_ref[...] * (1.0 - lam_init)

    for idx in range(nblk * nstrip):
        yt = acc_ref[0, idx] / l_ref[0, idx] - lam * (acc_ref[1, idx] / l_ref[1, idx])
        yt = yt * lax.rsqrt(jnp.mean(yt * yt, axis=0, keepdims=True) + GN_EPS)
        rows = slice(idx * _ATTN_STRIP, (idx + 1) * _ATTN_STRIP)
        o_ref[rows, :] = (yt.T * sw).astype(o_ref.dtype)


def _diff_attention(proj, lq1, lk1, lq2, lk2, subln_w, cast_weights, *, batch, seq, lam_init,
                    blk=512):
    nh = N_DIFF_HEADS
    nblk = seq // blk
    nchain = 2 * blk // _ATTN_STRIP
    nstrips = seq // _ATTN_STRIP
    steps = batch * nh
    q_off, k_off, v_off = 4 * nh, 5 * nh, 6 * nh
    head_blk = lambda off: pl.BlockSpec((seq, HEAD_DIM), lambda s: (s // nh, off + s % nh))
    lam_spec = pl.BlockSpec((1, DIFF_DQK), lambda s: (0, 0))
    cast_specs = [pl.BlockSpec((w.shape[0] // steps, w.shape[1]), lambda s: (s, 0))
                  for w in cast_weights]
    outs = pl.pallas_call(
        functools.partial(_attn_kernel, blk=blk, lam_init=lam_init, n_cast=len(cast_weights)),
        out_shape=[jax.ShapeDtypeStruct((batch * seq, nh * HEAD_DIM), BF16)]
        + [jax.ShapeDtypeStruct(w.shape, BF16) for w in cast_weights],
        grid=(steps,),
        in_specs=[
            head_blk(q_off), head_blk(k_off), head_blk(v_off),
            lam_spec, lam_spec, lam_spec, lam_spec,
            pl.BlockSpec((1, HEAD_DIM), lambda s: (0, s % nh)),
        ] + cast_specs,
        out_specs=[pl.BlockSpec((seq, HEAD_DIM), lambda s: (s // nh, s % nh))] + cast_specs,
        scratch_shapes=[
            pltpu.VMEM((nblk, HEAD_DIM, blk), BF16),
            pltpu.VMEM((2, nstrips, HEAD_DIM, _ATTN_STRIP), BF16),
            pltpu.VMEM((2, nstrips, 1, _ATTN_STRIP), F32),
            pltpu.VMEM((2, nstrips, 1, _ATTN_STRIP), F32),
            pltpu.VMEM((2, nstrips, HEAD_DIM, _ATTN_STRIP), F32),
        ] + [pltpu.VMEM((nchain, blk, _ATTN_STRIP), F32)] * _ATTN_SCORE_BUFS,
        compiler_params=pltpu.CompilerParams(
            dimension_semantics=("arbitrary",), vmem_limit_bytes=V7X_VMEM_LIMIT_BYTES),
        name="diff_attn",
    )(proj, proj, proj, lq1, lk1, lq2, lk2, subln_w, *cast_weights)
    return outs[0], outs[1:]


_OUT_RC = 512


def _outproj_kernel(yr_ref, yd_ref, w_ref, x_ref, g_ref, x1_ref, *, bm, wr):
    for r in range(bm // _OUT_RC):
        rows = slice(r * _OUT_RC, (r + 1) * _OUT_RC)
        mixed = (jnp.dot(yr_ref[rows, :], w_ref[:wr, :], preferred_element_type=F32)
                 + jnp.dot(yd_ref[rows, :], w_ref[wr:, :], preferred_element_type=F32))
        x1_ref[rows, :] = x_ref[rows, :] + g_ref[0] * mixed


def _outproj(y_ret, y_dif, w_out_bf16, x2d, mod3, *, seq, bm=512, bn=2048):
    bt, d = x2d.shape
    wr = y_ret.shape[1]
    wd = y_dif.shape[1]
    tiles_per_seq = seq // bm
    return pl.pallas_call(
        functools.partial(_outproj_kernel, bm=bm, wr=wr),
        out_shape=jax.ShapeDtypeStruct((bt, d), F32),
        grid=(bt // bm, d // bn),
        in_specs=[
            pl.BlockSpec((bm, wr), lambda m, n: (m, 0)),
            pl.BlockSpec((bm, wd), lambda m, n: (m, 0)),
            pl.BlockSpec((wr + wd, bn), lambda m, n: (0, n)),
            pl.BlockSpec((bm, bn), lambda m, n: (m, n)),
            pl.BlockSpec((1, 1, bn), lambda m, n: ((m // tiles_per_seq) * 6 + 2, 0, n)),
        ],
        out_specs=pl.BlockSpec((bm, bn), lambda m, n: (m, n)),
        compiler_params=pltpu.CompilerParams(
            dimension_semantics=("arbitrary", "arbitrary"), vmem_limit_bytes=V7X_VMEM_LIMIT_BYTES),
        name="outproj",
    )(y_ret, y_dif, w_out_bf16, x2d, mod3)


_MLP_RC = 512
_MLP_NORM_ROWS = 256
_MLP_FINAL_ROWS = 16


def _mlp_kernel(x1_ref, sh_ref, sc_ref, nw_ref, w1_ref, w2_ref, g_ref, fw_ref, o_ref, h_ref,
                *, bm, final_norm):
    f = pl.program_id(1)
    nf = pl.num_programs(1)

    def ffn(first):
        for r in range(bm // _MLP_RC):
            rows = slice(r * _MLP_RC, (r + 1) * _MLP_RC)
            a = jnp.dot(h_ref[rows, :], w1_ref[...], preferred_element_type=F32)
            a = jnp.maximum(a, 0.0)
            a = (a * a).astype(BF16)
            part = jnp.dot(a, w2_ref[...], preferred_element_type=F32)
            if first:
                o_ref[rows, :] = part
            else:
                o_ref[rows, :] += part

    @pl.when(f == 0)
    def _():
        for r in range(bm // _MLP_NORM_ROWS):
            rows = slice(r * _MLP_NORM_ROWS, (r + 1) * _MLP_NORM_ROWS)
            x = x1_ref[rows, :]
            ms = jnp.mean(x * x, axis=-1, keepdims=True)
            y = x * lax.rsqrt(ms + NORM_EPS) * nw_ref[...]
            h_ref[rows, :] = (y * (1.0 + sc_ref[0]) + sh_ref[0]).astype(BF16)
        ffn(True)

    @pl.when(f > 0)
    def _():
        ffn(False)

    @pl.when(f == nf - 1)
    def _():
        for r in range(bm // _MLP_FINAL_ROWS):
            rows = slice(r * _MLP_FINAL_ROWS, (r + 1) * _MLP_FINAL_ROWS)
            x2 = x1_ref[rows, :] + g_ref[0] * o_ref[rows, :]
            if final_norm:
                ms = jnp.mean(x2 * x2, axis=-1, keepdims=True)
                x2 = x2 * lax.rsqrt(ms + NORM_EPS) * fw_ref[...]
            o_ref[rows, :] = x2


def _mlp(x1, w1_bf16, w2_bf16, mod3, norm_w, final_w, *, seq, final_norm, bm=1024, bf=1024):
    bt, d = x1.shape
    dff = w1_bf16.shape[1]
    tiles_per_seq = seq // bm
    mod_spec = lambda idx: pl.BlockSpec(
        (1, 1, d), lambda m, f: ((m // tiles_per_seq) * 6 + idx, 0, 0))
    return pl.pallas_call(
        functools.partial(_mlp_kernel, bm=bm, final_norm=final_norm),
        out_shape=jax.ShapeDtypeStruct((bt, d), F32),
        grid=(bt // bm, dff // bf),
        in_specs=[
            pl.BlockSpec((bm, d), lambda m, f: (m, 0)),
            mod_spec(3), mod_spec(4),
            pl.BlockSpec((1, d), lambda m, f: (0, 0)),
            pl.BlockSpec((d, bf), lambda m, f: (0, f)),
            pl.BlockSpec((bf, d), lambda m, f: (f, 0)),
            mod_spec(5),
            pl.BlockSpec((1, d), lambda m, f: (0, 0)),
        ],
        out_specs=pl.BlockSpec((bm, d), lambda m, f: (m, 0)),
        scratch_shapes=[pltpu.VMEM((bm, d), BF16)],
        compiler_params=pltpu.CompilerParams(
            dimension_semantics=("arbitrary", "arbitrary"), vmem_limit_bytes=V7X_VMEM_LIMIT_MLP_BYTES),
        name="mlp",
    )(x1, mod3, mod3, norm_w, w1_bf16, w2_bf16, mod3, final_w)


def kernel(x, c, w_ada, b_ada, norm1_w, norm2_w, w_in, ret_gn_w, diff_lq1, diff_lk1, diff_lq2,
           diff_lk2, diff_subln_w, w_out, w_mlp1, w_mlp2, final_norm_w):
    batch, seq, d = x.shape
    depth = w_ada.shape[0]
    c_pad = jnp.pad(c, ((0, F32_SUBLANES - batch), (0, 0)))
    rot = _rotary_lane_constants()
    x2d = x.reshape(batch * seq, d)
    final_w = final_norm_w.reshape(1, d)

    for l in range(depth):
        lam_init = 0.8 - 0.6 * math.exp(-0.3 * l)
        mod, tabs = _ada(c_pad, w_ada[l], b_ada[l].reshape(1, -1), rot, seq=seq)
        mod3 = mod[:batch].reshape(batch * 6, 1, d)
        proj = _proj(x2d, mod3, norm1_w[l].reshape(1, d), w_in[l], tabs, seq=seq)
        y_ret = _retention(proj, ret_gn_w[l].reshape(1, -1), batch=batch, seq=seq)
        y_dif, (w_out_b, w1_b, w2_b) = _diff_attention(
            proj, diff_lq1[l].reshape(1, -1), diff_lk1[l].reshape(1, -1),
            diff_lq2[l].reshape(1, -1), diff_lk2[l].reshape(1, -1),
            diff_subln_w[l].reshape(1, -1), (w_out[l], w_mlp1[l], w_mlp2[l]),
            batch=batch, seq=seq, lam_init=lam_init)
        x1 = _outproj(y_ret, y_dif, w_out_b, x2d, mod3, seq=seq)
        x2d = _mlp(x1, w1_b, w2_b, mod3, norm2_w[l].reshape(1, d), final_w, seq=seq,
                   final_norm=(l == depth - 1))
    if depth == 0:
        raise ValueError("depth must be positive")
    return x2d.reshape(batch, seq, d)
```

```python
import functools
import math

import jax
import jax.numpy as jnp
from jax import lax
from jax.experimental import pallas as pl
from jax.experimental.pallas import tpu as pltpu

F32 = jnp.float32
BF16 = jnp.bfloat16

HEAD_DIM = 128
N_RET_HEADS = 8
N_DIFF_HEADS = 8
DIFF_DQK = HEAD_DIM // 2
DIFF_ROT_DIM = DIFF_DQK // 4
RET_CHUNK = 128
ROPE_THETA = 500000.0
RET_THETA = 10000.0
NORM_EPS = 1e-6
GN_EPS = 1e-5

V7X_VMEM_LIMIT_BYTES = 58 * 1024 * 1024
V7X_VMEM_LIMIT_MLP_BYTES = 61 * 1024 * 1024
F32_SUBLANES = 8


def _silu(v):
    return v * jax.nn.sigmoid(v)


def _ada_kernel(c_ref, w_ref, b_ref, rot_ref, o_ref, ret_c_ref, ret_s_ref, dif_c_ref,
                dif_a_ref, dif_b_ref, base_ref, *, tab_rows):
    s = _silu(c_ref[...]).astype(BF16)
    part = jnp.dot(s, w_ref[...].astype(BF16), preferred_element_type=F32)

    @pl.when(pl.program_id(0) == 0)
    def _():
        o_ref[...] = part + b_ref[...]

    @pl.when(pl.program_id(0) > 0)
    def _():
        o_ref[...] += part

    freq = (rot_ref[0:1, :], rot_ref[2:3, :])

    @pl.when(pl.program_id(0) == 0)
    def _():
        r = lax.broadcasted_iota(jnp.int32, (tab_rows, HEAD_DIM), 0).astype(F32)
        for k, w in enumerate(freq):
            base_ref[2 * k] = jnp.cos(r * w)
            base_ref[2 * k + 1] = jnp.sin(r * w)

    p0 = jnp.full((F32_SUBLANES, HEAD_DIM), pl.program_id(0) * tab_rows, jnp.int32).astype(F32)
    cos_sin = []
    for k, w in enumerate(freq):
        off_c = jnp.cos(p0 * w)[0:1, :]
        off_s = jnp.sin(p0 * w)[0:1, :]
        base_c, base_s = base_ref[2 * k], base_ref[2 * k + 1]
        cos_sin.append((off_c * base_c - off_s * base_s, off_s * base_c + off_c * base_s))
    (cos_ret, sin_ret), (cos_dif, sin_dif) = cos_sin
    ret_c_ref[...] = cos_ret
    ret_s_ref[...] = sin_ret * rot_ref[1:2, :]
    dif_c_ref[...] = cos_dif
    dif_a_ref[...] = sin_dif * rot_ref[3:4, :]
    dif_b_ref[...] = sin_dif * rot_ref[4:5, :]


def _ada(c_pad, w_ada, b_ada, rot, *, seq, bk=256):
    rows, d = c_pad.shape
    n = w_ada.shape[1]
    steps = d // bk
    tab_rows = seq // steps
    assert steps * bk == d and steps * tab_rows == seq and tab_rows % F32_SUBLANES == 0
    tab_spec = pl.BlockSpec((tab_rows, HEAD_DIM), lambda j: (j, 0))
    outs = pl.pallas_call(
        functools.partial(_ada_kernel, tab_rows=tab_rows),
        out_shape=[jax.ShapeDtypeStruct((rows, n), F32)]
        + [jax.ShapeDtypeStruct((seq, HEAD_DIM), F32)] * 5,
        grid=(steps,),
        in_specs=[
            pl.BlockSpec((rows, bk), lambda j: (0, j)),
            pl.BlockSpec((bk, n), lambda j: (j, 0)),
            pl.BlockSpec((1, n), lambda j: (0, 0)),
            pl.BlockSpec(rot.shape, lambda j: (0, 0)),
        ],
        out_specs=[pl.BlockSpec((rows, n), lambda j: (0, 0))] + [tab_spec] * 5,
        scratch_shapes=[pltpu.VMEM((4, tab_rows, HEAD_DIM), F32)],
        compiler_params=pltpu.CompilerParams(
            dimension_semantics=("arbitrary",), vmem_limit_bytes=V7X_VMEM_LIMIT_BYTES),
        name="ada",
    )(c_pad, w_ada, b_ada, rot)
    return outs[0], tuple(outs[1:])


_PROJ_RC = 512
_PROJ_CC = 256
_PROJ_NORM_ROWS = 256


def _proj_kernel(x_ref, sh_ref, sc_ref, nw_ref, w_ref, rc_ref, rs_ref, dc_ref, da_ref, db_ref,
                 o_ref, h_ref, wb_ref, *, bm, bn):
    j = pl.program_id(1)
    n_rc = bm // _PROJ_RC

    def norm_modulate():
        for r in range(bm // _PROJ_NORM_ROWS):
            rows = slice(r * _PROJ_NORM_ROWS, (r + 1) * _PROJ_NORM_ROWS)
            x = x_ref[rows, :]
            ms = jnp.mean(x * x, axis=-1, keepdims=True)
            y = x * lax.rsqrt(ms + NORM_EPS) * nw_ref[...]
            h_ref[rows, :] = (y * (1.0 + sc_ref[0]) + sh_ref[0]).astype(BF16)

    def run(epilogue):
        for cc in range(bn // _PROJ_CC):
            cols = slice(cc * _PROJ_CC, (cc + 1) * _PROJ_CC)
            wb_ref[:, cols] = w_ref[:, cols].astype(BF16)
            for r in range(n_rc):
                rows = slice(r * _PROJ_RC, (r + 1) * _PROJ_RC)
                acc = jnp.dot(h_ref[rows, :], wb_ref[:, cols], preferred_element_type=F32)
                for hb in range(_PROJ_CC // HEAD_DIM):
                    a = acc[:, hb * HEAD_DIM:(hb + 1) * HEAD_DIM]
                    c0 = cc * _PROJ_CC + hb * HEAD_DIM
                    o_ref[rows, c0:c0 + HEAD_DIM] = epilogue(a, rows).astype(o_ref.dtype)

    def ret_rotary(a, rows):
        return a * rc_ref[rows, :] + pltpu.roll(a, HEAD_DIM // 2, 1) * rs_ref[rows, :]

    @pl.when(j == 0)
    def _():
        norm_modulate()
        run(ret_rotary)

    @pl.when(j == 1)
    def _():
        run(lambda a, rows: ret_rotary(a, rows) * (HEAD_DIM ** -0.5))

    @pl.when(jnp.logical_or(j == 4, j == 5))
    def _():
        scale = jnp.where(j == 4, DIFF_DQK ** -0.5 * math.log2(math.e), 1.0).astype(F32)
        half = DIFF_ROT_DIM // 2

        def ep(a, rows):
            rot = (a * dc_ref[rows, :]
                   + pltpu.roll(a, HEAD_DIM - half, 1) * da_ref[rows, :]
                   + pltpu.roll(a, half, 1) * db_ref[rows, :])
            return rot * scale
        run(ep)

    @pl.when(jnp.logical_or(jnp.logical_or(j == 2, j == 3), j == 6))
    def _():
        run(lambda a, rows: a)


def _proj(x2d, mod3, norm_w, w_in, tabs, *, seq, bm=1024, bn=1024):
    bt, d = x2d.shape
    n = w_in.shape[1]
    assert bn == N_RET_HEADS * HEAD_DIM == N_DIFF_HEADS * HEAD_DIM and n == 7 * bn
    tiles_per_seq = seq // bm
    n_m = bt // bm

    def tab_spec(last_use):
        return pl.BlockSpec((bm, HEAD_DIM), lambda m, j: (
            jnp.minimum(jnp.where(j > last_use, m + 1, m), n_m - 1) % tiles_per_seq, 0))
    return pl.pallas_call(
        functools.partial(_proj_kernel, bm=bm, bn=bn),
        out_shape=jax.ShapeDtypeStruct((bt, n), BF16),
        grid=(bt // bm, n // bn),
        in_specs=[
            pl.BlockSpec((bm, d), lambda m, j: (jnp.minimum(jnp.where(j > 0, m + 1, m), n_m - 1), 0)),
            pl.BlockSpec((1, 1, d), lambda m, j: ((m // tiles_per_seq) * 6 + 0, 0, 0)),
            pl.BlockSpec((1, 1, d), lambda m, j: ((m // tiles_per_seq) * 6 + 1, 0, 0)),
            pl.BlockSpec((1, d), lambda m, j: (0, 0)),
            pl.BlockSpec((d, bn), lambda m, j: (0, j)),
            tab_spec(1), tab_spec(1), tab_spec(5), tab_spec(5), tab_spec(5),
        ],
        out_specs=pl.BlockSpec((bm, bn), lambda m, j: (m, j)),
        scratch_shapes=[pltpu.VMEM((bm, d), BF16), pltpu.VMEM((d, bn), BF16)],
        compiler_params=pltpu.CompilerParams(
            dimension_semantics=("arbitrary", "arbitrary"), vmem_limit_bytes=V7X_VMEM_LIMIT_BYTES),
        name="proj",
    )(x2d, mod3, mod3, norm_w, w_in, *tabs)


def _rotary_lane_constants():
    lane = jnp.arange(HEAD_DIM)
    half = HEAD_DIM // 2
    inv_ret = RET_THETA ** (-(2 * (lane % half)).astype(F32) / HEAD_DIM)
    sign_ret = jnp.where(lane < half, -1.0, 1.0)
    half = DIFF_ROT_DIM // 2
    within = lane % DIFF_DQK
    inv_dif = ROPE_THETA ** (-(2 * (within % half)).astype(F32) / DIFF_ROT_DIM)
    inv_dif = jnp.where(within < DIFF_ROT_DIM, inv_dif, 0.0)
    sel_a = jnp.where(within < half, -1.0, 0.0)
    sel_b = jnp.where((within >= half) & (within < DIFF_ROT_DIM), 1.0, 0.0)
    rows = [inv_ret, sign_ret, inv_dif, sel_a, sel_b]
    rows += [jnp.zeros((HEAD_DIM,), F32)] * (F32_SUBLANES - len(rows))
    return jnp.stack(rows).astype(F32)


def _ret_kernel(q_ref, k_ref, v_ref, g_ref, gw_ref, o_ref, s_ref, const_ref, *, seq):
    C = RET_CHUNK
    nc = seq // C
    head = (pl.program_id(0) % N_RET_HEADS).astype(F32)
    ii = lax.broadcasted_iota(jnp.int32, (C, C), 0).astype(F32)
    jj = lax.broadcasted_iota(jnp.int32, (C, C), 1).astype(F32)
    lg = jnp.log(1.0 - jnp.exp2(jnp.zeros((C, C), F32) - 5.0 - head))
    rel = ii - jj
    DECAY, XI, ZETA, CHUNK_DECAY = range(4)
    const_ref[DECAY] = jnp.where(rel >= 0, jnp.exp(jnp.maximum(rel, 0.0) * lg), 0.0)
    const_ref[XI] = jnp.exp((ii + 1.0) * lg)
    const_ref[ZETA] = jnp.exp((C - 1.0 - ii) * lg)
    const_ref[CHUNK_DECAY] = jnp.exp(C * lg)

    def scan_body(c, state):
        rows = pl.ds(pl.multiple_of(c * C, C), C)
        s_ref[c] = state.astype(BF16)
        kz = (k_ref[rows, :].astype(F32) * const_ref[ZETA]).astype(BF16)
        kv = lax.dot_general(kz, v_ref[rows, :], (((0,), (0,)), ((), ())),
                             preferred_element_type=F32)
        return state * const_ref[CHUNK_DECAY] + kv
    lax.fori_loop(0, nc, scan_body, jnp.zeros((C, C), F32), unroll=32)

    gw = gw_ref[...]

    def out_body(c, carry):
        rows = pl.ds(pl.multiple_of(c * C, C), C)
        q = q_ref[rows, :]
        inner = lax.dot_general(q, k_ref[rows, :], (((1,), (1,)), ((), ())),
                                preferred_element_type=F32) * const_ref[DECAY]
        o = jnp.dot(inner.astype(BF16), v_ref[rows, :], preferred_element_type=F32)
        o = o + jnp.dot(q, s_ref[c], preferred_element_type=F32) * const_ref[XI]
        mu = jnp.mean(o, axis=-1, keepdims=True)
        d = o - mu
        var = jnp.mean(d * d, axis=-1, keepdims=True)
        y = d * lax.rsqrt(var + GN_EPS) * gw
        o_ref[rows, :] = (y * _silu(g_ref[rows, :].astype(F32))).astype(o_ref.dtype)
        return carry
    lax.fori_loop(0, nc, out_body, 0, unroll=32)


def _retention(proj, gn_w, *, batch, seq):
    nh = N_RET_HEADS
    blk = lambda off: pl.BlockSpec((seq, HEAD_DIM), lambda s: (s // nh, off + s % nh))
    return pl.pallas_call(
        functools.partial(_ret_kernel, seq=seq),
        out_shape=jax.ShapeDtypeStruct((batch * seq, nh * HEAD_DIM), BF16),
        grid=(batch * nh,),
        in_specs=[blk(0), blk(nh), blk(2 * nh), blk(3 * nh),
                  pl.BlockSpec((1, HEAD_DIM), lambda s: (0, s % nh))],
        out_specs=pl.BlockSpec((seq, HEAD_DIM), lambda s: (s // nh, s % nh)),
        scratch_shapes=[pltpu.VMEM((seq // RET_CHUNK, RET_CHUNK, HEAD_DIM), BF16),
                        pltpu.VMEM((4, RET_CHUNK, RET_CHUNK), F32)],
        compiler_params=pltpu.CompilerParams(
            dimension_semantics=("arbitrary",), vmem_limit_bytes=V7X_VMEM_LIMIT_BYTES),
        name="retention",
    )(proj, proj, proj, proj, gn_w)


_ATTN_STRIP = 256
_ATTN_SCORE_BUFS = 4


_ATTN_N_IN = 8


def _attn_kernel(*refs, blk, lam_init, n_cast):
    q_ref, k_ref, v_ref, lq1_ref, lk1_ref, lq2_ref, lk2_ref, sw_ref = refs[:_ATTN_N_IN]
    cast_src = refs[_ATTN_N_IN:_ATTN_N_IN + n_cast]
    o_ref = refs[_ATTN_N_IN + n_cast]
    cast_dst = refs[_ATTN_N_IN + n_cast + 1:_ATTN_N_IN + 2 * n_cast + 1]
    vt_ref, qt_ref, m_ref, l_ref, acc_ref, *score_bufs = refs[_ATTN_N_IN + 2 * n_cast + 1:]

    def side_work(part, n_parts):
        for src, dst in zip(cast_src, cast_dst):
            rows = src.shape[0] // n_parts
            dst[part * rows:(part + 1) * rows, :] = src[part * rows:(part + 1) * rows, :].astype(
                dst.dtype)

    _attn_body(q_ref, k_ref, v_ref, lq1_ref, lk1_ref, lq2_ref, lk2_ref, sw_ref, o_ref,
               vt_ref, qt_ref, m_ref, l_ref, acc_ref, tuple(score_bufs), side_work,
               blk=blk, lam_init=lam_init)


def _attn_body(q_ref, k_ref, v_ref, lq1_ref, lk1_ref, lq2_ref, lk2_ref, sw_ref, o_ref,
               vt_ref, qt_ref, m_ref, l_ref, acc_ref, score_bufs, side_work, *, blk, lam_init):
    nblk = vt_ref.shape[0]
    nstrip = blk // _ATTN_STRIP
    assert nblk % 2 == 0

    for jb in range(nblk):
        vt_ref[jb] = v_ref[jb * blk:(jb + 1) * blk, :].T

    feat = lax.broadcasted_iota(jnp.int32, (HEAD_DIM, _ATTN_STRIP), 0)
    for idx in range(nblk * nstrip):
        qt = q_ref[idx * _ATTN_STRIP:(idx + 1) * _ATTN_STRIP, :].T
        zero = jnp.zeros_like(qt)
        qt_ref[0, idx] = jnp.where(feat < DIFF_DQK, qt, zero)
        qt_ref[1, idx] = jnp.where(feat >= DIFF_DQK, qt, zero)

    chains = [(comp, s) for comp in range(2) for s in range(nstrip)]

    def keys_needed(s, diagonal):
        return (s + 1) * _ATTN_STRIP if diagonal else blk

    def scores(item, s_ref, diagonal=False):
        i, j = item
        jrow = j * blk
        if not isinstance(jrow, int):
            jrow = pl.multiple_of(jrow, blk)
        for c, (comp, s) in enumerate(chains):
            nk = keys_needed(s, diagonal)
            s_ref[c, :nk, :] = jnp.dot(k_ref[pl.ds(jrow, nk), :], qt_ref[comp, i * nstrip + s],
                                       preferred_element_type=F32)

    def softmax_av(item, s_ref, diagonal):
        i, j = item
        vtb = vt_ref[j]
        for c, (comp, s) in enumerate(chains):
            nk = keys_needed(s, diagonal)
            st = s_ref[c, :nk, :]
            idx = i * nstrip + s
            if diagonal:
                kpos = lax.broadcasted_iota(jnp.int32, st.shape, 0)
                qpos = lax.broadcasted_iota(jnp.int32, st.shape, 1) + s * _ATTN_STRIP
                st = jnp.where(kpos <= qpos, st, -jnp.inf)
                m_new = jnp.max(st, axis=0, keepdims=True)
                p = jnp.exp2(st - m_new)
                l_ref[comp, idx] = jnp.sum(p, axis=0, keepdims=True)
                acc_ref[comp, idx] = jnp.dot(vtb[:, :nk], p.astype(BF16),
                                             preferred_element_type=F32)
            else:
                m_old = m_ref[comp, idx]
                m_new = jnp.maximum(m_old, jnp.max(st, axis=0, keepdims=True))
                alpha = jnp.exp2(m_old - m_new)
                p = jnp.exp2(st - m_new)
                l_ref[comp, idx] = alpha * l_ref[comp, idx] + jnp.sum(p, axis=0, keepdims=True)
                acc_ref[comp, idx] = alpha * acc_ref[comp, idx] + jnp.dot(
                    vtb, p.astype(BF16), preferred_element_type=F32)
            m_ref[comp, idx] = m_new

    def next_item(item):
        i, j = item
        wrap = j + 1 >= i
        return jnp.where(wrap, i + 1, i), jnp.where(wrap, 0, j + 1)

    nbuf = len(score_bufs)
    n_off = nblk * (nblk - 1) // 2
    assert n_off % nbuf == 0
    buf = lambda n: score_bufs[n % nbuf]

    n_parts = nblk // 2
    scores((0, 0), buf(0), True)
    for n in range(nblk):
        if n % 2 == 0:
            side_work(n // 2, n_parts)
        if n + 1 < nblk:
            scores((n + 1, n + 1), buf(n + 1), True)
        else:
            scores((1, 0), buf(n + 1))
        softmax_av((n, n), buf(n), True)

    def rotation(t, item):
        items = [item]
        for _ in range(nbuf):
            items.append(next_item(items[-1]))
        for n in range(nbuf):
            scores(items[n + 1], buf(nblk + n + 1))
            softmax_av(items[n], buf(nblk + n), False)
        return items[-1]
    item = lax.fori_loop(0, n_off // nbuf - 1, rotation, (jnp.int32(1), jnp.int32(0)))
    items = [item]
    for _ in range(nbuf - 1):
        items.append(next_item(items[-1]))
    for n in range(nbuf):
        if n + 1 < nbuf:
            scores(items[n + 1], buf(nblk + n + 1))
        softmax_av(items[n], buf(nblk + n), False)

    lam = (jnp.exp(jnp.sum(lq1_ref[...] * lk1_ref[...], axis=-1, keepdims=True))
           - jnp.exp(jnp.sum(lq2_ref[...] * lk2_ref[...], axis=-1, keepdims=True))
           + lam_init)
    sw = sw_ref[...] * (1.0 - lam_init)

    for idx in range(nblk * nstrip):
        yt = acc_ref[0, idx] / l_ref[0, idx] - lam * (acc_ref[1, idx] / l_ref[1, idx])
        yt = yt * lax.rsqrt(jnp.mean(yt * yt, axis=0, keepdims=True) + GN_EPS)
        rows = slice(idx * _ATTN_STRIP, (idx + 1) * _ATTN_STRIP)
        o_ref[rows, :] = (yt.T * sw).astype(o_ref.dtype)


def _diff_attention(proj, lq1, lk1, lq2, lk2, subln_w, cast_weights, *, batch, seq, lam_init,
                    blk=512):
    nh = N_DIFF_HEADS
    nblk = seq // blk
    nchain = 2 * blk // _ATTN_STRIP
    nstrips = seq // _ATTN_STRIP
    steps = batch * nh
    q_off, k_off, v_off = 4 * nh, 5 * nh, 6 * nh
    head_blk = lambda off: pl.BlockSpec((seq, HEAD_DIM), lambda s: (s // nh, off + s % nh))
    lam_spec = pl.BlockSpec((1, DIFF_DQK), lambda s: (0, 0))
    cast_specs = [pl.BlockSpec((w.shape[0] // steps, w.shape[1]), lambda s: (s, 0))
                  for w in cast_weights]
    outs = pl.pallas_call(
        functools.partial(_attn_kernel, blk=blk, lam_init=lam_init, n_cast=len(cast_weights)),
        out_shape=[jax.ShapeDtypeStruct((batch * seq, nh * HEAD_DIM), BF16)]
        + [jax.ShapeDtypeStruct(w.shape, BF16) for w in cast_weights],
        grid=(steps,),
        in_specs=[
            head_blk(q_off), head_blk(k_off), head_blk(v_off),
            lam_spec, lam_spec, lam_spec, lam_spec,
            pl.BlockSpec((1, HEAD_DIM), lambda s: (0, s % nh)),
        ] + cast_specs,
        out_specs=[pl.BlockSpec((seq, HEAD_DIM), lambda s: (s // nh, s % nh))] + cast_specs,
        scratch_shapes=[
            pltpu.VMEM((nblk, HEAD_DIM, blk), BF16),
            pltpu.VMEM((2, nstrips, HEAD_DIM, _ATTN_STRIP), BF16),
            pltpu.VMEM((2, nstrips, 1, _ATTN_STRIP), F32),
            pltpu.VMEM((2, nstrips, 1, _ATTN_STRIP), F32),
            pltpu.VMEM((2, nstrips, HEAD_DIM, _ATTN_STRIP), F32),
        ] + [pltpu.VMEM((nchain, blk, _ATTN_STRIP), F32)] * _ATTN_SCORE_BUFS,
        compiler_params=pltpu.CompilerParams(
            dimension_semantics=("arbitrary",), vmem_limit_bytes=V7X_VMEM_LIMIT_BYTES),
        name="diff_attn",
    )(proj, proj, proj, lq1, lk1, lq2, lk2, subln_w, *cast_weights)
    return outs[0], outs[1:]


_OUT_RC = 512


def _outproj_kernel(yr_ref, yd_ref, w_ref, x_ref, g_ref, x1_ref, *, bm, wr):
    for r in range(bm // _OUT_RC):
        rows = slice(r * _OUT_RC, (r + 1) * _OUT_RC)
        mixed = (jnp.dot(yr_ref[rows, :], w_ref[:wr, :], preferred_element_type=F32)
                 + jnp.dot(yd_ref[rows, :], w_ref[wr:, :], preferred_element_type=F32))
        x1_ref[rows, :] = x_ref[rows, :] + g_ref[0] * mixed


def _outproj(y_ret, y_dif, w_out_bf16, x2d, mod3, *, seq, bm=1024, bn=2048):
    bt, d = x2d.shape
    wr = y_ret.shape[1]
    wd = y_dif.shape[1]
    tiles_per_seq = seq // bm
    w_mode = dict(pipeline_mode=pl.Buffered(1)) if bn == d else {}
    return pl.pallas_call(
        functools.partial(_outproj_kernel, bm=bm, wr=wr),
        out_shape=jax.ShapeDtypeStruct((bt, d), F32),
        grid=(bt // bm, d // bn),
        in_specs=[
            pl.BlockSpec((bm, wr), lambda m, n: (m, 0)),
            pl.BlockSpec((bm, wd), lambda m, n: (m, 0)),
            pl.BlockSpec((wr + wd, bn), lambda m, n: (0, n), **w_mode),
            pl.BlockSpec((bm, bn), lambda m, n: (m, n)),
            pl.BlockSpec((1, 1, bn), lambda m, n: ((m // tiles_per_seq) * 6 + 2, 0, n)),
        ],
        out_specs=pl.BlockSpec((bm, bn), lambda m, n: (m, n)),
        compiler_params=pltpu.CompilerParams(
            dimension_semantics=("arbitrary", "arbitrary"), vmem_limit_bytes=V7X_VMEM_LIMIT_BYTES),
        name="outproj",
    )(y_ret, y_dif, w_out_bf16, x2d, mod3)


_MLP_RC = 512
_MLP_NORM_ROWS = 256
_MLP_FINAL_ROWS = 16


def _mlp_kernel(x1_ref, sh_ref, sc_ref, nw_ref, w1_ref, w2_ref, g_ref, fw_ref, o_ref, h_ref,
                *, bm, final_norm):
    f = pl.program_id(1)
    nf = pl.num_programs(1)

    def ffn(first):
        for r in range(bm // _MLP_RC):
            rows = slice(r * _MLP_RC, (r + 1) * _MLP_RC)
            a = jnp.dot(h_ref[rows, :], w1_ref[...], preferred_element_type=F32)
            a = jnp.maximum(a, 0.0)
            a = (a * a).astype(BF16)
            part = jnp.dot(a, w2_ref[...], preferred_element_type=F32)
            if first:
                o_ref[rows, :] = part
            else:
                o_ref[rows, :] += part

    @pl.when(f == 0)
    def _():
        for r in range(bm // _MLP_NORM_ROWS):
            rows = slice(r * _MLP_NORM_ROWS, (r + 1) * _MLP_NORM_ROWS)
            x = x1_ref[rows, :]
            ms = jnp.mean(x * x, axis=-1, keepdims=True)
            y = x * lax.rsqrt(ms + NORM_EPS) * nw_ref[...]
            h_ref[rows, :] = (y * (1.0 + sc_ref[0]) + sh_ref[0]).astype(BF16)
        ffn(True)

    @pl.when(f > 0)
    def _():
        ffn(False)

    @pl.when(f == nf - 1)
    def _():
        for r in range(bm // _MLP_FINAL_ROWS):
            rows = slice(r * _MLP_FINAL_ROWS, (r + 1) * _MLP_FINAL_ROWS)
            x2 = x1_ref[rows, :] + g_ref[0] * o_ref[rows, :]
            if final_norm:
                ms = jnp.mean(x2 * x2, axis=-1, keepdims=True)
                x2 = x2 * lax.rsqrt(ms + NORM_EPS) * fw_ref[...]
            o_ref[rows, :] = x2


def _mlp(x1, w1_bf16, w2_bf16, mod3, norm_w, final_w, *, seq, final_norm, bm=1024, bf=1024):
    bt, d = x1.shape
    dff = w1_bf16.shape[1]
    tiles_per_seq = seq // bm
    mod_spec = lambda idx: pl.BlockSpec(
        (1, 1, d), lambda m, f: ((m // tiles_per_seq) * 6 + idx, 0, 0))
    return pl.pallas_call(
        functools.partial(_mlp_kernel, bm=bm, final_norm=final_norm),
        out_shape=jax.ShapeDtypeStruct((bt, d), F32),
        grid=(bt // bm, dff // bf),
        in_specs=[
            pl.BlockSpec((bm, d), lambda m, f: (m, 0)),
            mod_spec(3), mod_spec(4),
            pl.BlockSpec((1, d), lambda m, f: (0, 0)),
            pl.BlockSpec((d, bf), lambda m, f: (0, f)),
            pl.BlockSpec((bf, d), lambda m, f: (f, 0)),
            mod_spec(5),
            pl.BlockSpec((1, d), lambda m, f: (0, 0)),
        ],
        out_specs=pl.BlockSpec((bm, d), lambda m, f: (m, 0)),
        scratch_shapes=[pltpu.VMEM((bm, d), BF16)],
        compiler_params=pltpu.CompilerParams(
            dimension_semantics=("arbitrary", "arbitrary"), vmem_limit_bytes=V7X_VMEM_LIMIT_MLP_BYTES),
        name="mlp",
    )(x1, mod3, mod3, norm_w, w1_bf16, w2_bf16, mod3, final_w)


def kernel(x, c, w_ada, b_ada, norm1_w, norm2_w, w_in, ret_gn_w, diff_lq1, diff_lk1, diff_lq2,
           diff_lk2, diff_subln_w, w_out, w_mlp1, w_mlp2, final_norm_w):
    batch, seq, d = x.shape
    depth = w_ada.shape[0]
    c_pad = jnp.pad(c, ((0, F32_SUBLANES - batch), (0, 0)))
    rot = _rotary_lane_constants()
    x2d = x.reshape(batch * seq, d)
    final_w = final_norm_w.reshape(1, d)

    for l in range(depth):
        lam_init = 0.8 - 0.6 * math.exp(-0.3 * l)
        mod, tabs = _ada(c_pad, w_ada[l], b_ada[l].reshape(1, -1), rot, seq=seq)
        mod3 = mod[:batch].reshape(batch * 6, 1, d)
        proj = _proj(x2d, mod3, norm1_w[l].reshape(1, d), w_in[l], tabs, seq=seq)
        y_ret = _retention(proj, ret_gn_w[l].reshape(1, -1), batch=batch, seq=seq)
        y_dif, (w_out_b, w1_b, w2_b) = _diff_attention(
            proj, diff_lq1[l].reshape(1, -1), diff_lk1[l].reshape(1, -1),
            diff_lq2[l].reshape(1, -1), diff_lk2[l].reshape(1, -1),
            diff_subln_w[l].reshape(1, -1), (w_out[l], w_mlp1[l], w_mlp2[l]),
            batch=batch, seq=seq, lam_init=lam_init)
        x1 = _outproj(y_ret, y_dif, w_out_b, x2d, mod3, seq=seq)
        x2d = _mlp(x1, w1_b, w2_b, mod3, norm2_w[l].reshape(1, d), final_w, seq=seq,
                   final_norm=(l == depth - 1))
    if depth == 0:
        raise ValueError("depth must be positive")
    return x2d.reshape(batch, seq, d)
```

```python
import functools
import math

import jax
import jax.numpy as jnp
from jax import lax
from jax.experimental import pallas as pl
from jax.experimental.pallas import tpu as pltpu

F32 = jnp.float32
BF16 = jnp.bfloat16

HEAD_DIM = 128
N_RET_HEADS = 8
N_DIFF_HEADS = 8
DIFF_DQK = HEAD_DIM // 2
DIFF_ROT_DIM = DIFF_DQK // 4
RET_CHUNK = 128
ROPE_THETA = 500000.0
RET_THETA = 10000.0
NORM_EPS = 1e-6
GN_EPS = 1e-5

V7X_VMEM_LIMIT_BYTES = 58 * 1024 * 1024
V7X_VMEM_LIMIT_MLP_BYTES = 61 * 1024 * 1024
F32_SUBLANES = 8


def _silu(v):
    return v * jax.nn.sigmoid(v)


def _ada_kernel(c_ref, w_ref, b_ref, rot_ref, o_ref, ret_c_ref, ret_s_ref, dif_c_ref,
                dif_a_ref, dif_b_ref, base_ref, *, tab_rows):
    s = _silu(c_ref[...]).astype(BF16)
    part = jnp.dot(s, w_ref[...].astype(BF16), preferred_element_type=F32)

    @pl.when(pl.program_id(0) == 0)
    def _():
        o_ref[...] = part + b_ref[...]

    @pl.when(pl.program_id(0) > 0)
    def _():
        o_ref[...] += part

    freq = (rot_ref[0:1, :], rot_ref[2:3, :])

    @pl.when(pl.program_id(0) == 0)
    def _():
        r = lax.broadcasted_iota(jnp.int32, (tab_rows, HEAD_DIM), 0).astype(F32)
        for k, w in enumerate(freq):
            base_ref[2 * k] = jnp.cos(r * w)
            base_ref[2 * k + 1] = jnp.sin(r * w)

    p0 = jnp.full((F32_SUBLANES, HEAD_DIM), pl.program_id(0) * tab_rows, jnp.int32).astype(F32)
    cos_sin = []
    for k, w in enumerate(freq):
        off_c = jnp.cos(p0 * w)[0:1, :]
        off_s = jnp.sin(p0 * w)[0:1, :]
        base_c, base_s = base_ref[2 * k], base_ref[2 * k + 1]
        cos_sin.append((off_c * base_c - off_s * base_s, off_s * base_c + off_c * base_s))
    (cos_ret, sin_ret), (cos_dif, sin_dif) = cos_sin
    ret_c_ref[...] = cos_ret
    ret_s_ref[...] = sin_ret * rot_ref[1:2, :]
    dif_c_ref[...] = cos_dif
    dif_a_ref[...] = sin_dif * rot_ref[3:4, :]
    dif_b_ref[...] = sin_dif * rot_ref[4:5, :]


def _ada(c_pad, w_ada, b_ada, rot, *, seq, n, bk=128):
    rows, d = c_pad.shape
    steps = d // bk
    tab_rows = seq // steps
    assert steps * bk == d and steps * tab_rows == seq and tab_rows % F32_SUBLANES == 0
    tab_spec = pl.BlockSpec((tab_rows, HEAD_DIM), lambda j: (j, 0))
    outs = pl.pallas_call(
        functools.partial(_ada_kernel, tab_rows=tab_rows),
        out_shape=[jax.ShapeDtypeStruct((rows, n), F32)]
        + [jax.ShapeDtypeStruct((seq, HEAD_DIM), F32)] * 5,
        grid=(steps,),
        in_specs=[
            pl.BlockSpec((rows, bk), lambda j: (0, j)),
            pl.BlockSpec((bk, n), lambda j: (j, 0)),
            pl.BlockSpec((1, n), lambda j: (0, 0)),
            pl.BlockSpec(rot.shape, lambda j: (0, 0)),
        ],
        out_specs=[pl.BlockSpec((rows, n), lambda j: (0, 0))] + [tab_spec] * 5,
        scratch_shapes=[pltpu.VMEM((4, tab_rows, HEAD_DIM), F32)],
        compiler_params=pltpu.CompilerParams(
            dimension_semantics=("arbitrary",), vmem_limit_bytes=V7X_VMEM_LIMIT_BYTES),
        name="ada",
    )(c_pad, w_ada, b_ada, rot)
    return outs[0], tuple(outs[1:])


_PROJ_RC = 512
_PROJ_CC = 256
_PROJ_NORM_ROWS = 256


def _proj_kernel(x_ref, sh_ref, sc_ref, nw_ref, w_ref, rc_ref, rs_ref, dc_ref, da_ref, db_ref,
                 o_ref, h_ref, wb_ref, *, bm, bn):
    j = pl.program_id(1)
    n_rc = bm // _PROJ_RC

    def norm_modulate():
        for r in range(bm // _PROJ_NORM_ROWS):
            rows = slice(r * _PROJ_NORM_ROWS, (r + 1) * _PROJ_NORM_ROWS)
            x = x_ref[rows, :]
            ms = jnp.mean(x * x, axis=-1, keepdims=True)
            y = x * lax.rsqrt(ms + NORM_EPS) * nw_ref[...]
            h_ref[rows, :] = (y * (1.0 + sc_ref[0]) + sh_ref[0]).astype(BF16)

    def run(epilogue):
        for cc in range(bn // _PROJ_CC):
            cols = slice(cc * _PROJ_CC, (cc + 1) * _PROJ_CC)
            wb_ref[:, cols] = w_ref[:, cols].astype(BF16)
            for r in range(n_rc):
                rows = slice(r * _PROJ_RC, (r + 1) * _PROJ_RC)
                acc = jnp.dot(h_ref[rows, :], wb_ref[:, cols], preferred_element_type=F32)
                for hb in range(_PROJ_CC // HEAD_DIM):
                    a = acc[:, hb * HEAD_DIM:(hb + 1) * HEAD_DIM]
                    c0 = cc * _PROJ_CC + hb * HEAD_DIM
                    o_ref[rows, c0:c0 + HEAD_DIM] = epilogue(a, rows).astype(o_ref.dtype)

    def ret_rotary(a, rows):
        return a * rc_ref[rows, :] + pltpu.roll(a, HEAD_DIM // 2, 1) * rs_ref[rows, :]

    @pl.when(j == 0)
    def _():
        norm_modulate()
        run(ret_rotary)

    @pl.when(j == 1)
    def _():
        run(lambda a, rows: ret_rotary(a, rows) * (HEAD_DIM ** -0.5))

    @pl.when(jnp.logical_or(j == 4, j == 5))
    def _():
        scale = jnp.where(j == 4, DIFF_DQK ** -0.5 * math.log2(math.e), 1.0).astype(F32)
        half = DIFF_ROT_DIM // 2

        def ep(a, rows):
            rot = (a * dc_ref[rows, :]
                   + pltpu.roll(a, HEAD_DIM - half, 1) * da_ref[rows, :]
                   + pltpu.roll(a, half, 1) * db_ref[rows, :])
            return rot * scale
        run(ep)

    @pl.when(jnp.logical_or(jnp.logical_or(j == 2, j == 3), j == 6))
    def _():
        run(lambda a, rows: a)


def _proj(x2d, mod3, norm_w, w_in, tabs, *, seq, bm=1024, bn=1024):
    bt, d = x2d.shape
    n = w_in.shape[1]
    assert bn == N_RET_HEADS * HEAD_DIM == N_DIFF_HEADS * HEAD_DIM and n == 7 * bn
    tiles_per_seq = seq // bm
    n_m = bt // bm

    def tab_spec(last_use):
        return pl.BlockSpec((bm, HEAD_DIM), lambda m, j: (
            jnp.minimum(jnp.where(j > last_use, m + 1, m), n_m - 1) % tiles_per_seq, 0))
    return pl.pallas_call(
        functools.partial(_proj_kernel, bm=bm, bn=bn),
        out_shape=jax.ShapeDtypeStruct((bt, n), BF16),
        grid=(bt // bm, n // bn),
        in_specs=[
            pl.BlockSpec((bm, d), lambda m, j: (jnp.minimum(jnp.where(j > 0, m + 1, m), n_m - 1), 0)),
            pl.BlockSpec((1, 1, d), lambda m, j: ((m // tiles_per_seq) * 2 + 0, 0, 0)),
            pl.BlockSpec((1, 1, d), lambda m, j: ((m // tiles_per_seq) * 2 + 1, 0, 0)),
            pl.BlockSpec((1, d), lambda m, j: (0, 0)),
            pl.BlockSpec((d, bn), lambda m, j: (0, j)),
            tab_spec(1), tab_spec(1), tab_spec(5), tab_spec(5), tab_spec(5),
        ],
        out_specs=pl.BlockSpec((bm, bn), lambda m, j: (m, j)),
        scratch_shapes=[pltpu.VMEM((bm, d), BF16), pltpu.VMEM((d, bn), BF16)],
        compiler_params=pltpu.CompilerParams(
            dimension_semantics=("arbitrary", "arbitrary"), vmem_limit_bytes=V7X_VMEM_LIMIT_BYTES),
        name="proj",
    )(x2d, mod3, mod3, norm_w, w_in, *tabs)


def _rotary_lane_constants():
    lane = jnp.arange(HEAD_DIM)
    half = HEAD_DIM // 2
    inv_ret = RET_THETA ** (-(2 * (lane % half)).astype(F32) / HEAD_DIM)
    sign_ret = jnp.where(lane < half, -1.0, 1.0)
    half = DIFF_ROT_DIM // 2
    within = lane % DIFF_DQK
    inv_dif = ROPE_THETA ** (-(2 * (within % half)).astype(F32) / DIFF_ROT_DIM)
    inv_dif = jnp.where(within < DIFF_ROT_DIM, inv_dif, 0.0)
    sel_a = jnp.where(within < half, -1.0, 0.0)
    sel_b = jnp.where((within >= half) & (within < DIFF_ROT_DIM), 1.0, 0.0)
    rows = [inv_ret, sign_ret, inv_dif, sel_a, sel_b]
    rows += [jnp.zeros((HEAD_DIM,), F32)] * (F32_SUBLANES - len(rows))
    return jnp.stack(rows).astype(F32)


def _ret_kernel(q_ref, k_ref, v_ref, g_ref, gw_ref, c_ref, wada_ref, bada_ref, o_ref, mod_ref,
                s_ref, const_ref, *, seq):
    mod_ref[...] = jnp.dot(_silu(c_ref[...]).astype(BF16), wada_ref[...].astype(BF16),
                           preferred_element_type=F32) + bada_ref[...]

    C = RET_CHUNK
    nc = seq // C
    head = (pl.program_id(0) % N_RET_HEADS).astype(F32)
    ii = lax.broadcasted_iota(jnp.int32, (C, C), 0).astype(F32)
    jj = lax.broadcasted_iota(jnp.int32, (C, C), 1).astype(F32)
    lg = jnp.log(1.0 - jnp.exp2(jnp.zeros((C, C), F32) - 5.0 - head))
    rel = ii - jj
    DECAY, XI, ZETA, CHUNK_DECAY = range(4)
    const_ref[DECAY] = jnp.where(rel >= 0, jnp.exp(jnp.maximum(rel, 0.0) * lg), 0.0)
    const_ref[XI] = jnp.exp((ii + 1.0) * lg)
    const_ref[ZETA] = jnp.exp((C - 1.0 - ii) * lg)
    const_ref[CHUNK_DECAY] = jnp.exp(C * lg)

    def scan_body(c, state):
        rows = pl.ds(pl.multiple_of(c * C, C), C)
        s_ref[c] = state.astype(BF16)
        kz = (k_ref[rows, :].astype(F32) * const_ref[ZETA]).astype(BF16)
        kv = lax.dot_general(kz, v_ref[rows, :], (((0,), (0,)), ((), ())),
                             preferred_element_type=F32)
        return state * const_ref[CHUNK_DECAY] + kv
    lax.fori_loop(0, nc, scan_body, jnp.zeros((C, C), F32), unroll=32)

    gw = gw_ref[...]

    def out_body(c, carry):
        rows = pl.ds(pl.multiple_of(c * C, C), C)
        q = q_ref[rows, :]
        inner = lax.dot_general(q, k_ref[rows, :], (((1,), (1,)), ((), ())),
                                preferred_element_type=F32) * const_ref[DECAY]
        o = jnp.dot(inner.astype(BF16), v_ref[rows, :], preferred_element_type=F32)
        o = o + jnp.dot(q, s_ref[c], preferred_element_type=F32) * const_ref[XI]
        mu = jnp.mean(o, axis=-1, keepdims=True)
        d = o - mu
        var = jnp.mean(d * d, axis=-1, keepdims=True)
        y = d * lax.rsqrt(var + GN_EPS) * gw
        o_ref[rows, :] = (y * _silu(g_ref[rows, :].astype(F32))).astype(o_ref.dtype)
        return carry
    lax.fori_loop(0, nc, out_body, 0, unroll=32)


def _retention(proj, gn_w, c_pad, w_ada, b_ada, ada_col0, *, batch, seq):
    nh = N_RET_HEADS
    steps = batch * nh
    blk = lambda off: pl.BlockSpec((seq, HEAD_DIM), lambda s: (s // nh, off + s % nh))
    rows, d = c_pad.shape
    ada_cols = w_ada.shape[1] - ada_col0
    ada_bn = ada_cols // steps
    assert ada_bn * steps == ada_cols and ada_col0 % ada_bn == 0 and ada_bn % HEAD_DIM == 0
    ada_blk0 = ada_col0 // ada_bn
    return pl.pallas_call(
        functools.partial(_ret_kernel, seq=seq),
        out_shape=(jax.ShapeDtypeStruct((batch * seq, nh * HEAD_DIM), BF16),
                   jax.ShapeDtypeStruct((rows, ada_cols), F32)),
        grid=(steps,),
        in_specs=[blk(0), blk(nh), blk(2 * nh), blk(3 * nh),
                  pl.BlockSpec((1, HEAD_DIM), lambda s: (0, s % nh)),
                  pl.BlockSpec((rows, d), lambda s: (0, 0)),
                  pl.BlockSpec((d, ada_bn), lambda s: (0, ada_blk0 + s)),
                  pl.BlockSpec((1, ada_bn), lambda s: (0, ada_blk0 + s))],
        out_specs=(pl.BlockSpec((seq, HEAD_DIM), lambda s: (s // nh, s % nh)),
                   pl.BlockSpec((rows, ada_bn), lambda s: (0, s))),
        scratch_shapes=[pltpu.VMEM((seq // RET_CHUNK, RET_CHUNK, HEAD_DIM), BF16),
                        pltpu.VMEM((4, RET_CHUNK, RET_CHUNK), F32)],
        compiler_params=pltpu.CompilerParams(
            dimension_semantics=("arbitrary",), vmem_limit_bytes=V7X_VMEM_LIMIT_BYTES),
        name="retention",
    )(proj, proj, proj, proj, gn_w, c_pad, w_ada, b_ada)


_ATTN_STRIP = 256
_ATTN_SCORE_BUFS = 4


_ATTN_N_IN = 8


def _attn_kernel(*refs, blk, lam_init, n_cast):
    q_ref, k_ref, v_ref, lq1_ref, lk1_ref, lq2_ref, lk2_ref, sw_ref = refs[:_ATTN_N_IN]
    cast_src = refs[_ATTN_N_IN:_ATTN_N_IN + n_cast]
    o_ref = refs[_ATTN_N_IN + n_cast]
    cast_dst = refs[_ATTN_N_IN + n_cast + 1:_ATTN_N_IN + 2 * n_cast + 1]
    vt_ref, qt_ref, m_ref, l_ref, acc_ref, *score_bufs = refs[_ATTN_N_IN + 2 * n_cast + 1:]

    def side_work(part, n_parts):
        for src, dst in zip(cast_src, cast_dst):
            rows = src.shape[0] // n_parts
            dst[part * rows:(part + 1) * rows, :] = src[part * rows:(part + 1) * rows, :].astype(
                dst.dtype)

    _attn_body(q_ref, k_ref, v_ref, lq1_ref, lk1_ref, lq2_ref, lk2_ref, sw_ref, o_ref,
               vt_ref, qt_ref, m_ref, l_ref, acc_ref, tuple(score_bufs), side_work,
               blk=blk, lam_init=lam_init)


def _attn_body(q_ref, k_ref, v_ref, lq1_ref, lk1_ref, lq2_ref, lk2_ref, sw_ref, o_ref,
               vt_ref, qt_ref, m_ref, l_ref, acc_ref, score_bufs, side_work, *, blk, lam_init):
    nblk = vt_ref.shape[0]
    nstrip = blk // _ATTN_STRIP
    assert nblk % 2 == 0

    for jb in range(nblk):
        vt_ref[jb] = v_ref[jb * blk:(jb + 1) * blk, :].T

    feat = lax.broadcasted_iota(jnp.int32, (HEAD_DIM, _ATTN_STRIP), 0)
    for idx in range(nblk * nstrip):
        qt = q_ref[idx * _ATTN_STRIP:(idx + 1) * _ATTN_STRIP, :].T
        zero = jnp.zeros_like(qt)
        qt_ref[0, idx] = jnp.where(feat < DIFF_DQK, qt, zero)
        qt_ref[1, idx] = jnp.where(feat >= DIFF_DQK, qt, zero)

    chains = [(comp, s) for comp in range(2) for s in range(nstrip)]

    def keys_needed(s, diagonal):
        return (s + 1) * _ATTN_STRIP if diagonal else blk

    def scores(item, s_ref, diagonal=False):
        i, j = item
        jrow = j * blk
        if not isinstance(jrow, int):
            jrow = pl.multiple_of(jrow, blk)
        for c, (comp, s) in enumerate(chains):
            nk = keys_needed(s, diagonal)
            s_ref[c, :nk, :] = jnp.dot(k_ref[pl.ds(jrow, nk), :], qt_ref[comp, i * nstrip + s],
                                       preferred_element_type=F32)

    def softmax_av(item, s_ref, diagonal):
        i, j = item
        vtb = vt_ref[j]
        for c, (comp, s) in enumerate(chains):
            nk = keys_needed(s, diagonal)
            st = s_ref[c, :nk, :]
            idx = i * nstrip + s
            if diagonal:
                kpos = lax.broadcasted_iota(jnp.int32, st.shape, 0)
                qpos = lax.broadcasted_iota(jnp.int32, st.shape, 1) + s * _ATTN_STRIP
                st = jnp.where(kpos <= qpos, st, -jnp.inf)
                m_new = jnp.max(st, axis=0, keepdims=True)
                p = jnp.exp2(st - m_new)
                l_ref[comp, idx] = jnp.sum(p, axis=0, keepdims=True)
                acc_ref[comp, idx] = jnp.dot(vtb[:, :nk], p.astype(BF16),
                                             preferred_element_type=F32)
            else:
                m_old = m_ref[comp, idx]
                m_new = jnp.maximum(m_old, jnp.max(st, axis=0, keepdims=True))
                alpha = jnp.exp2(m_old - m_new)
                p = jnp.exp2(st - m_new)
                l_ref[comp, idx] = alpha * l_ref[comp, idx] + jnp.sum(p, axis=0, keepdims=True)
                acc_ref[comp, idx] = alpha * acc_ref[comp, idx] + jnp.dot(
                    vtb, p.astype(BF16), preferred_element_type=F32)
            m_ref[comp, idx] = m_new

    def next_item(item):
        i, j = item
        wrap = j + 1 >= i
        return jnp.where(wrap, i + 1, i), jnp.where(wrap, 0, j + 1)

    nbuf = len(score_bufs)
    n_off = nblk * (nblk - 1) // 2
    assert n_off % nbuf == 0
    buf = lambda n: score_bufs[n % nbuf]

    n_parts = nblk // 2
    scores((0, 0), buf(0), True)
    for n in range(nblk):
        if n % 2 == 0:
            side_work(n // 2, n_parts)
        if n + 1 < nblk:
            scores((n + 1, n + 1), buf(n + 1), True)
        else:
            scores((1, 0), buf(n + 1))
        softmax_av((n, n), buf(n), True)

    def rotation(t, item):
        items = [item]
        for _ in range(nbuf):
            items.append(next_item(items[-1]))
        for n in range(nbuf):
            scores(items[n + 1], buf(nblk + n + 1))
            softmax_av(items[n], buf(nblk + n), False)
        return items[-1]
    item = lax.fori_loop(0, n_off // nbuf - 1, rotation, (jnp.int32(1), jnp.int32(0)))
    items = [item]
    for _ in range(nbuf - 1):
        items.append(next_item(items[-1]))
    for n in range(nbuf):
        if n + 1 < nbuf:
            scores(items[n + 1], buf(nblk + n + 1))
        softmax_av(items[n], buf(nblk + n), False)

    lam = (jnp.exp(jnp.sum(lq1_ref[...] * lk1_ref[...], axis=-1, keepdims=True))
           - jnp.exp(jnp.sum(lq2_ref[...] * lk2_ref[...], axis=-1, keepdims=True))
           + lam_init)
    sw = sw_ref[...] * (1.0 - lam_init)

    for idx in range(nblk * nstrip):
        yt = acc_ref[0, idx] / l_ref[0, idx] - lam * (acc_ref[1, idx] / l_ref[1, idx])
        yt = yt * lax.rsqrt(jnp.mean(yt * yt, axis=0, keepdims=True) + GN_EPS)
        rows = slice(idx * _ATTN_STRIP, (idx + 1) * _ATTN_STRIP)
        o_ref[rows, :] = (yt.T * sw).astype(o_ref.dtype)


def _diff_attention(proj, lq1, lk1, lq2, lk2, subln_w, cast_weights, *, batch, seq, lam_init,
                    blk=512):
    nh = N_DIFF_HEADS
    nblk = seq // blk
    nchain = 2 * blk // _ATTN_STRIP
    nstrips = seq // _ATTN_STRIP
    steps = batch * nh
    q_off, k_off, v_off = 4 * nh, 5 * nh, 6 * nh
    head_blk = lambda off: pl.BlockSpec((seq, HEAD_DIM), lambda s: (s // nh, off + s % nh))
    lam_spec = pl.BlockSpec((1, DIFF_DQK), lambda s: (0, 0))
    cast_specs = [pl.BlockSpec((w.shape[0] // steps, w.shape[1]), lambda s: (s, 0))
                  for w in cast_weights]
    outs = pl.pallas_call(
        functools.partial(_attn_kernel, blk=blk, lam_init=lam_init, n_cast=len(cast_weights)),
        out_shape=[jax.ShapeDtypeStruct((batch * seq, nh * HEAD_DIM), BF16)]
        + [jax.ShapeDtypeStruct(w.shape, BF16) for w in cast_weights],
        grid=(steps,),
        in_specs=[
            head_blk(q_off), head_blk(k_off), head_blk(v_off),
            lam_spec, lam_spec, lam_spec, lam_spec,
            pl.BlockSpec((1, HEAD_DIM), lambda s: (0, s % nh)),
        ] + cast_specs,
        out_specs=[pl.BlockSpec((seq, HEAD_DIM), lambda s: (s // nh, s % nh))] + cast_specs,
        scratch_shapes=[
            pltpu.VMEM((nblk, HEAD_DIM, blk), BF16),
            pltpu.VMEM((2, nstrips, HEAD_DIM, _ATTN_STRIP), BF16),
            pltpu.VMEM((2, nstrips, 1, _ATTN_STRIP), F32),
            pltpu.VMEM((2, nstrips, 1, _ATTN_STRIP), F32),
            pltpu.VMEM((2, nstrips, HEAD_DIM, _ATTN_STRIP), F32),
        ] + [pltpu.VMEM((nchain, blk, _ATTN_STRIP), F32)] * _ATTN_SCORE_BUFS,
        compiler_params=pltpu.CompilerParams(
            dimension_semantics=("arbitrary",), vmem_limit_bytes=V7X_VMEM_LIMIT_BYTES),
        name="diff_attn",
    )(proj, proj, proj, lq1, lk1, lq2, lk2, subln_w, *cast_weights)
    return outs[0], outs[1:]


_OUT_RC = 512


def _outproj_kernel(yr_ref, yd_ref, w_ref, x_ref, g_ref, x1_ref, *, bm, wr):
    for r in range(bm // _OUT_RC):
        rows = slice(r * _OUT_RC, (r + 1) * _OUT_RC)
        mixed = (jnp.dot(yr_ref[rows, :], w_ref[:wr, :], preferred_element_type=F32)
                 + jnp.dot(yd_ref[rows, :], w_ref[wr:, :], preferred_element_type=F32))
        x1_ref[rows, :] = x_ref[rows, :] + g_ref[0] * mixed


def _outproj(y_ret, y_dif, w_out_bf16, x2d, mod3, *, seq, bm=512, bn=2048):
    bt, d = x2d.shape
    wr = y_ret.shape[1]
    wd = y_dif.shape[1]
    tiles_per_seq = seq // bm
    return pl.pallas_call(
        functools.partial(_outproj_kernel, bm=bm, wr=wr),
        out_shape=jax.ShapeDtypeStruct((bt, d), F32),
        grid=(bt // bm, d // bn),
        in_specs=[
            pl.BlockSpec((bm, wr), lambda m, n: (m, 0)),
            pl.BlockSpec((bm, wd), lambda m, n: (m, 0)),
            pl.BlockSpec((wr + wd, bn), lambda m, n: (0, n)),
            pl.BlockSpec((bm, bn), lambda m, n: (m, n)),
            pl.BlockSpec((1, 1, bn), lambda m, n: ((m // tiles_per_seq) * 6 + 2, 0, n)),
        ],
        out_specs=pl.BlockSpec((bm, bn), lambda m, n: (m, n)),
        compiler_params=pltpu.CompilerParams(
            dimension_semantics=("arbitrary", "arbitrary"), vmem_limit_bytes=V7X_VMEM_LIMIT_BYTES),
        name="outproj",
    )(y_ret, y_dif, w_out_bf16, x2d, mod3)


_MLP_RC = 512
_MLP_NORM_ROWS = 256
_MLP_FINAL_ROWS = 16


def _mlp_kernel(x1_ref, sh_ref, sc_ref, nw_ref, w1_ref, w2_ref, g_ref, fw_ref, o_ref, h_ref,
                *, bm, final_norm):
    f = pl.program_id(1)
    nf = pl.num_programs(1)

    def ffn(first):
        for r in range(bm // _MLP_RC):
            rows = slice(r * _MLP_RC, (r + 1) * _MLP_RC)
            a = jnp.dot(h_ref[rows, :], w1_ref[...], preferred_element_type=F32)
            a = jnp.maximum(a, 0.0)
            a = (a * a).astype(BF16)
            part = jnp.dot(a, w2_ref[...], preferred_element_type=F32)
            if first:
                o_ref[rows, :] = part
            else:
                o_ref[rows, :] += part

    @pl.when(f == 0)
    def _():
        for r in range(bm // _MLP_NORM_ROWS):
            rows = slice(r * _MLP_NORM_ROWS, (r + 1) * _MLP_NORM_ROWS)
            x = x1_ref[rows, :]
            ms = jnp.mean(x * x, axis=-1, keepdims=True)
            y = x * lax.rsqrt(ms + NORM_EPS) * nw_ref[...]
            h_ref[rows, :] = (y * (1.0 + sc_ref[0]) + sh_ref[0]).astype(BF16)
        ffn(True)

    @pl.when(f > 0)
    def _():
        ffn(False)

    @pl.when(f == nf - 1)
    def _():
        for r in range(bm // _MLP_FINAL_ROWS):
            rows = slice(r * _MLP_FINAL_ROWS, (r + 1) * _MLP_FINAL_ROWS)
            x2 = x1_ref[rows, :] + g_ref[0] * o_ref[rows, :]
            if final_norm:
                ms = jnp.mean(x2 * x2, axis=-1, keepdims=True)
                x2 = x2 * lax.rsqrt(ms + NORM_EPS) * fw_ref[...]
            o_ref[rows, :] = x2


def _mlp(x1, w1_bf16, w2_bf16, mod3, norm_w, final_w, *, seq, final_norm, bm=1024, bf=1024):
    bt, d = x1.shape
    dff = w1_bf16.shape[1]
    tiles_per_seq = seq // bm
    mod_spec = lambda idx: pl.BlockSpec(
        (1, 1, d), lambda m, f: ((m // tiles_per_seq) * 6 + idx, 0, 0))
    return pl.pallas_call(
        functools.partial(_mlp_kernel, bm=bm, final_norm=final_norm),
        out_shape=jax.ShapeDtypeStruct((bt, d), F32),
        grid=(bt // bm, dff // bf),
        in_specs=[
            pl.BlockSpec((bm, d), lambda m, f: (m, 0)),
            mod_spec(3), mod_spec(4),
            pl.BlockSpec((1, d), lambda m, f: (0, 0)),
            pl.BlockSpec((d, bf), lambda m, f: (0, f)),
            pl.BlockSpec((bf, d), lambda m, f: (f, 0)),
            mod_spec(5),
            pl.BlockSpec((1, d), lambda m, f: (0, 0)),
        ],
        out_specs=pl.BlockSpec((bm, d), lambda m, f: (m, 0)),
        scratch_shapes=[pltpu.VMEM((bm, d), BF16)],
        compiler_params=pltpu.CompilerParams(
            dimension_semantics=("arbitrary", "arbitrary"), vmem_limit_bytes=V7X_VMEM_LIMIT_MLP_BYTES),
        name="mlp",
    )(x1, mod3, mod3, norm_w, w1_bf16, w2_bf16, mod3, final_w)


def kernel(x, c, w_ada, b_ada, norm1_w, norm2_w, w_in, ret_gn_w, diff_lq1, diff_lk1, diff_lq2,
           diff_lk2, diff_subln_w, w_out, w_mlp1, w_mlp2, final_norm_w):
    batch, seq, d = x.shape
    depth = w_ada.shape[0]
    c_pad = jnp.pad(c, ((0, F32_SUBLANES - batch), (0, 0)))
    rot = _rotary_lane_constants()
    x2d = x.reshape(batch * seq, d)
    final_w = final_norm_w.reshape(1, d)

    for l in range(depth):
        lam_init = 0.8 - 0.6 * math.exp(-0.3 * l)
        b_ada_l = b_ada[l].reshape(1, -1)
        mod_a, tabs = _ada(c_pad, w_ada[l], b_ada_l, rot, seq=seq, n=2 * d)
        mod3_a = mod_a[:batch].reshape(batch * 2, 1, d)
        proj = _proj(x2d, mod3_a, norm1_w[l].reshape(1, d), w_in[l], tabs, seq=seq)
        y_ret, mod_b = _retention(proj, ret_gn_w[l].reshape(1, -1), c_pad, w_ada[l], b_ada_l,
                                  2 * d, batch=batch, seq=seq)
        mod3 = jnp.concatenate([mod_a[:batch], mod_b[:batch]], axis=1).reshape(batch * 6, 1, d)
        y_dif, (w_out_b, w1_b, w2_b) = _diff_attention(
            proj, diff_lq1[l].reshape(1, -1), diff_lk1[l].reshape(1, -1),
            diff_lq2[l].reshape(1, -1), diff_lk2[l].reshape(1, -1),
            diff_subln_w[l].reshape(1, -1), (w_out[l], w_mlp1[l], w_mlp2[l]),
            batch=batch, seq=seq, lam_init=lam_init)
        x1 = _outproj(y_ret, y_dif, w_out_b, x2d, mod3, seq=seq)
        x2d = _mlp(x1, w1_b, w2_b, mod3, norm2_w[l].reshape(1, d), final_w, seq=seq,
                   final_norm=(l == depth - 1))
    if depth == 0:
        raise ValueError("depth must be positive")
    return x2d.reshape(batch, seq, d)
```

```python
import functools
import math

import jax
import jax.numpy as jnp
from jax import lax
from jax.experimental import pallas as pl
from jax.experimental.pallas import tpu as pltpu

F32 = jnp.float32
BF16 = jnp.bfloat16

HEAD_DIM = 128
N_RET_HEADS = 8
N_DIFF_HEADS = 8
DIFF_DQK = HEAD_DIM // 2
DIFF_ROT_DIM = DIFF_DQK // 4
RET_CHUNK = 128
ROPE_THETA = 500000.0
RET_THETA = 10000.0
NORM_EPS = 1e-6
GN_EPS = 1e-5

V7X_VMEM_LIMIT_BYTES = 58 * 1024 * 1024
V7X_VMEM_LIMIT_MLP_BYTES = 61 * 1024 * 1024
F32_SUBLANES = 8


def _silu(v):
    return v * jax.nn.sigmoid(v)


def _ada_kernel(c_ref, w_ref, b_ref, rot_ref, o_ref, ret_c_ref, ret_s_ref, dif_c_ref,
                dif_a_ref, dif_b_ref, base_ref, *, tab_rows):
    s = _silu(c_ref[...]).astype(BF16)
    part = jnp.dot(s, w_ref[...].astype(BF16), preferred_element_type=F32)

    @pl.when(pl.program_id(0) == 0)
    def _():
        o_ref[...] = part + b_ref[...]

    @pl.when(pl.program_id(0) > 0)
    def _():
        o_ref[...] += part

    freq = (rot_ref[0:1, :], rot_ref[2:3, :])

    @pl.when(pl.program_id(0) == 0)
    def _():
        r = lax.broadcasted_iota(jnp.int32, (tab_rows, HEAD_DIM), 0).astype(F32)
        for k, w in enumerate(freq):
            base_ref[2 * k] = jnp.cos(r * w)
            base_ref[2 * k + 1] = jnp.sin(r * w)

    p0 = jnp.full((F32_SUBLANES, HEAD_DIM), pl.program_id(0) * tab_rows, jnp.int32).astype(F32)
    cos_sin = []
    for k, w in enumerate(freq):
        off_c = jnp.cos(p0 * w)[0:1, :]
        off_s = jnp.sin(p0 * w)[0:1, :]
        base_c, base_s = base_ref[2 * k], base_ref[2 * k + 1]
        cos_sin.append((off_c * base_c - off_s * base_s, off_s * base_c + off_c * base_s))
    (cos_ret, sin_ret), (cos_dif, sin_dif) = cos_sin
    ret_c_ref[...] = cos_ret
    ret_s_ref[...] = sin_ret * rot_ref[1:2, :]
    dif_c_ref[...] = cos_dif
    dif_a_ref[...] = sin_dif * rot_ref[3:4, :]
    dif_b_ref[...] = sin_dif * rot_ref[4:5, :]


def _ada(c_pad, w_ada, b_ada, rot, *, seq, bk=128):
    rows, d = c_pad.shape
    n = w_ada.shape[1]
    steps = d // bk
    tab_rows = seq // steps
    assert steps * bk == d and steps * tab_rows == seq and tab_rows % F32_SUBLANES == 0
    tab_spec = pl.BlockSpec((tab_rows, HEAD_DIM), lambda j: (j, 0))
    outs = pl.pallas_call(
        functools.partial(_ada_kernel, tab_rows=tab_rows),
        out_shape=[jax.ShapeDtypeStruct((rows, n), F32)]
        + [jax.ShapeDtypeStruct((seq, HEAD_DIM), F32)] * 5,
        grid=(steps,),
        in_specs=[
            pl.BlockSpec((rows, bk), lambda j: (0, j)),
            pl.BlockSpec((bk, n), lambda j: (j, 0)),
            pl.BlockSpec((1, n), lambda j: (0, 0)),
            pl.BlockSpec(rot.shape, lambda j: (0, 0)),
        ],
        out_specs=[pl.BlockSpec((rows, n), lambda j: (0, 0))] + [tab_spec] * 5,
        scratch_shapes=[pltpu.VMEM((4, tab_rows, HEAD_DIM), F32)],
        compiler_params=pltpu.CompilerParams(
            dimension_semantics=("arbitrary",), vmem_limit_bytes=V7X_VMEM_LIMIT_BYTES),
        name="ada",
    )(c_pad, w_ada, b_ada, rot)
    return outs[0], tuple(outs[1:])


_PROJ_RC = 512
_PROJ_CC = 512
_PROJ_NORM_ROWS = 256


def _proj_kernel(x_ref, sh_ref, sc_ref, nw_ref, w_ref, rc_ref, rs_ref, dc_ref, da_ref, db_ref,
                 o_ref, h_ref, wb_ref, *, bm, bn):
    j = pl.program_id(1)
    n_rc = bm // _PROJ_RC

    def norm_modulate():
        for r in range(bm // _PROJ_NORM_ROWS):
            rows = slice(r * _PROJ_NORM_ROWS, (r + 1) * _PROJ_NORM_ROWS)
            x = x_ref[rows, :]
            ms = jnp.mean(x * x, axis=-1, keepdims=True)
            y = x * lax.rsqrt(ms + NORM_EPS) * nw_ref[...]
            h_ref[rows, :] = (y * (1.0 + sc_ref[0]) + sh_ref[0]).astype(BF16)

    def run(epilogue):
        for cc in range(bn // _PROJ_CC):
            cols = slice(cc * _PROJ_CC, (cc + 1) * _PROJ_CC)
            wb_ref[:, cols] = w_ref[:, cols].astype(BF16)
            for r in range(n_rc):
                rows = slice(r * _PROJ_RC, (r + 1) * _PROJ_RC)
                acc = jnp.dot(h_ref[rows, :], wb_ref[:, cols], preferred_element_type=F32)
                for hb in range(_PROJ_CC // HEAD_DIM):
                    a = acc[:, hb * HEAD_DIM:(hb + 1) * HEAD_DIM]
                    c0 = cc * _PROJ_CC + hb * HEAD_DIM
                    o_ref[rows, c0:c0 + HEAD_DIM] = epilogue(a, rows).astype(o_ref.dtype)

    def ret_rotary(a, rows):
        return a * rc_ref[rows, :] + pltpu.roll(a, HEAD_DIM // 2, 1) * rs_ref[rows, :]

    @pl.when(j == 0)
    def _():
        norm_modulate()
        run(ret_rotary)

    @pl.when(j == 1)
    def _():
        run(lambda a, rows: ret_rotary(a, rows) * (HEAD_DIM ** -0.5))

    @pl.when(jnp.logical_or(j == 4, j == 5))
    def _():
        scale = jnp.where(j == 4, DIFF_DQK ** -0.5 * math.log2(math.e), 1.0).astype(F32)
        half = DIFF_ROT_DIM // 2

        def ep(a, rows):
            rot = (a * dc_ref[rows, :]
                   + pltpu.roll(a, HEAD_DIM - half, 1) * da_ref[rows, :]
                   + pltpu.roll(a, half, 1) * db_ref[rows, :])
            return rot * scale
        run(ep)

    @pl.when(jnp.logical_or(jnp.logical_or(j == 2, j == 3), j == 6))
    def _():
        run(lambda a, rows: a)


def _proj(x2d, mod3, norm_w, w_in, tabs, *, seq, bm=1024, bn=1024):
    bt, d = x2d.shape
    n = w_in.shape[1]
    assert bn == N_RET_HEADS * HEAD_DIM == N_DIFF_HEADS * HEAD_DIM and n == 7 * bn
    tiles_per_seq = seq // bm
    n_m = bt // bm

    def tab_spec(last_use):
        return pl.BlockSpec((bm, HEAD_DIM), lambda m, j: (
            jnp.minimum(jnp.where(j > last_use, m + 1, m), n_m - 1) % tiles_per_seq, 0))
    return pl.pallas_call(
        functools.partial(_proj_kernel, bm=bm, bn=bn),
        out_shape=jax.ShapeDtypeStruct((bt, n), BF16),
        grid=(bt // bm, n // bn),
        in_specs=[
            pl.BlockSpec((bm, d), lambda m, j: (jnp.minimum(jnp.where(j > 0, m + 1, m), n_m - 1), 0)),
            pl.BlockSpec((1, 1, d), lambda m, j: ((m // tiles_per_seq) * 6 + 0, 0, 0)),
            pl.BlockSpec((1, 1, d), lambda m, j: ((m // tiles_per_seq) * 6 + 1, 0, 0)),
            pl.BlockSpec((1, d), lambda m, j: (0, 0)),
            pl.BlockSpec((d, bn), lambda m, j: (0, j)),
            tab_spec(1), tab_spec(1), tab_spec(5), tab_spec(5), tab_spec(5),
        ],
        out_specs=pl.BlockSpec((bm, bn), lambda m, j: (m, j)),
        scratch_shapes=[pltpu.VMEM((bm, d), BF16), pltpu.VMEM((d, bn), BF16)],
        compiler_params=pltpu.CompilerParams(
            dimension_semantics=("arbitrary", "arbitrary"), vmem_limit_bytes=V7X_VMEM_LIMIT_BYTES),
        name="proj",
    )(x2d, mod3, mod3, norm_w, w_in, *tabs)


def _rotary_lane_constants():
    lane = jnp.arange(HEAD_DIM)
    half = HEAD_DIM // 2
    inv_ret = RET_THETA ** (-(2 * (lane % half)).astype(F32) / HEAD_DIM)
    sign_ret = jnp.where(lane < half, -1.0, 1.0)
    half = DIFF_ROT_DIM // 2
    within = lane % DIFF_DQK
    inv_dif = ROPE_THETA ** (-(2 * (within % half)).astype(F32) / DIFF_ROT_DIM)
    inv_dif = jnp.where(within < DIFF_ROT_DIM, inv_dif, 0.0)
    sel_a = jnp.where(within < half, -1.0, 0.0)
    sel_b = jnp.where((within >= half) & (within < DIFF_ROT_DIM), 1.0, 0.0)
    rows = [inv_ret, sign_ret, inv_dif, sel_a, sel_b]
    rows += [jnp.zeros((HEAD_DIM,), F32)] * (F32_SUBLANES - len(rows))
    return jnp.stack(rows).astype(F32)


def _ret_kernel(q_ref, k_ref, v_ref, g_ref, gw_ref, o_ref, s_ref, const_ref, *, seq):
    C = RET_CHUNK
    nc = seq // C
    head = (pl.program_id(0) % N_RET_HEADS).astype(F32)
    ii = lax.broadcasted_iota(jnp.int32, (C, C), 0).astype(F32)
    jj = lax.broadcasted_iota(jnp.int32, (C, C), 1).astype(F32)
    lg = jnp.log(1.0 - jnp.exp2(jnp.zeros((C, C), F32) - 5.0 - head))
    rel = ii - jj
    DECAY, XI, ZETA, CHUNK_DECAY = range(4)
    const_ref[DECAY] = jnp.where(rel >= 0, jnp.exp(jnp.maximum(rel, 0.0) * lg), 0.0)
    const_ref[XI] = jnp.exp((ii + 1.0) * lg)
    const_ref[ZETA] = jnp.exp((C - 1.0 - ii) * lg)
    const_ref[CHUNK_DECAY] = jnp.exp(C * lg)

    def scan_body(c, state):
        rows = pl.ds(pl.multiple_of(c * C, C), C)
        s_ref[c] = state.astype(BF16)
        kz = (k_ref[rows, :].astype(F32) * const_ref[ZETA]).astype(BF16)
        kv = lax.dot_general(kz, v_ref[rows, :], (((0,), (0,)), ((), ())),
                             preferred_element_type=F32)
        return state * const_ref[CHUNK_DECAY] + kv
    lax.fori_loop(0, nc, scan_body, jnp.zeros((C, C), F32), unroll=32)

    gw = gw_ref[...]

    def out_body(c, carry):
        rows = pl.ds(pl.multiple_of(c * C, C), C)
        q = q_ref[rows, :]
        inner = lax.dot_general(q, k_ref[rows, :], (((1,), (1,)), ((), ())),
                                preferred_element_type=F32) * const_ref[DECAY]
        o = jnp.dot(inner.astype(BF16), v_ref[rows, :], preferred_element_type=F32)
        o = o + jnp.dot(q, s_ref[c], preferred_element_type=F32) * const_ref[XI]
        mu = jnp.mean(o, axis=-1, keepdims=True)
        d = o - mu
        var = jnp.mean(d * d, axis=-1, keepdims=True)
        y = d * lax.rsqrt(var + GN_EPS) * gw
        o_ref[rows, :] = (y * _silu(g_ref[rows, :].astype(F32))).astype(o_ref.dtype)
        return carry
    lax.fori_loop(0, nc, out_body, 0, unroll=32)


def _retention(proj, gn_w, *, batch, seq):
    nh = N_RET_HEADS
    blk = lambda off: pl.BlockSpec((seq, HEAD_DIM), lambda s: (s // nh, off + s % nh))
    return pl.pallas_call(
        functools.partial(_ret_kernel, seq=seq),
        out_shape=jax.ShapeDtypeStruct((batch * seq, nh * HEAD_DIM), BF16),
        grid=(batch * nh,),
        in_specs=[blk(0), blk(nh), blk(2 * nh), blk(3 * nh),
                  pl.BlockSpec((1, HEAD_DIM), lambda s: (0, s % nh))],
        out_specs=pl.BlockSpec((seq, HEAD_DIM), lambda s: (s // nh, s % nh)),
        scratch_shapes=[pltpu.VMEM((seq // RET_CHUNK, RET_CHUNK, HEAD_DIM), BF16),
                        pltpu.VMEM((4, RET_CHUNK, RET_CHUNK), F32)],
        compiler_params=pltpu.CompilerParams(
            dimension_semantics=("arbitrary",), vmem_limit_bytes=V7X_VMEM_LIMIT_BYTES),
        name="retention",
    )(proj, proj, proj, proj, gn_w)


_ATTN_STRIP = 256
_ATTN_SCORE_BUFS = 4


_ATTN_N_IN = 8


def _attn_kernel(*refs, blk, lam_init, n_cast):
    q_ref, k_ref, v_ref, lq1_ref, lk1_ref, lq2_ref, lk2_ref, sw_ref = refs[:_ATTN_N_IN]
    cast_src = refs[_ATTN_N_IN:_ATTN_N_IN + n_cast]
    o_ref = refs[_ATTN_N_IN + n_cast]
    cast_dst = refs[_ATTN_N_IN + n_cast + 1:_ATTN_N_IN + 2 * n_cast + 1]
    vt_ref, qt_ref, m_ref, l_ref, acc_ref, *score_bufs = refs[_ATTN_N_IN + 2 * n_cast + 1:]

    def side_work(part, n_parts):
        for src, dst in zip(cast_src, cast_dst):
            rows = src.shape[0] // n_parts
            dst[part * rows:(part + 1) * rows, :] = src[part * rows:(part + 1) * rows, :].astype(
                dst.dtype)

    _attn_body(q_ref, k_ref, v_ref, lq1_ref, lk1_ref, lq2_ref, lk2_ref, sw_ref, o_ref,
               vt_ref, qt_ref, m_ref, l_ref, acc_ref, tuple(score_bufs), side_work,
               blk=blk, lam_init=lam_init)


def _attn_body(q_ref, k_ref, v_ref, lq1_ref, lk1_ref, lq2_ref, lk2_ref, sw_ref, o_ref,
               vt_ref, qt_ref, m_ref, l_ref, acc_ref, score_bufs, side_work, *, blk, lam_init):
    nblk = vt_ref.shape[0]
    nstrip = blk // _ATTN_STRIP
    assert nblk % 2 == 0

    for jb in range(nblk):
        vt_ref[jb] = v_ref[jb * blk:(jb + 1) * blk, :].T

    feat = lax.broadcasted_iota(jnp.int32, (HEAD_DIM, _ATTN_STRIP), 0)
    for idx in range(nblk * nstrip):
        qt = q_ref[idx * _ATTN_STRIP:(idx + 1) * _ATTN_STRIP, :].T
        zero = jnp.zeros_like(qt)
        qt_ref[0, idx] = jnp.where(feat < DIFF_DQK, qt, zero)
        qt_ref[1, idx] = jnp.where(feat >= DIFF_DQK, qt, zero)

    chains = [(comp, s) for comp in range(2) for s in range(nstrip)]

    def keys_needed(s, diagonal):
        return (s + 1) * _ATTN_STRIP if diagonal else blk

    def scores(item, s_ref, diagonal=False):
        i, j = item
        jrow = j * blk
        if not isinstance(jrow, int):
            jrow = pl.multiple_of(jrow, blk)
        for c, (comp, s) in enumerate(chains):
            nk = keys_needed(s, diagonal)
            s_ref[c, :nk, :] = jnp.dot(k_ref[pl.ds(jrow, nk), :], qt_ref[comp, i * nstrip + s],
                                       preferred_element_type=F32)

    def softmax_av(item, s_ref, diagonal):
        i, j = item
        vtb = vt_ref[j]
        for c, (comp, s) in enumerate(chains):
            nk = keys_needed(s, diagonal)
            st = s_ref[c, :nk, :]
            idx = i * nstrip + s
            if diagonal:
                kpos = lax.broadcasted_iota(jnp.int32, st.shape, 0)
                qpos = lax.broadcasted_iota(jnp.int32, st.shape, 1) + s * _ATTN_STRIP
                st = jnp.where(kpos <= qpos, st, -jnp.inf)
                m_new = jnp.max(st, axis=0, keepdims=True)
                p = jnp.exp2(st - m_new)
                l_ref[comp, idx] = jnp.sum(p, axis=0, keepdims=True)
                acc_ref[comp, idx] = jnp.dot(vtb[:, :nk], p.astype(BF16),
                                             preferred_element_type=F32)
            else:
                m_old = m_ref[comp, idx]
                m_new = jnp.maximum(m_old, jnp.max(st, axis=0, keepdims=True))
                alpha = jnp.exp2(m_old - m_new)
                p = jnp.exp2(st - m_new)
                l_ref[comp, idx] = alpha * l_ref[comp, idx] + jnp.sum(p, axis=0, keepdims=True)
                acc_ref[comp, idx] = alpha * acc_ref[comp, idx] + jnp.dot(
                    vtb, p.astype(BF16), preferred_element_type=F32)
            m_ref[comp, idx] = m_new

    def next_item(item):
        i, j = item
        wrap = j + 1 >= i
        return jnp.where(wrap, i + 1, i), jnp.where(wrap, 0, j + 1)

    nbuf = len(score_bufs)
    n_off = nblk * (nblk - 1) // 2
    assert n_off % nbuf == 0
    buf = lambda n: score_bufs[n % nbuf]

    n_parts = nblk // 2
    scores((0, 0), buf(0), True)
    for n in range(nblk):
        if n % 2 == 0:
            side_work(n // 2, n_parts)
        if n + 1 < nblk:
            scores((n + 1, n + 1), buf(n + 1), True)
        else:
            scores((1, 0), buf(n + 1))
        softmax_av((n, n), buf(n), True)

    def rotation(t, item):
        items = [item]
        for _ in range(nbuf):
            items.append(next_item(items[-1]))
        for n in range(nbuf):
            scores(items[n + 1], buf(nblk + n + 1))
            softmax_av(items[n], buf(nblk + n), False)
        return items[-1]
    item = lax.fori_loop(0, n_off // nbuf - 1, rotation, (jnp.int32(1), jnp.int32(0)))
    items = [item]
    for _ in range(nbuf - 1):
        items.append(next_item(items[-1]))
    for n in range(nbuf):
        if n + 1 < nbuf:
            scores(items[n + 1], buf(nblk + n + 1))
        softmax_av(items[n], buf(nblk + n), False)

    lam = (jnp.exp(jnp.sum(lq1_ref[...] * lk1_ref[...], axis=-1, keepdims=True))
           - jnp.exp(jnp.sum(lq2_ref[...] * lk2_ref[...], axis=-1, keepdims=True))
           + lam_init)
    sw = sw_ref[...] * (1.0 - lam_init)

    for idx in range(nblk * nstrip):
        yt = acc_ref[0, idx] / l_ref[0, idx] - lam * (acc_ref[1, idx] / l_ref[1, idx])
        yt = yt * lax.rsqrt(jnp.mean(yt * yt, axis=0, keepdims=True) + GN_EPS)
        rows = slice(idx * _ATTN_STRIP, (idx + 1) * _ATTN_STRIP)
        o_ref[rows, :] = (yt.T * sw).astype(o_ref.dtype)


def _diff_attention(proj, lq1, lk1, lq2, lk2, subln_w, cast_weights, *, batch, seq, lam_init,
                    blk=512):
    nh = N_DIFF_HEADS
    nblk = seq // blk
    nchain = 2 * blk // _ATTN_STRIP
    nstrips = seq // _ATTN_STRIP
    steps = batch * nh
    q_off, k_off, v_off = 4 * nh, 5 * nh, 6 * nh
    head_blk = lambda off: pl.BlockSpec((seq, HEAD_DIM), lambda s: (s // nh, off + s % nh))
    lam_spec = pl.BlockSpec((1, DIFF_DQK), lambda s: (0, 0))
    cast_specs = [pl.BlockSpec((w.shape[0] // steps, w.shape[1]), lambda s: (s, 0))
                  for w in cast_weights]
    outs = pl.pallas_call(
        functools.partial(_attn_kernel, blk=blk, lam_init=lam_init, n_cast=len(cast_weights)),
        out_shape=[jax.ShapeDtypeStruct((batch * seq, nh * HEAD_DIM), BF16)]
        + [jax.ShapeDtypeStruct(w.shape, BF16) for w in cast_weights],
        grid=(steps,),
        in_specs=[
            head_blk(q_off), head_blk(k_off), head_blk(v_off),
            lam_spec, lam_spec, lam_spec, lam_spec,
            pl.BlockSpec((1, HEAD_DIM), lambda s: (0, s % nh)),
        ] + cast_specs,
        out_specs=[pl.BlockSpec((seq, HEAD_DIM), lambda s: (s // nh, s % nh))] + cast_specs,
        scratch_shapes=[
            pltpu.VMEM((nblk, HEAD_DIM, blk), BF16),
            pltpu.VMEM((2, nstrips, HEAD_DIM, _ATTN_STRIP), BF16),
            pltpu.VMEM((2, nstrips, 1, _ATTN_STRIP), F32),
            pltpu.VMEM((2, nstrips, 1, _ATTN_STRIP), F32),
            pltpu.VMEM((2, nstrips, HEAD_DIM, _ATTN_STRIP), F32),
        ] + [pltpu.VMEM((nchain, blk, _ATTN_STRIP), F32)] * _ATTN_SCORE_BUFS,
        compiler_params=pltpu.CompilerParams(
            dimension_semantics=("arbitrary",), vmem_limit_bytes=V7X_VMEM_LIMIT_BYTES),
        name="diff_attn",
    )(proj, proj, proj, lq1, lk1, lq2, lk2, subln_w, *cast_weights)
    return outs[0], outs[1:]


_OUT_RC = 512


def _outproj_kernel(yr_ref, yd_ref, w_ref, x_ref, g_ref, x1_ref, *, bm, wr):
    for r in range(bm // _OUT_RC):
        rows = slice(r * _OUT_RC, (r + 1) * _OUT_RC)
        mixed = (jnp.dot(yr_ref[rows, :], w_ref[:wr, :], preferred_element_type=F32)
                 + jnp.dot(yd_ref[rows, :], w_ref[wr:, :], preferred_element_type=F32))
        x1_ref[rows, :] = x_ref[rows, :] + g_ref[0] * mixed


def _outproj(y_ret, y_dif, w_out_bf16, x2d, mod3, *, seq, bm=512, bn=2048):
    bt, d = x2d.shape
    wr = y_ret.shape[1]
    wd = y_dif.shape[1]
    tiles_per_seq = seq // bm
    return pl.pallas_call(
        functools.partial(_outproj_kernel, bm=bm, wr=wr),
        out_shape=jax.ShapeDtypeStruct((bt, d), F32),
        grid=(bt // bm, d // bn),
        in_specs=[
            pl.BlockSpec((bm, wr), lambda m, n: (m, 0)),
            pl.BlockSpec((bm, wd), lambda m, n: (m, 0)),
            pl.BlockSpec((wr + wd, bn), lambda m, n: (0, n)),
            pl.BlockSpec((bm, bn), lambda m, n: (m, n)),
            pl.BlockSpec((1, 1, bn), lambda m, n: ((m // tiles_per_seq) * 6 + 2, 0, n)),
        ],
        out_specs=pl.BlockSpec((bm, bn), lambda m, n: (m, n)),
        compiler_params=pltpu.CompilerParams(
            dimension_semantics=("arbitrary", "arbitrary"), vmem_limit_bytes=V7X_VMEM_LIMIT_BYTES),
        name="outproj",
    )(y_ret, y_dif, w_out_bf16, x2d, mod3)


_MLP_RC = 512
_MLP_NORM_ROWS = 256
_MLP_FINAL_ROWS = 16


def _mlp_kernel(x1_ref, sh_ref, sc_ref, nw_ref, w1_ref, w2_ref, g_ref, fw_ref, o_ref, h_ref,
                *, bm, final_norm):
    f = pl.program_id(1)
    nf = pl.num_programs(1)

    def finish_rows(row0, nrows):
        for r in range(row0 // _MLP_FINAL_ROWS, (row0 + nrows) // _MLP_FINAL_ROWS):
            rows = slice(r * _MLP_FINAL_ROWS, (r + 1) * _MLP_FINAL_ROWS)
            x2 = x1_ref[rows, :] + g_ref[0] * o_ref[rows, :]
            if final_norm:
                ms = jnp.mean(x2 * x2, axis=-1, keepdims=True)
                x2 = x2 * lax.rsqrt(ms + NORM_EPS) * fw_ref[...]
            o_ref[rows, :] = x2

    def ffn(first, last):
        for r in range(bm // _MLP_RC):
            rows = slice(r * _MLP_RC, (r + 1) * _MLP_RC)
            a = jnp.dot(h_ref[rows, :], w1_ref[...], preferred_element_type=F32)
            a = jnp.maximum(a, 0.0)
            a = (a * a).astype(BF16)
            part = jnp.dot(a, w2_ref[...], preferred_element_type=F32)
            if first:
                o_ref[rows, :] = part
            else:
                o_ref[rows, :] += part
            if last:
                finish_rows(r * _MLP_RC, _MLP_RC)

    @pl.when(f == 0)
    def _():
        for r in range(bm // _MLP_NORM_ROWS):
            rows = slice(r * _MLP_NORM_ROWS, (r + 1) * _MLP_NORM_ROWS)
            x = x1_ref[rows, :]
            ms = jnp.mean(x * x, axis=-1, keepdims=True)
            y = x * lax.rsqrt(ms + NORM_EPS) * nw_ref[...]
            h_ref[rows, :] = (y * (1.0 + sc_ref[0]) + sh_ref[0]).astype(BF16)
        ffn(True, False)

    @pl.when(jnp.logical_and(f > 0, f < nf - 1))
    def _():
        ffn(False, False)

    @pl.when(f == nf - 1)
    def _():
        ffn(False, True)


def _mlp(x1, w1_bf16, w2_bf16, mod3, norm_w, final_w, *, seq, final_norm, bm=1024, bf=1024):
    bt, d = x1.shape
    dff = w1_bf16.shape[1]
    assert dff // bf >= 2
    tiles_per_seq = seq // bm
    mod_spec = lambda idx: pl.BlockSpec(
        (1, 1, d), lambda m, f: ((m // tiles_per_seq) * 6 + idx, 0, 0))
    return pl.pallas_call(
        functools.partial(_mlp_kernel, bm=bm, final_norm=final_norm),
        out_shape=jax.ShapeDtypeStruct((bt, d), F32),
        grid=(bt // bm, dff // bf),
        in_specs=[
            pl.BlockSpec((bm, d), lambda m, f: (m, 0)),
            mod_spec(3), mod_spec(4),
            pl.BlockSpec((1, d), lambda m, f: (0, 0)),
            pl.BlockSpec((d, bf), lambda m, f: (0, f)),
            pl.BlockSpec((bf, d), lambda m, f: (f, 0)),
            mod_spec(5),
            pl.BlockSpec((1, d), lambda m, f: (0, 0)),
        ],
        out_specs=pl.BlockSpec((bm, d), lambda m, f: (m, 0)),
        scratch_shapes=[pltpu.VMEM((bm, d), BF16)],
        compiler_params=pltpu.CompilerParams(
            dimension_semantics=("arbitrary", "arbitrary"), vmem_limit_bytes=V7X_VMEM_LIMIT_MLP_BYTES),
        name="mlp",
    )(x1, mod3, mod3, norm_w, w1_bf16, w2_bf16, mod3, final_w)


def kernel(x, c, w_ada, b_ada, norm1_w, norm2_w, w_in, ret_gn_w, diff_lq1, diff_lk1, diff_lq2,
           diff_lk2, diff_subln_w, w_out, w_mlp1, w_mlp2, final_norm_w):
    batch, seq, d = x.shape
    depth = w_ada.shape[0]
    c_pad = jnp.pad(c, ((0, F32_SUBLANES - batch), (0, 0)))
    rot = _rotary_lane_constants()
    x2d = x.reshape(batch * seq, d)
    final_w = final_norm_w.reshape(1, d)

    for l in range(depth):
        lam_init = 0.8 - 0.6 * math.exp(-0.3 * l)
        mod, tabs = _ada(c_pad, w_ada[l], b_ada[l].reshape(1, -1), rot, seq=seq)
        mod3 = mod[:batch].reshape(batch * 6, 1, d)
        proj = _proj(x2d, mod3, norm1_w[l].reshape(1, d), w_in[l], tabs, seq=seq)
        y_ret = _retention(proj, ret_gn_w[l].reshape(1, -1), batch=batch, seq=seq)
        y_dif, (w_out_b, w1_b, w2_b) = _diff_attention(
            proj, diff_lq1[l].reshape(1, -1), diff_lk1[l].reshape(1, -1),
            diff_lq2[l].reshape(1, -1), diff_lk2[l].reshape(1, -1),
            diff_subln_w[l].reshape(1, -1), (w_out[l], w_mlp1[l], w_mlp2[l]),
            batch=batch, seq=seq, lam_init=lam_init)
        x1 = _outproj(y_ret, y_dif, w_out_b, x2d, mod3, seq=seq)
        x2d = _mlp(x1, w1_b, w2_b, mod3, norm2_w[l].reshape(1, d), final_w, seq=seq,
                   final_norm=(l == depth - 1))
    if depth == 0:
        raise ValueError("depth must be positive")
    return x2d.reshape(batch, seq, d)
```

```python
import functools
import math

import jax
import jax.numpy as jnp
from jax import lax
from jax.experimental import pallas as pl
from jax.experimental.pallas import tpu as pltpu

F32 = jnp.float32
BF16 = jnp.bfloat16

HEAD_DIM = 128
N_RET_HEADS = 8
N_DIFF_HEADS = 8
DIFF_DQK = HEAD_DIM // 2
DIFF_ROT_DIM = DIFF_DQK // 4
RET_CHUNK = 128
ROPE_THETA = 500000.0
RET_THETA = 10000.0
NORM_EPS = 1e-6
GN_EPS = 1e-5

V7X_VMEM_LIMIT_BYTES = 58 * 1024 * 1024
V7X_VMEM_LIMIT_MLP_BYTES = 61 * 1024 * 1024
F32_SUBLANES = 8


def _silu(v):
    return v * jax.nn.sigmoid(v)


def _ada_kernel(c_ref, w_ref, b_ref, rot_ref, o_ref, ret_c_ref, ret_s_ref, dif_c_ref,
                dif_a_ref, dif_b_ref, base_ref, *, tab_rows):
    s = _silu(c_ref[...]).astype(BF16)
    part = jnp.dot(s, w_ref[...].astype(BF16), preferred_element_type=F32)

    @pl.when(pl.program_id(0) == 0)
    def _():
        o_ref[...] = part + b_ref[...]

    @pl.when(pl.program_id(0) > 0)
    def _():
        o_ref[...] += part

    freq = (rot_ref[0:1, :], rot_ref[2:3, :])

    @pl.when(pl.program_id(0) == 0)
    def _():
        r = lax.broadcasted_iota(jnp.int32, (tab_rows, HEAD_DIM), 0).astype(F32)
        for k, w in enumerate(freq):
            base_ref[2 * k] = jnp.cos(r * w)
            base_ref[2 * k + 1] = jnp.sin(r * w)

    p0 = jnp.full((F32_SUBLANES, HEAD_DIM), pl.program_id(0) * tab_rows, jnp.int32).astype(F32)
    cos_sin = []
    for k, w in enumerate(freq):
        off_c = jnp.cos(p0 * w)[0:1, :]
        off_s = jnp.sin(p0 * w)[0:1, :]
        base_c, base_s = base_ref[2 * k], base_ref[2 * k + 1]
        cos_sin.append((off_c * base_c - off_s * base_s, off_s * base_c + off_c * base_s))
    (cos_ret, sin_ret), (cos_dif, sin_dif) = cos_sin
    ret_c_ref[...] = cos_ret
    ret_s_ref[...] = sin_ret * rot_ref[1:2, :]
    dif_c_ref[...] = cos_dif
    dif_a_ref[...] = sin_dif * rot_ref[3:4, :]
    dif_b_ref[...] = sin_dif * rot_ref[4:5, :]


def _ada(c_pad, w_ada, b_ada, rot, *, seq, bk=128):
    rows, d = c_pad.shape
    n = w_ada.shape[1]
    steps = d // bk
    tab_rows = seq // steps
    assert steps * bk == d and steps * tab_rows == seq and tab_rows % F32_SUBLANES == 0
    tab_spec = pl.BlockSpec((tab_rows, HEAD_DIM), lambda j: (j, 0))
    outs = pl.pallas_call(
        functools.partial(_ada_kernel, tab_rows=tab_rows),
        out_shape=[jax.ShapeDtypeStruct((rows, n), F32)]
        + [jax.ShapeDtypeStruct((seq, HEAD_DIM), F32)] * 5,
        grid=(steps,),
        in_specs=[
            pl.BlockSpec((rows, bk), lambda j: (0, j)),
            pl.BlockSpec((bk, n), lambda j: (j, 0)),
            pl.BlockSpec((1, n), lambda j: (0, 0)),
            pl.BlockSpec(rot.shape, lambda j: (0, 0)),
        ],
        out_specs=[pl.BlockSpec((rows, n), lambda j: (0, 0))] + [tab_spec] * 5,
        scratch_shapes=[pltpu.VMEM((4, tab_rows, HEAD_DIM), F32)],
        compiler_params=pltpu.CompilerParams(
            dimension_semantics=("arbitrary",), vmem_limit_bytes=V7X_VMEM_LIMIT_BYTES),
        name="ada",
    )(c_pad, w_ada, b_ada, rot)
    return outs[0], tuple(outs[1:])


_PROJ_RC = 512
_PROJ_CC = 256
_PROJ_NORM_ROWS = 256


def _proj_kernel(x_ref, sh_ref, sc_ref, nw_ref, w_ref, rc_ref, rs_ref, dc_ref, da_ref, db_ref,
                 o_ref, h_ref, wb_ref, *, bm, bn):
    j = pl.program_id(1)
    n_rc = bm // _PROJ_RC

    def norm_modulate():
        for r in range(bm // _PROJ_NORM_ROWS):
            rows = slice(r * _PROJ_NORM_ROWS, (r + 1) * _PROJ_NORM_ROWS)
            x = x_ref[rows, :]
            ms = jnp.mean(x * x, axis=-1, keepdims=True)
            y = x * lax.rsqrt(ms + NORM_EPS) * nw_ref[...]
            h_ref[rows, :] = (y * (1.0 + sc_ref[0]) + sh_ref[0]).astype(BF16)

    def run(epilogue):
        for cc in range(bn // _PROJ_CC):
            cols = slice(cc * _PROJ_CC, (cc + 1) * _PROJ_CC)
            wb_ref[:, cols] = w_ref[:, cols].astype(BF16)
            for r in range(n_rc):
                rows = slice(r * _PROJ_RC, (r + 1) * _PROJ_RC)
                acc = jnp.dot(h_ref[rows, :], wb_ref[:, cols], preferred_element_type=F32)
                for hb in range(_PROJ_CC // HEAD_DIM):
                    a = acc[:, hb * HEAD_DIM:(hb + 1) * HEAD_DIM]
                    c0 = cc * _PROJ_CC + hb * HEAD_DIM
                    o_ref[rows, c0:c0 + HEAD_DIM] = epilogue(a, rows).astype(o_ref.dtype)

    def ret_rotary(a, rows):
        return a * rc_ref[rows, :] + pltpu.roll(a, HEAD_DIM // 2, 1) * rs_ref[rows, :]

    @pl.when(j == 0)
    def _():
        norm_modulate()
        run(ret_rotary)

    @pl.when(j == 1)
    def _():
        run(lambda a, rows: ret_rotary(a, rows) * (HEAD_DIM ** -0.5))

    @pl.when(jnp.logical_or(j == 4, j == 5))
    def _():
        scale = jnp.where(j == 4, DIFF_DQK ** -0.5 * math.log2(math.e), 1.0).astype(F32)
        half = DIFF_ROT_DIM // 2

        def ep(a, rows):
            rot = (a * dc_ref[rows, :]
                   + pltpu.roll(a, HEAD_DIM - half, 1) * da_ref[rows, :]
                   + pltpu.roll(a, half, 1) * db_ref[rows, :])
            return rot * scale
        run(ep)

    @pl.when(jnp.logical_or(jnp.logical_or(j == 2, j == 3), j == 6))
    def _():
        run(lambda a, rows: a)


def _proj(x2d, mod3, norm_w, w_in, tabs, *, seq, bm=1024, bn=1024):
    bt, d = x2d.shape
    n = w_in.shape[1]
    assert bn == N_RET_HEADS * HEAD_DIM == N_DIFF_HEADS * HEAD_DIM and n == 7 * bn
    tiles_per_seq = seq // bm
    n_m = bt // bm

    def tab_spec(last_use):
        return pl.BlockSpec((bm, HEAD_DIM), lambda m, j: (
            jnp.minimum(jnp.where(j > last_use, m + 1, m), n_m - 1) % tiles_per_seq, 0))
    return pl.pallas_call(
        functools.partial(_proj_kernel, bm=bm, bn=bn),
        out_shape=jax.ShapeDtypeStruct((bt, n), BF16),
        grid=(bt // bm, n // bn),
        in_specs=[
            pl.BlockSpec((bm, d), lambda m, j: (jnp.minimum(jnp.where(j > 0, m + 1, m), n_m - 1), 0)),
            pl.BlockSpec((1, 1, d), lambda m, j: ((m // tiles_per_seq) * 6 + 0, 0, 0)),
            pl.BlockSpec((1, 1, d), lambda m, j: ((m // tiles_per_seq) * 6 + 1, 0, 0)),
            pl.BlockSpec((1, d), lambda m, j: (0, 0)),
            pl.BlockSpec((d, bn), lambda m, j: (0, j)),
            tab_spec(1), tab_spec(1), tab_spec(5), tab_spec(5), tab_spec(5),
        ],
        out_specs=pl.BlockSpec((bm, bn), lambda m, j: (m, j)),
        scratch_shapes=[pltpu.VMEM((bm, d), BF16), pltpu.VMEM((d, bn), BF16)],
        compiler_params=pltpu.CompilerParams(
            dimension_semantics=("arbitrary", "arbitrary"), vmem_limit_bytes=V7X_VMEM_LIMIT_BYTES),
        name="proj",
    )(x2d, mod3, mod3, norm_w, w_in, *tabs)


def _rotary_lane_constants():
    lane = jnp.arange(HEAD_DIM)
    half = HEAD_DIM // 2
    inv_ret = RET_THETA ** (-(2 * (lane % half)).astype(F32) / HEAD_DIM)
    sign_ret = jnp.where(lane < half, -1.0, 1.0)
    half = DIFF_ROT_DIM // 2
    within = lane % DIFF_DQK
    inv_dif = ROPE_THETA ** (-(2 * (within % half)).astype(F32) / DIFF_ROT_DIM)
    inv_dif = jnp.where(within < DIFF_ROT_DIM, inv_dif, 0.0)
    sel_a = jnp.where(within < half, -1.0, 0.0)
    sel_b = jnp.where((within >= half) & (within < DIFF_ROT_DIM), 1.0, 0.0)
    rows = [inv_ret, sign_ret, inv_dif, sel_a, sel_b]
    rows += [jnp.zeros((HEAD_DIM,), F32)] * (F32_SUBLANES - len(rows))
    return jnp.stack(rows).astype(F32)


def _ret_kernel(q_ref, k_ref, v_ref, g_ref, gw_ref, o_ref, s_ref, const_ref, *, seq):
    C = RET_CHUNK
    nc = seq // C
    head = (pl.program_id(0) % N_RET_HEADS).astype(F32)
    ii = lax.broadcasted_iota(jnp.int32, (C, C), 0).astype(F32)
    jj = lax.broadcasted_iota(jnp.int32, (C, C), 1).astype(F32)
    lg = jnp.log(1.0 - jnp.exp2(jnp.zeros((C, C), F32) - 5.0 - head))
    rel = ii - jj
    DECAY, XI, ZETA, CHUNK_DECAY = range(4)
    const_ref[DECAY] = jnp.where(rel >= 0, jnp.exp(jnp.maximum(rel, 0.0) * lg), 0.0)
    const_ref[XI] = jnp.exp((ii + 1.0) * lg)
    const_ref[ZETA] = jnp.exp((C - 1.0 - ii) * lg)
    const_ref[CHUNK_DECAY] = jnp.exp(C * lg)

    def scan_body(c, state):
        rows = pl.ds(pl.multiple_of(c * C, C), C)
        s_ref[c] = state.astype(BF16)
        kz = (k_ref[rows, :].astype(F32) * const_ref[ZETA]).astype(BF16)
        kv = lax.dot_general(kz, v_ref[rows, :], (((0,), (0,)), ((), ())),
                             preferred_element_type=F32)
        return state * const_ref[CHUNK_DECAY] + kv
    lax.fori_loop(0, nc, scan_body, jnp.zeros((C, C), F32), unroll=32)

    gw = gw_ref[...]

    def out_body(c, carry):
        rows = pl.ds(pl.multiple_of(c * C, C), C)
        q = q_ref[rows, :]
        inner = lax.dot_general(q, k_ref[rows, :], (((1,), (1,)), ((), ())),
                                preferred_element_type=F32) * const_ref[DECAY]
        o = jnp.dot(inner.astype(BF16), v_ref[rows, :], preferred_element_type=F32)
        o = o + jnp.dot(q, s_ref[c], preferred_element_type=F32) * const_ref[XI]
        mu = jnp.mean(o, axis=-1, keepdims=True)
        d = o - mu
        var = jnp.mean(d * d, axis=-1, keepdims=True)
        y = d * lax.rsqrt(var + GN_EPS) * gw
        o_ref[rows, :] = (y * _silu(g_ref[rows, :].astype(F32))).astype(o_ref.dtype)
        return carry
    lax.fori_loop(0, nc, out_body, 0, unroll=32)


def _retention(proj, gn_w, *, batch, seq):
    nh = N_RET_HEADS
    blk = lambda off: pl.BlockSpec((seq, HEAD_DIM), lambda s: (s // nh, off + s % nh))
    return pl.pallas_call(
        functools.partial(_ret_kernel, seq=seq),
        out_shape=jax.ShapeDtypeStruct((batch * seq, nh * HEAD_DIM), BF16),
        grid=(batch * nh,),
        in_specs=[blk(0), blk(nh), blk(2 * nh), blk(3 * nh),
                  pl.BlockSpec((1, HEAD_DIM), lambda s: (0, s % nh))],
        out_specs=pl.BlockSpec((seq, HEAD_DIM), lambda s: (s // nh, s % nh)),
        scratch_shapes=[pltpu.VMEM((seq // RET_CHUNK, RET_CHUNK, HEAD_DIM), BF16),
                        pltpu.VMEM((4, RET_CHUNK, RET_CHUNK), F32)],
        compiler_params=pltpu.CompilerParams(
            dimension_semantics=("arbitrary",), vmem_limit_bytes=V7X_VMEM_LIMIT_BYTES),
        name="retention",
    )(proj, proj, proj, proj, gn_w)


_ATTN_STRIP = 256
_ATTN_SCORE_BUFS = 4


_ATTN_N_IN = 8


def _attn_kernel(*refs, blk, lam_init, n_cast):
    q_ref, k_ref, v_ref, lq1_ref, lk1_ref, lq2_ref, lk2_ref, sw_ref = refs[:_ATTN_N_IN]
    cast_src = refs[_ATTN_N_IN:_ATTN_N_IN + n_cast]
    o_ref = refs[_ATTN_N_IN + n_cast]
    cast_dst = refs[_ATTN_N_IN + n_cast + 1:_ATTN_N_IN + 2 * n_cast + 1]
    vt_ref, qt_ref, m_ref, l_ref, acc_ref, *score_bufs = refs[_ATTN_N_IN + 2 * n_cast + 1:]

    def side_work(part, n_parts):
        for src, dst in zip(cast_src, cast_dst):
            rows = src.shape[0] // n_parts
            dst[part * rows:(part + 1) * rows, :] = src[part * rows:(part + 1) * rows, :].astype(
                dst.dtype)

    _attn_body(q_ref, k_ref, v_ref, lq1_ref, lk1_ref, lq2_ref, lk2_ref, sw_ref, o_ref,
               vt_ref, qt_ref, m_ref, l_ref, acc_ref, tuple(score_bufs), side_work,
               blk=blk, lam_init=lam_init)


def _attn_body(q_ref, k_ref, v_ref, lq1_ref, lk1_ref, lq2_ref, lk2_ref, sw_ref, o_ref,
               vt_ref, qt_ref, m_ref, l_ref, acc_ref, score_bufs, side_work, *, blk, lam_init):
    nblk = vt_ref.shape[0]
    nstrip = blk // _ATTN_STRIP
    assert nblk % 2 == 0

    for jb in range(nblk):
        vt_ref[jb] = v_ref[jb * blk:(jb + 1) * blk, :].T

    feat = lax.broadcasted_iota(jnp.int32, (HEAD_DIM, _ATTN_STRIP), 0)
    for idx in range(nblk * nstrip):
        qt = q_ref[idx * _ATTN_STRIP:(idx + 1) * _ATTN_STRIP, :].T
        zero = jnp.zeros_like(qt)
        qt_ref[0, idx] = jnp.where(feat < DIFF_DQK, qt, zero)
        qt_ref[1, idx] = jnp.where(feat >= DIFF_DQK, qt, zero)

    chains = [(comp, s) for comp in range(2) for s in range(nstrip)]

    def keys_needed(s, diagonal):
        return (s + 1) * _ATTN_STRIP if diagonal else blk

    def scores(item, s_ref, diagonal=False):
        i, j = item
        jrow = j * blk
        if not isinstance(jrow, int):
            jrow = pl.multiple_of(jrow, blk)
        for c, (comp, s) in enumerate(chains):
            nk = keys_needed(s, diagonal)
            s_ref[c, :nk, :] = jnp.dot(k_ref[pl.ds(jrow, nk), :], qt_ref[comp, i * nstrip + s],
                                       preferred_element_type=F32)

    def softmax_av(item, s_ref, diagonal):
        i, j = item
        vtb = vt_ref[j]
        for c, (comp, s) in enumerate(chains):
            nk = keys_needed(s, diagonal)
            st = s_ref[c, :nk, :]
            idx = i * nstrip + s
            if diagonal:
                kpos = lax.broadcasted_iota(jnp.int32, st.shape, 0)
                qpos = lax.broadcasted_iota(jnp.int32, st.shape, 1) + s * _ATTN_STRIP
                st = jnp.where(kpos <= qpos, st, -jnp.inf)
                m_new = jnp.max(st, axis=0, keepdims=True)
                p = jnp.exp2(st - m_new)
                l_ref[comp, idx] = jnp.sum(p, axis=0, keepdims=True)
                acc_ref[comp, idx] = jnp.dot(vtb[:, :nk], p.astype(BF16),
                                             preferred_element_type=F32)
            else:
                m_old = m_ref[comp, idx]
                m_new = jnp.maximum(m_old, jnp.max(st, axis=0, keepdims=True))
                alpha = jnp.exp2(m_old - m_new)
                p = jnp.exp2(st - m_new)
                l_ref[comp, idx] = alpha * l_ref[comp, idx] + jnp.sum(p, axis=0, keepdims=True)
                acc_ref[comp, idx] = alpha * acc_ref[comp, idx] + jnp.dot(
                    vtb, p.astype(BF16), preferred_element_type=F32)
            m_ref[comp, idx] = m_new

    def next_item(item):
        i, j = item
        wrap = j + 1 >= i
        return jnp.where(wrap, i + 1, i), jnp.where(wrap, 0, j + 1)

    nbuf = len(score_bufs)
    n_off = nblk * (nblk - 1) // 2
    assert n_off % nbuf == 0
    buf = lambda n: score_bufs[n % nbuf]

    n_parts = nblk // 2
    scores((0, 0), buf(0), True)
    for n in range(nblk):
        if n % 2 == 0:
            side_work(n // 2, n_parts)
        if n + 1 < nblk:
            scores((n + 1, n + 1), buf(n + 1), True)
        else:
            scores((1, 0), buf(n + 1))
        softmax_av((n, n), buf(n), True)

    def rotation(t, item):
        items = [item]
        for _ in range(nbuf):
            items.append(next_item(items[-1]))
        for n in range(nbuf):
            scores(items[n + 1], buf(nblk + n + 1))
            softmax_av(items[n], buf(nblk + n), False)
        return items[-1]
    item = lax.fori_loop(0, n_off // nbuf - 1, rotation, (jnp.int32(1), jnp.int32(0)))
    items = [item]
    for _ in range(nbuf - 1):
        items.append(next_item(items[-1]))
    for n in range(nbuf):
        if n + 1 < nbuf:
            scores(items[n + 1], buf(nblk + n + 1))
        softmax_av(items[n], buf(nblk + n), False)

    lam = (jnp.exp(jnp.sum(lq1_ref[...] * lk1_ref[...], axis=-1, keepdims=True))
           - jnp.exp(jnp.sum(lq2_ref[...] * lk2_ref[...], axis=-1, keepdims=True))
           + lam_init)
    sw = sw_ref[...] * (1.0 - lam_init)

    for idx in range(nblk * nstrip):
        yt = acc_ref[0, idx] / l_ref[0, idx] - lam * (acc_ref[1, idx] / l_ref[1, idx])
        yt = yt * lax.rsqrt(jnp.mean(yt * yt, axis=0, keepdims=True) + GN_EPS)
        rows = slice(idx * _ATTN_STRIP, (idx + 1) * _ATTN_STRIP)
        o_ref[rows, :] = (yt.T * sw).astype(o_ref.dtype)


def _diff_attention(proj, lq1, lk1, lq2, lk2, subln_w, cast_weights, *, batch, seq, lam_init,
                    blk=512):
    nh = N_DIFF_HEADS
    nblk = seq // blk
    nchain = 2 * blk // _ATTN_STRIP
    nstrips = seq // _ATTN_STRIP
    steps = batch * nh
    q_off, k_off, v_off = 4 * nh, 5 * nh, 6 * nh
    head_blk = lambda off: pl.BlockSpec((seq, HEAD_DIM), lambda s: (s // nh, off + s % nh))
    lam_spec = pl.BlockSpec((1, DIFF_DQK), lambda s: (0, 0))
    cast_specs = [pl.BlockSpec((w.shape[0] // steps, w.shape[1]), lambda s: (s, 0))
                  for w in cast_weights]
    outs = pl.pallas_call(
        functools.partial(_attn_kernel, blk=blk, lam_init=lam_init, n_cast=len(cast_weights)),
        out_shape=[jax.ShapeDtypeStruct((batch * seq, nh * HEAD_DIM), BF16)]
        + [jax.ShapeDtypeStruct(w.shape, BF16) for w in cast_weights],
        grid=(steps,),
        in_specs=[
            head_blk(q_off), head_blk(k_off), head_blk(v_off),
            lam_spec, lam_spec, lam_spec, lam_spec,
            pl.BlockSpec((1, HEAD_DIM), lambda s: (0, s % nh)),
        ] + cast_specs,
        out_specs=[pl.BlockSpec((seq, HEAD_DIM), lambda s: (s // nh, s % nh))] + cast_specs,
        scratch_shapes=[
            pltpu.VMEM((nblk, HEAD_DIM, blk), BF16),
            pltpu.VMEM((2, nstrips, HEAD_DIM, _ATTN_STRIP), BF16),
            pltpu.VMEM((2, nstrips, 1, _ATTN_STRIP), F32),
            pltpu.VMEM((2, nstrips, 1, _ATTN_STRIP), F32),
            pltpu.VMEM((2, nstrips, HEAD_DIM, _ATTN_STRIP), F32),
        ] + [pltpu.VMEM((nchain, blk, _ATTN_STRIP), F32)] * _ATTN_SCORE_BUFS,
        compiler_params=pltpu.CompilerParams(
            dimension_semantics=("arbitrary",), vmem_limit_bytes=V7X_VMEM_LIMIT_BYTES),
        name="diff_attn",
    )(proj, proj, proj, lq1, lk1, lq2, lk2, subln_w, *cast_weights)
    return outs[0], outs[1:]


_OUT_RC = 512


def _outproj_kernel(yr_ref, yd_ref, w_ref, x_ref, g_ref, x1_ref, *, bm, wr):
    for r in range(bm // _OUT_RC):
        rows = slice(r * _OUT_RC, (r + 1) * _OUT_RC)
        mixed = (jnp.dot(yr_ref[rows, :], w_ref[:wr, :], preferred_element_type=F32)
                 + jnp.dot(yd_ref[rows, :], w_ref[wr:, :], preferred_element_type=F32))
        x1_ref[rows, :] = x_ref[rows, :] + g_ref[0] * mixed


def _outproj(y_ret, y_dif, w_out_bf16, x2d, mod3, *, seq, bm=512, bn=2048):
    bt, d = x2d.shape
    wr = y_ret.shape[1]
    wd = y_dif.shape[1]
    tiles_per_seq = seq // bm
    return pl.pallas_call(
        functools.partial(_outproj_kernel, bm=bm, wr=wr),
        out_shape=jax.ShapeDtypeStruct((bt, d), F32),
        grid=(bt // bm, d // bn),
        in_specs=[
            pl.BlockSpec((bm, wr), lambda m, n: (m, 0)),
            pl.BlockSpec((bm, wd), lambda m, n: (m, 0)),
            pl.BlockSpec((wr + wd, bn), lambda m, n: (0, n)),
            pl.BlockSpec((bm, bn), lambda m, n: (m, n)),
            pl.BlockSpec((1, 1, bn), lambda m, n: ((m // tiles_per_seq) * 6 + 2, 0, n)),
        ],
        out_specs=pl.BlockSpec((bm, bn), lambda m, n: (m, n)),
        compiler_params=pltpu.CompilerParams(
            dimension_semantics=("arbitrary", "arbitrary"), vmem_limit_bytes=V7X_VMEM_LIMIT_BYTES),
        name="outproj",
    )(y_ret, y_dif, w_out_bf16, x2d, mod3)


_MLP_RC = 512
_MLP_NORM_ROWS = 256
_MLP_FINAL_ROWS = 16


def _mlp_kernel(x1_ref, sh_ref, sc_ref, nw_ref, w1_ref, w2_ref, g_ref, fw_ref, o_ref, h_ref,
                *, bm, final_norm):
    f = pl.program_id(1)
    nf = pl.num_programs(1)

    def ffn(first):
        for r in range(bm // _MLP_RC):
            rows = slice(r * _MLP_RC, (r + 1) * _MLP_RC)
            a = jnp.dot(h_ref[rows, :], w1_ref[...], preferred_element_type=F32)
            a = jnp.maximum(a, 0.0)
            a = (a * a).astype(BF16)
            part = g_ref[0] * jnp.dot(a, w2_ref[...], preferred_element_type=F32)
            if first:
                o_ref[rows, :] = x1_ref[rows, :] + part
            else:
                o_ref[rows, :] += part

    @pl.when(f == 0)
    def _():
        for r in range(bm // _MLP_NORM_ROWS):
            rows = slice(r * _MLP_NORM_ROWS, (r + 1) * _MLP_NORM_ROWS)
            x = x1_ref[rows, :]
            ms = jnp.mean(x * x, axis=-1, keepdims=True)
            y = x * lax.rsqrt(ms + NORM_EPS) * nw_ref[...]
            h_ref[rows, :] = (y * (1.0 + sc_ref[0]) + sh_ref[0]).astype(BF16)
        ffn(True)

    @pl.when(f > 0)
    def _():
        ffn(False)

    @pl.when(f == nf - 1)
    def _():
        for r in range(bm // _MLP_FINAL_ROWS if final_norm else 0):
            rows = slice(r * _MLP_FINAL_ROWS, (r + 1) * _MLP_FINAL_ROWS)
            x2 = o_ref[rows, :]
            ms = jnp.mean(x2 * x2, axis=-1, keepdims=True)
            o_ref[rows, :] = x2 * lax.rsqrt(ms + NORM_EPS) * fw_ref[...]


def _mlp(x1, w1_bf16, w2_bf16, mod3, norm_w, final_w, *, seq, final_norm, bm=1024, bf=1024):
    bt, d = x1.shape
    dff = w1_bf16.shape[1]
    tiles_per_seq = seq // bm
    n_m = bt // bm
    mod_spec = lambda idx: pl.BlockSpec(
        (1, 1, d), lambda m, f: ((m // tiles_per_seq) * 6 + idx, 0, 0))
    return pl.pallas_call(
        functools.partial(_mlp_kernel, bm=bm, final_norm=final_norm),
        out_shape=jax.ShapeDtypeStruct((bt, d), F32),
        grid=(bt // bm, dff // bf),
        in_specs=[
            pl.BlockSpec((bm, d), lambda m, f: (jnp.minimum(jnp.where(f > 0, m + 1, m), n_m - 1), 0)),
            mod_spec(3), mod_spec(4),
            pl.BlockSpec((1, d), lambda m, f: (0, 0)),
            pl.BlockSpec((d, bf), lambda m, f: (0, f)),
            pl.BlockSpec((bf, d), lambda m, f: (f, 0)),
            mod_spec(5),
            pl.BlockSpec((1, d), lambda m, f: (0, 0)),
        ],
        out_specs=pl.BlockSpec((bm, d), lambda m, f: (m, 0)),
        scratch_shapes=[pltpu.VMEM((bm, d), BF16)],
        compiler_params=pltpu.CompilerParams(
            dimension_semantics=("arbitrary", "arbitrary"), vmem_limit_bytes=V7X_VMEM_LIMIT_MLP_BYTES),
        name="mlp",
    )(x1, mod3, mod3, norm_w, w1_bf16, w2_bf16, mod3, final_w)


def kernel(x, c, w_ada, b_ada, norm1_w, norm2_w, w_in, ret_gn_w, diff_lq1, diff_lk1, diff_lq2,
           diff_lk2, diff_subln_w, w_out, w_mlp1, w_mlp2, final_norm_w):
    batch, seq, d = x.shape
    depth = w_ada.shape[0]
    c_pad = jnp.pad(c, ((0, F32_SUBLANES - batch), (0, 0)))
    rot = _rotary_lane_constants()
    x2d = x.reshape(batch * seq, d)
    final_w = final_norm_w.reshape(1, d)

    for l in range(depth):
        lam_init = 0.8 - 0.6 * math.exp(-0.3 * l)
        mod, tabs = _ada(c_pad, w_ada[l], b_ada[l].reshape(1, -1), rot, seq=seq)
        mod3 = mod[:batch].reshape(batch * 6, 1, d)
        proj = _proj(x2d, mod3, norm1_w[l].reshape(1, d), w_in[l], tabs, seq=seq)
        y_ret = _retention(proj, ret_gn_w[l].reshape(1, -1), batch=batch, seq=seq)
        y_dif, (w_out_b, w1_b, w2_b) = _diff_attention(
            proj, diff_lq1[l].reshape(1, -1), diff_lk1[l].reshape(1, -1),
            diff_lq2[l].reshape(1, -1), diff_lk2[l].reshape(1, -1),
            diff_subln_w[l].reshape(1, -1), (w_out[l], w_mlp1[l], w_mlp2[l]),
            batch=batch, seq=seq, lam_init=lam_init)
        x1 = _outproj(y_ret, y_dif, w_out_b, x2d, mod3, seq=seq)
        x2d = _mlp(x1, w1_b, w2_b, mod3, norm2_w[l].reshape(1, d), final_w, seq=seq,
                   final_norm=(l == depth - 1))
    if depth == 0:
        raise ValueError("depth must be positive")
    return x2d.reshape(batch, seq, d)
```
